```python
import jax, jax.numpy as jnp
from jax import lax
import numpy as np

D_MODEL = 2048
BATCH = 4
SEQ = 8192
DEPTH = 2

CHUNK = 64
N_MIXERS = 2
PLE_DIM = 256
D_FF = 4 * D_MODEL
N_HEADS = 16
HEAD_DIM = 128
N_KV_HEADS = 4
GROUP = N_HEADS // N_KV_HEADS
N_IDX_HEADS = 8
IDX_DIM = 128
TOPK_MAX = 256
KEY_FRAC = 4
Q_BLOCK = 128
ROPE_THETA = 500000.0
ROPE_FRAC = 4
CONV_WIDTH = 31
NORM_EPS = 1e-6

Q_DIM = N_HEADS * HEAD_DIM
KV_DIM = N_KV_HEADS * HEAD_DIM
IDXQ_DIM = N_IDX_HEADS * IDX_DIM
DSA_IN = Q_DIM + 2 * KV_DIM + IDXQ_DIM + IDX_DIM + N_IDX_HEADS
DSA_SPLITS = [Q_DIM, Q_DIM + KV_DIM, Q_DIM + 2 * KV_DIM, Q_DIM + 2 * KV_DIM + IDXQ_DIM,
              Q_DIM + 2 * KV_DIM + IDXQ_DIM + IDX_DIM]

kernel_name = "hybrid_dsa_conformer_chunk_causal"


def rms_norm(x, g):
    x32 = x.astype(jnp.float32)
    y = x32 * lax.rsqrt(jnp.mean(x32 * x32, axis=-1, keepdims=True) + NORM_EPS) * g.astype(jnp.float32)
    return y.astype(x.dtype)


def layer_norm(x, g, b):
    x32 = x.astype(jnp.float32)
    mu = jnp.mean(x32, axis=-1, keepdims=True)
    var = jnp.mean(jnp.square(x32 - mu), axis=-1, keepdims=True)
    y = (x32 - mu) * lax.rsqrt(var + NORM_EPS) * g.astype(jnp.float32) + b.astype(jnp.float32)
    return y.astype(x.dtype)


def partial_rope(x, pos):
    rd = x.shape[-1] // ROPE_FRAC
    half = rd // 2
    inv = ROPE_THETA ** (-jnp.arange(half, dtype=jnp.float32) * 2.0 / rd)
    ang = pos[:, None] * inv[None, :]
    cos = jnp.cos(ang)[:, None, :].astype(x.dtype)
    sin = jnp.sin(ang)[:, None, :].astype(x.dtype)
    x1, x2, rest = x[..., :half], x[..., half:rd], x[..., rd:]
    return jnp.concatenate([x1 * cos - x2 * sin, x2 * cos + x1 * sin, rest], axis=-1)


def dsa_mixer(u, w_in, w_out, pos, topk):
    B, S, _ = u.shape
    proj = u @ w_in
    q, k, v, qi, ki, wi = jnp.split(proj, DSA_SPLITS, axis=-1)
    q = partial_rope(q.reshape(B, S, N_HEADS, HEAD_DIM), pos)
    k = partial_rope(k.reshape(B, S, N_KV_HEADS, HEAD_DIM), pos)
    v = v.reshape(B, S, N_KV_HEADS, HEAD_DIM)
    qi = partial_rope(qi.reshape(B, S, N_IDX_HEADS, IDX_DIM), pos) * (IDX_DIM ** -0.5)
    ki = partial_rope(ki[:, :, None, :], pos)[:, :, 0, :]
    wi = wi * (N_IDX_HEADS ** -0.5)

    nb = S // Q_BLOCK
    qb = q.reshape(B, nb, Q_BLOCK, N_KV_HEADS, GROUP, HEAD_DIM).transpose(1, 0, 2, 3, 4, 5)
    qib = qi.reshape(B, nb, Q_BLOCK, N_IDX_HEADS, IDX_DIM).transpose(1, 0, 2, 3, 4)
    wib = wi.reshape(B, nb, Q_BLOCK, N_IDX_HEADS).transpose(1, 0, 2, 3)
    key_chunk = jnp.arange(S) // CHUNK
    bidx = jnp.arange(B)[:, None, None]
    scale = HEAD_DIM ** -0.5

    def one_block(args):
        bi, q_blk, qi_blk, w_blk = args
        t = bi * Q_BLOCK + jnp.arange(Q_BLOCK)
        q_chunk = t // CHUNK
        rel = jax.nn.relu(jnp.einsum('bqhd,bsd->bqhs', qi_blk, ki).astype(jnp.float32))
        score = jnp.einsum('bqh,bqhs->bqs', w_blk.astype(jnp.float32), rel)
        admissible = key_chunk[None, :] <= q_chunk[:, None]
        score = jnp.where(admissible[None], score, -jnp.inf)
        _, idx = lax.top_k(score, topk)
        valid = (idx // CHUNK) <= q_chunk[None, :, None]
        k_sel = k[bidx, idx]
        v_sel = v[bidx, idx]
        s = jnp.einsum('bqgrd,bqkgd->bqgrk', q_blk, k_sel).astype(jnp.float32) * scale
        s = jnp.where(valid[:, :, None, None, :], s, -jnp.inf)
        prob = jax.nn.softmax(s, axis=-1).astype(v.dtype)
        return jnp.einsum('bqgrk,bqkgd->bqgrd', prob, v_sel)

    o = lax.map(one_block, (jnp.arange(nb), qb, qib, wib))
    o = o.transpose(1, 0, 2, 3, 4, 5).reshape(B, S, Q_DIM)
    return o @ w_out


def conformer_conv(u, w_in, b_in, w_dw, b_dw, ln_g, ln_b, w_out, b_out):
    D = u.shape[-1]
    a, g = jnp.split(u @ w_in + b_in, 2, axis=-1)
    y = a * jax.nn.sigmoid(g)
    y = lax.conv_general_dilated(
        y, w_dw[:, None, :].astype(y.dtype), window_strides=(1,),
        padding=[(CONV_WIDTH - 1, 0)],
        dimension_numbers=('NWC', 'WIO', 'NWC'),
        feature_group_count=D) + b_dw
    y = jax.nn.silu(layer_norm(y, ln_g, ln_b))
    return y @ w_out + b_out


def sqrelu_mlp(u, w1, w2):
    return jnp.square(jax.nn.relu(u @ w1)) @ w2


def setup_inputs(seed: int = 0) -> dict:
    key = jax.random.key(seed)
    ks = jax.random.split(key, 24)
    n_a = (DEPTH + 1) // 2
    n_b = DEPTH // 2
    f32 = jnp.float32

    def nrm(k, shape, scale):
        return jax.random.normal(k, shape, f32) * scale

    def gain(k, shape):
        return 1.0 + 0.01 * jax.random.normal(k, shape, f32)

    return {
        "x": jax.random.normal(ks[0], (BATCH, SEQ, D_MODEL), f32),
        "p": jax.random.normal(ks[1], (DEPTH, BATCH, SEQ, PLE_DIM), f32),
        "mix_norm": gain(ks[2], (DEPTH, D_MODEL)),
        "mlp_norm": gain(ks[3], (DEPTH, D_MODEL)),
        "mlp_w1": nrm(ks[4], (DEPTH, D_MODEL, D_FF), D_MODEL ** -0.5),
        "mlp_w2": nrm(ks[5], (DEPTH, D_FF, D_MODEL), D_FF ** -0.5),
        "pe_proj": nrm(ks[6], (DEPTH, PLE_DIM, D_MODEL), PLE_DIM ** -0.5),
        "pe_gate_norm": gain(ks[7], (DEPTH, D_MODEL)),
        "pe_gate": nrm(ks[8], (DEPTH, D_MODEL, D_MODEL), D_MODEL ** -0.5),
        "dsa_w_in": nrm(ks[9], (n_a, D_MODEL, DSA_IN), D_MODEL ** -0.5),
        "dsa_w_out": nrm(ks[10], (n_a, Q_DIM, D_MODEL), Q_DIM ** -0.5),
        "conv_w_in": nrm(ks[11], (n_b, D_MODEL, 2 * D_MODEL), D_MODEL ** -0.5),
        "conv_b_in": nrm(ks[12], (n_b, 2 * D_MODEL), 0.01),
        "conv_w_dw": nrm(ks[13], (n_b, CONV_WIDTH, D_MODEL), CONV_WIDTH ** -0.5),
        "conv_b_dw": nrm(ks[14], (n_b, D_MODEL), 0.01),
        "conv_ln_g": gain(ks[15], (n_b, D_MODEL)),
        "conv_ln_b": nrm(ks[16], (n_b, D_MODEL), 0.01),
        "conv_w_out": nrm(ks[17], (n_b, D_MODEL, D_MODEL), D_MODEL ** -0.5),
        "conv_b_out": nrm(ks[18], (n_b, D_MODEL), 0.01),
        "final_norm": gain(ks[19], (D_MODEL,)),
    }


def reference(x, p, mix_norm, mlp_norm, mlp_w1, mlp_w2, pe_proj, pe_gate_norm, pe_gate,
              dsa_w_in, dsa_w_out, conv_w_in, conv_b_in, conv_w_dw, conv_b_dw,
              conv_ln_g, conv_ln_b, conv_w_out, conv_b_out, final_norm):
    S = x.shape[1]
    topk = min(TOPK_MAX, S // KEY_FRAC)
    pos = jnp.arange(S, dtype=jnp.float32)
    h = x
    for i in range(DEPTH):
        j = i // N_MIXERS
        u = rms_norm(h, mix_norm[i])
        if i % N_MIXERS == 0:
            h = h + dsa_mixer(u, dsa_w_in[j], dsa_w_out[j], pos, topk)
        else:
            h = h + conformer_conv(u, conv_w_in[j], conv_b_in[j], conv_w_dw[j], conv_b_dw[j],
                                   conv_ln_g[j], conv_ln_b[j], conv_w_out[j], conv_b_out[j])
        h = h + sqrelu_mlp(rms_norm(h, mlp_norm[i]), mlp_w1[i], mlp_w2[i])
        gate = jax.nn.sigmoid(rms_norm(h, pe_gate_norm[i]) @ pe_gate[i])
        h = h + (p[i] @ pe_proj[i]) * gate
    return rms_norm(h, final_norm)
```

```python
import functools
import math

import jax
import jax.numpy as jnp
from jax import lax
from jax.experimental import pallas as pl
from jax.experimental.pallas import tpu as pltpu

F32 = jnp.float32
BF16 = jnp.bfloat16
I32 = jnp.int32

CHUNK = 64
N_HEADS = 16
HEAD_DIM = 128
N_KV_HEADS = 4
GROUP = N_HEADS // N_KV_HEADS
N_IDX_HEADS = 8
IDX_DIM = 128
TOPK_MAX = 256
KEY_FRAC = 4
ROPE_THETA = 500000.0
ROPE_FRAC = 4
CONV_WIDTH = 31
NORM_EPS = 1e-6
Q_DIM = N_HEADS * HEAD_DIM
KV_DIM = N_KV_HEADS * HEAD_DIM
IDXQ_DIM = N_IDX_HEADS * IDX_DIM

LANES = 128
INT_MIN = -(2 ** 31)
NEG_BIAS = -1e30
LOG2E = math.log2(math.e)

VMEM_LIMIT = 56 * 1024 * 1024


def _params(sem, vmem=VMEM_LIMIT):
    return pltpu.CompilerParams(dimension_semantics=sem, vmem_limit_bytes=vmem)


def _resident(shape):
    nd = len(shape)
    return pl.BlockSpec(shape, lambda *_: (0,) * nd, pipeline_mode=pl.Buffered(1))


def _rms(x, g):
    ms = jnp.mean(x * x, axis=-1, keepdims=True)
    return x * lax.rsqrt(ms + NORM_EPS) * g


_C_Q = 0
_C_K = _C_Q + Q_DIM
_C_QI = _C_K + KV_DIM
_C_KI = _C_QI + IDXQ_DIM
_C_V = _C_KI + IDX_DIM
_C_WI = _C_V + KV_DIM
_C_END = _C_WI + LANES


def _dsa_proj_kernel(h_ref, g_ref, w_ref, cos_ref, sa_ref, sb_ref,
                     q_ref, k_ref, v_ref, qi_ref, ki_ref, wi_ref):
    u = _rms(h_ref[...], g_ref[...]).astype(BF16)
    cos = cos_ref[...]
    sa = sa_ref[...]
    sb = sb_ref[...]
    rd = HEAD_DIM // ROPE_FRAC
    half = rd // 2

    def rope(y):
        return (y * cos + pltpu.roll(y, LANES - half, 1) * sa + pltpu.roll(y, half, 1) * sb)

    def proj(c0, n):
        return jnp.dot(u, w_ref[:, c0:c0 + n], preferred_element_type=F32)

    qscale = (HEAD_DIM ** -0.5) * LOG2E
    for c in range(N_HEADS // 4):
        y = proj(_C_Q + 512 * c, 512)
        for j in range(4):
            q_ref[4 * c + j] = (rope(y[:, 128 * j:128 * (j + 1)]) * qscale).astype(BF16)
    y = proj(_C_K, KV_DIM)
    for j in range(N_KV_HEADS):
        k_ref[:, 128 * j:128 * (j + 1)] = rope(y[:, 128 * j:128 * (j + 1)]).astype(BF16)
    for c in range(N_IDX_HEADS // 4):
        y = proj(_C_QI + 512 * c, 512)
        for j in range(4):
            qi_ref[4 * c + j] = (rope(y[:, 128 * j:128 * (j + 1)]) * (IDX_DIM ** -0.5)).astype(BF16)
    ki_ref[...] = rope(proj(_C_KI, IDX_DIM)).astype(BF16)
    v_ref[...] = proj(_C_V, KV_DIM).astype(BF16)
    wi_ref[...] = proj(_C_WI, LANES) * (N_IDX_HEADS ** -0.5)


def _dsa_proj(h, g, w_all, cos, sa, sb, seq, tm):
    t, d = h.shape
    nseq = seq // tm
    row = lambda i: (i, 0)
    tab = pl.BlockSpec((tm, LANES), lambda i: (i % nseq, 0))
    return pl.pallas_call(
        _dsa_proj_kernel,
        grid=(t // tm,),
        in_specs=[pl.BlockSpec((tm, d), row), _resident((1, d)), _resident(w_all.shape), tab, tab, tab],
        out_specs=[
            pl.BlockSpec((N_HEADS, tm, HEAD_DIM), lambda i: (0, i, 0)),
            pl.BlockSpec((tm, KV_DIM), row),
            pl.BlockSpec((tm, KV_DIM), row),
            pl.BlockSpec((N_IDX_HEADS, tm, IDX_DIM), lambda i: (0, i, 0)),
            pl.BlockSpec((tm, IDX_DIM), row),
            pl.BlockSpec((tm, LANES), row),
        ],
        out_shape=[
            jax.ShapeDtypeStruct((N_HEADS, t, HEAD_DIM), BF16),
            jax.ShapeDtypeStruct((t, KV_DIM), BF16),
            jax.ShapeDtypeStruct((t, KV_DIM), BF16),
            jax.ShapeDtypeStruct((N_IDX_HEADS, t, IDX_DIM), BF16),
            jax.ShapeDtypeStruct((t, IDX_DIM), BF16),
            jax.ShapeDtypeStruct((t, LANES), F32),
        ],
        compiler_params=_params(("arbitrary",)),
        name="dsa_proj",
    )(h, g, w_all, cos, sa, sb)


_RG = 64


def _topk_mask_kernel(qi_ref, ki_ref, wi_ref, bias_ref, sc_ref, wb_ref, *, tq, tk, topk, seq):
    i = pl.program_id(1)
    q0 = i * tq
    nkt = (q0 + tq + tk - 1) // tk
    nchunk = tk // LANES
    kf = float(topk)

    w = wi_ref[...]
    for h in range(N_IDX_HEADS):
        wb_ref[h] = jnp.broadcast_to(w[:, h:h + 1], (tq, LANES))
    qs = qi_ref[...].reshape(N_IDX_HEADS * tq, IDX_DIM)
    row = lax.broadcasted_iota(I32, (tq, LANES), 0)
    lane = lax.broadcasted_iota(I32, (tq, LANES), 1)
    adm_end = q0 + (row // CHUNK + 1) * CHUNK
    lane_rg = lax.broadcasted_iota(I32, (_RG, LANES), 1)

    def score_tile(kt, carry):
        kb = ki_ref[pl.ds(pl.multiple_of(kt * tk, tk), tk), :]
        r = lax.dot_general(qs, kb, (((1,), (1,)), ((), ())), preferred_element_type=F32)
        for c in range(nchunk):
            acc = None
            for h in range(N_IDX_HEADS):
                term = wb_ref[h] * jnp.maximum(r[h * tq:(h + 1) * tq, c * LANES:(c + 1) * LANES], 0.0)
                acc = term if acc is None else acc + term
            bits = pltpu.bitcast(acc, I32)
            key = bits ^ ((bits >> 31) & 0x7FFFFFFF)
            key = jnp.where(acc == 0.0, 0, key)
            kidx = kt * tk + c * LANES + lane
            sc_ref[kt, :, c * LANES:(c + 1) * LANES] = jnp.where(kidx < adm_end, key, INT_MIN)
        return carry

    lax.fori_loop(0, nkt, score_tile, 0)

    def fill_tile(kt, carry):
        bias_ref[0, kt] = jnp.full((tq, tk), NEG_BIAS, BF16)
        return carry

    lax.fori_loop(nkt, seq // tk, fill_tile, 0)

    for rg in range(tq // _RG):
        rows = slice(rg * _RG, (rg + 1) * _RG)

        def count(pred):
            def body(kt, acc):
                for c in range(nchunk):
                    key = sc_ref[kt, rows, c * LANES:(c + 1) * LANES]
                    kidx = kt * tk + c * LANES + lane_rg
                    acc = acc + jnp.where(pred(key, kidx), 1.0, 0.0)
                return acc
            acc = lax.fori_loop(0, nkt, body, jnp.zeros((_RG, LANES), F32))
            return jnp.broadcast_to(jnp.sum(acc, axis=1, keepdims=True), (_RG, LANES))

        def bit_pass(p, tau):
            cand = tau ^ (jnp.int32(1) << (31 - p))
            cnt = count(lambda key, kidx: key >= cand)
            return jnp.where(cnt >= kf, cand, tau)

        tau = lax.fori_loop(0, 32, bit_pass, jnp.full((_RG, LANES), INT_MIN, I32))
        tau = jnp.maximum(tau, INT_MIN + 1)
        cnt_ge = count(lambda key, kidx: key >= tau)
        has_tie = jnp.max(cnt_ge) > kf

        @pl.when(jnp.logical_not(has_tie))
        def _():
            def emit(kt, carry):
                for c in range(nchunk):
                    key = sc_ref[kt, rows, c * LANES:(c + 1) * LANES]
                    bias_ref[0, kt, rows, c * LANES:(c + 1) * LANES] = jnp.where(
                        key >= tau, 0.0, NEG_BIAS).astype(BF16)
                return carry
            lax.fori_loop(0, nkt, emit, 0)

        @pl.when(has_tie)
        def _():
            need = kf - count(lambda key, kidx: key > tau)
            nbits = max(1, (seq - 1).bit_length())

            def idx_pass(p, j0):
                cand = j0 + (jnp.int32(1) << (nbits - 1 - p))
                c_lt = count(lambda key, kidx: jnp.logical_and(key == tau, kidx < cand))
                return jnp.where(c_lt < need, cand, j0)

            j0 = lax.fori_loop(0, nbits, idx_pass, jnp.zeros((_RG, LANES), I32))

            def emit(kt, carry):
                for c in range(nchunk):
                    key = sc_ref[kt, rows, c * LANES:(c + 1) * LANES]
                    kidx = kt * tk + c * LANES + lane_rg
                    keep = jnp.logical_or(key > tau, jnp.logical_and(key == tau, kidx <= j0))
                    bias_ref[0, kt, rows, c * LANES:(c + 1) * LANES] = jnp.where(
                        keep, 0.0, NEG_BIAS).astype(BF16)
                return carry
            lax.fori_loop(0, nkt, emit, 0)


def _topk_mask(qi, ki, wi, batch, seq, topk, tq, tk):
    nq = seq // tq
    nk = seq // tk
    kern = functools.partial(_topk_mask_kernel, tq=tq, tk=tk, topk=topk, seq=seq)
    return pl.pallas_call(
        kern,
        grid=(batch, nq),
        in_specs=[
            pl.BlockSpec((N_IDX_HEADS, tq, IDX_DIM), lambda b, i: (0, b * nq + i, 0)),
            pl.BlockSpec((seq, IDX_DIM), lambda b, i: (b, 0)),
            pl.BlockSpec((tq, LANES), lambda b, i: (b * nq + i, 0)),
        ],
        out_specs=pl.BlockSpec((1, nk, tq, tk), lambda b, i: (b, 0, i, 0)),
        out_shape=jax.ShapeDtypeStruct((batch, nk, seq, tk), BF16),
        scratch_shapes=[
            pltpu.VMEM((nk, tq, tk), I32),
            pltpu.VMEM((N_IDX_HEADS, tq, LANES), F32),
        ],
        compiler_params=_params(("arbitrary", "arbitrary")),
        name="topk_mask",
    )(qi, ki, wi)


def _attn_kernel(q_ref, k_ref, v_ref, bias_ref, o_ref, m_ref, l_ref, acc_ref, *, tq, tk):
    i = pl.program_id(1)
    kt = pl.program_id(2)
    nkt = (i * tq + tq + tk - 1) // tk

    @pl.when(kt == 0)
    def _():
        m_ref[...] = jnp.full(m_ref.shape, NEG_BIAS, F32)
        l_ref[...] = jnp.zeros(l_ref.shape, F32)
        acc_ref[...] = jnp.zeros(acc_ref.shape, F32)

    @pl.when(kt < nkt)
    def _():
        bias = bias_ref[0, 0].astype(F32)
        for g in range(N_KV_HEADS):
            hs = slice(GROUP * g, GROUP * (g + 1))
            qs = q_ref[hs].reshape(GROUP * tq, HEAD_DIM)
            kg = k_ref[:, HEAD_DIM * g:HEAD_DIM * (g + 1)]
            s = lax.dot_general(qs, kg, (((1,), (1,)), ((), ())), preferred_element_type=F32)
            s = s.reshape(GROUP, tq, tk) + bias[None]
            m_prev = m_ref[hs]
            m_new = jnp.maximum(m_prev, jnp.max(s, axis=-1, keepdims=True))
            alpha = jnp.exp2(m_prev - m_new)
            p = jnp.exp2(s - m_new[:, :, :1])
            l_ref[hs] = alpha * l_ref[hs] + jnp.sum(p, axis=-1, keepdims=True)
            pv = jnp.dot(p.reshape(GROUP * tq, tk).astype(BF16), v_ref[:, HEAD_DIM * g:HEAD_DIM * (g + 1)],
                         preferred_element_type=F32)
            acc_ref[hs] = alpha * acc_ref[hs] + pv.reshape(GROUP, tq, HEAD_DIM)
            m_ref[hs] = m_new

    @pl.when(kt == pl.num_programs(2) - 1)
    def _():
        for h in range(N_HEADS):
            o_ref[:, HEAD_DIM * h:HEAD_DIM * (h + 1)] = (acc_ref[h] / l_ref[h]).astype(BF16)


def _attention(q, k, v, bias, batch, seq, tq, tk):
    nq = seq // tq
    nk = seq // tk
    t = batch * seq

    def last_tile(i):
        return (i * tq + tq + tk - 1) // tk - 1

    kern = functools.partial(_attn_kernel, tq=tq, tk=tk)
    return pl.pallas_call(
        kern,
        grid=(batch, nq, nk),
        in_specs=[
            pl.BlockSpec((N_HEADS, tq, HEAD_DIM), lambda b, i, j: (0, b * nq + i, 0)),
            pl.BlockSpec((tk, KV_DIM), lambda b, i, j: (b * nk + jnp.minimum(j, last_tile(i)), 0)),
            pl.BlockSpec((tk, KV_DIM), lambda b, i, j: (b * nk + jnp.minimum(j, last_tile(i)), 0)),
            pl.BlockSpec((1, 1, tq, tk), lambda b, i, j: (b, jnp.minimum(j, last_tile(i)), i, 0)),
        ],
        out_specs=pl.BlockSpec((tq, Q_DIM), lambda b, i, j: (b * nq + i, 0)),
        out_shape=jax.ShapeDtypeStruct((t, Q_DIM), BF16),
        scratch_shapes=[
            pltpu.VMEM((N_HEADS, tq, LANES), F32),
            pltpu.VMEM((N_HEADS, tq, LANES), F32),
            pltpu.VMEM((N_HEADS, tq, HEAD_DIM), F32),
        ],
        compiler_params=_params(("arbitrary", "arbitrary", "arbitrary")),
        name="masked_attention",
    )(q, k, v, bias)


def _proj_residual_kernel(h_ref, x_ref, w_ref, *rest, has_bias, tn):
    if has_bias:
        b_ref, o_ref = rest
    else:
        (o_ref,) = rest
    x = x_ref[...]
    for c in range(w_ref.shape[1] // tn):
        cols = slice(c * tn, (c + 1) * tn)
        y = h_ref[:, cols] + jnp.dot(x, w_ref[:, cols], preferred_element_type=F32)
        if has_bias:
            y = y + b_ref[:, cols]
        o_ref[:, cols] = y


def _proj_residual(h, x, w, b, tm, name):
    t, d = h.shape
    kdim = x.shape[1]
    row = lambda i: (i, 0)
    in_specs = [pl.BlockSpec((tm, d), row), pl.BlockSpec((tm, kdim), row), _resident(w.shape)]
    args = [h, x, w]
    if b is not None:
        in_specs.append(_resident((1, d)))
        args.append(b)
    kern = functools.partial(_proj_residual_kernel, has_bias=b is not None, tn=512)
    return pl.pallas_call(
        kern,
        grid=(t // tm,),
        in_specs=in_specs,
        out_specs=pl.BlockSpec((tm, d), row),
        out_shape=jax.ShapeDtypeStruct((t, d), F32),
        compiler_params=_params(("arbitrary",)),
        name=name,
    )(*args)


def _mlp_kernel(h_ref, g_ref, w1_ref, w2_ref, o_ref, u_ref):
    @pl.when(pl.program_id(1) == 0)
    def _():
        h = h_ref[...]
        u_ref[...] = _rms(h, g_ref[...]).astype(BF16)
        o_ref[...] = h

    a = jnp.dot(u_ref[...], w1_ref[...], preferred_element_type=F32)
    a = jnp.square(jnp.maximum(a, 0.0)).astype(BF16)
    o_ref[...] += jnp.dot(a, w2_ref[...], preferred_element_type=F32)


def _mlp(h, g, w1, w2, tm, tf):
    t, d = h.shape
    f = w1.shape[1]
    return pl.pallas_call(
        _mlp_kernel,
        grid=(t // tm, f // tf),
        in_specs=[
            pl.BlockSpec((tm, d), lambda i, j: (i, 0)),
            _resident((1, d)),
            pl.BlockSpec((d, tf), lambda i, j: (0, j)),
            pl.BlockSpec((tf, d), lambda i, j: (j, 0)),
        ],
        out_specs=pl.BlockSpec((tm, d), lambda i, j: (i, 0)),
        out_shape=jax.ShapeDtypeStruct((t, d), F32),
        scratch_shapes=[pltpu.VMEM((tm, d), BF16)],
        compiler_params=_params(("arbitrary", "arbitrary")),
        name="sqrelu_mlp",
    )(h, g, w1, w2)


def _pe_kernel(h_ref, g_ref, wg_ref, p_ref, wp_ref, *rest, final, tn):
    if final:
        gf_ref, o_ref = rest
    else:
        (o_ref,) = rest
    h = h_ref[...]
    u = _rms(h, g_ref[...]).astype(BF16)
    pb = p_ref[...].astype(BF16)
    for c in range(wg_ref.shape[1] // tn):
        cols = slice(c * tn, (c + 1) * tn)
        gate = jax.nn.sigmoid(jnp.dot(u, wg_ref[:, cols], preferred_element_type=F32))
        emb = jnp.dot(pb, wp_ref[:, cols], preferred_element_type=F32)
        o_ref[:, cols] = h_ref[:, cols] + emb * gate
    if final:
        o_ref[...] = _rms(o_ref[...], gf_ref[...])


def _pe(h, g, wg, p, wp, gf, tm):
    t, d = h.shape
    row = lambda i: (i, 0)
    in_specs = [pl.BlockSpec((tm, d), row), _resident((1, d)), _resident(wg.shape),
                pl.BlockSpec((tm, p.shape[1]), row), _resident(wp.shape)]
    args = [h, g, wg, p, wp]
    if gf is not None:
        in_specs.append(_resident((1, d)))
        args.append(gf)
    kern = functools.partial(_pe_kernel, final=gf is not None, tn=512)
    return pl.pallas_call(
        kern,
        grid=(t // tm,),
        in_specs=in_specs,
        out_specs=pl.BlockSpec((tm, d), row),
        out_shape=jax.ShapeDtypeStruct((t, d), F32),
        compiler_params=_params(("arbitrary",)),
        name="gated_embedding",
    )(*args)


def _conv_in_kernel(h_ref, g_ref, w_ref, b_ref, y_ref, *, tn):
    d = y_ref.shape[1]
    u = _rms(h_ref[...], g_ref[...]).astype(BF16)
    for c in range(d // tn):
        a = jnp.dot(u, w_ref[:, c * tn:(c + 1) * tn], preferred_element_type=F32) + b_ref[:, c * tn:(c + 1) * tn]
        gt = (jnp.dot(u, w_ref[:, d + c * tn:d + (c + 1) * tn], preferred_element_type=F32)
              + b_ref[:, d + c * tn:d + (c + 1) * tn])
        y_ref[:, c * tn:(c + 1) * tn] = a * jax.nn.sigmoid(gt)


def _conv_in(h, g, w, b, tm):
    t, d = h.shape
    row = lambda i: (i, 0)
    return pl.pallas_call(
        functools.partial(_conv_in_kernel, tn=512),
        grid=(t // tm,),
        in_specs=[pl.BlockSpec((tm, d), row), _resident((1, d)), _resident(w.shape), _resident(b.shape)],
        out_specs=pl.BlockSpec((tm, d), row),
        out_shape=jax.ShapeDtypeStruct((t, d), F32),
        compiler_params=_params(("arbitrary",)),
        name="conv_in_glu",
    )(h, g, w, b)


_HALO = 32
_CONV_RS = 64


def _dwconv_kernel(y_ref, halo_ref, w_ref, bdw_ref, lg_ref, lb_ref, z_ref, ext_ref, cv_ref, *, tm):
    d = y_ref.shape[2]
    nc = d // LANES
    first = pl.program_id(1) == 0
    for c in range(nc):
        cols = slice(c * LANES, (c + 1) * LANES)
        ext_ref[c, 0:_HALO] = jnp.where(first, 0.0, halo_ref[0, :, cols])
        ext_ref[c, _HALO:_HALO + tm] = y_ref[0, :, cols]

    off = _HALO - (CONV_WIDTH - 1)

    def chunk(c, carry):
        for rs in range(tm // _CONV_RS):
            r0 = rs * _CONV_RS
            acc = jnp.broadcast_to(bdw_ref[c], (_CONV_RS, LANES))
            for j in range(CONV_WIDTH):
                acc = acc + w_ref[c, j:j + 1, :] * ext_ref[c, r0 + off + j:r0 + off + j + _CONV_RS, :]
            cv_ref[c, r0:r0 + _CONV_RS] = acc
        return carry

    lax.fori_loop(0, nc, chunk, 0)

    s1 = cv_ref[0]
    for c in range(1, nc):
        s1 = s1 + cv_ref[c]
    mu = jnp.sum(s1, axis=1, keepdims=True) / d
    s2 = jnp.square(cv_ref[0] - mu)
    for c in range(1, nc):
        s2 = s2 + jnp.square(cv_ref[c] - mu)
    rstd = lax.rsqrt(jnp.sum(s2, axis=1, keepdims=True) / d + NORM_EPS)
    for c in range(nc):
        yn = (cv_ref[c] - mu) * rstd * lg_ref[c] + lb_ref[c]
        z_ref[0, :, c * LANES:(c + 1) * LANES] = (yn * jax.nn.sigmoid(yn)).astype(BF16)


def _dwconv_ln_swish(y3, w3, bdw3, lg3, lb3, tm):
    batch, seq, d = y3.shape
    nc = d // LANES
    hb = tm // _HALO
    kern = functools.partial(_dwconv_kernel, tm=tm)
    return pl.pallas_call(
        kern,
        grid=(batch, seq // tm),
        in_specs=[
            pl.BlockSpec((1, tm, d), lambda b, i: (b, i, 0)),
            pl.BlockSpec((1, _HALO, d), lambda b, i: (b, jnp.maximum(i * hb - 1, 0), 0)),
            _resident(w3.shape), _resident(bdw3.shape), _resident(lg3.shape), _resident(lb3.shape),
        ],
        out_specs=pl.BlockSpec((1, tm, d), lambda b, i: (b, i, 0)),
        out_shape=jax.ShapeDtypeStruct((batch, seq, d), BF16),
        scratch_shapes=[
            pltpu.VMEM((nc, _HALO + tm, LANES), F32),
            pltpu.VMEM((nc, tm, LANES), F32),
        ],
        compiler_params=_params(("arbitrary", "arbitrary")),
        name="dwconv_ln_swish",
    )(y3, y3, w3, bdw3, lg3, lb3)


def _rope_tables(seq):
    rd = HEAD_DIM // ROPE_FRAC
    half = rd // 2
    pos = jnp.arange(seq, dtype=F32)
    inv = ROPE_THETA ** (-jnp.arange(half, dtype=F32) * 2.0 / rd)
    ang = pos[:, None] * inv[None, :]
    cos = jnp.cos(ang)
    sin = jnp.sin(ang)
    ones = jnp.ones((seq, HEAD_DIM - rd), F32)
    zeros = jnp.zeros((seq, HEAD_DIM - rd), F32)
    zh = jnp.zeros((seq, half), F32)
    cos_t = jnp.concatenate([cos, cos, ones], axis=1)
    sa_t = jnp.concatenate([-sin, zh, zeros], axis=1)
    sb_t = jnp.concatenate([zh, sin, zeros], axis=1)
    return cos_t, sa_t, sb_t


def _chunked(v, nc):
    v2 = v.reshape(-1, v.shape[-1])
    return v2.reshape(v2.shape[0], nc, LANES).transpose(1, 0, 2)


def kernel(x, p, mix_norm, mlp_norm, mlp_w1, mlp_w2, pe_proj, pe_gate_norm, pe_gate, dsa_w_in, dsa_w_out, conv_w_in, conv_b_in, conv_w_dw, conv_b_dw, conv_ln_g, conv_ln_b, conv_w_out, conv_b_out, final_norm):
    batch, seq, d = x.shape
    depth = p.shape[0]
    t = batch * seq
    topk = min(TOPK_MAX, seq // KEY_FRAC)
    tm = 512
    h = x.reshape(t, d)
    row_vec = lambda v: v.reshape(1, -1)

    for i in range(depth):
        j = i // 2
        if i % 2 == 0:
            w = dsa_w_in[j]
            wq, wk, wv, wqi, wki, wwi = jnp.split(
                w, [Q_DIM, Q_DIM + KV_DIM, Q_DIM + 2 * KV_DIM, Q_DIM + 2 * KV_DIM + IDXQ_DIM,
                    Q_DIM + 2 * KV_DIM + IDXQ_DIM + IDX_DIM], axis=1)
            wwi = jnp.pad(wwi, ((0, 0), (0, LANES - N_IDX_HEADS)))
            w_all = jnp.concatenate([wq, wk, wqi, wki, wv, wwi], axis=1).astype(BF16)
            cos_t, sa_t, sb_t = _rope_tables(seq)
            q, k, v, qi, ki, wi = _dsa_proj(h, row_vec(mix_norm[i]), w_all, cos_t, sa_t, sb_t, seq, tm)
            bias = _topk_mask(qi, ki, wi, batch, seq, topk, tq=256, tk=512)
            o = _attention(q, k, v, bias, batch, seq, tq=512, tk=512)
            h = _proj_residual(h, o, dsa_w_out[j].astype(BF16), None, tm, "dsa_out")
        else:
            nc = d // LANES
            y = _conv_in(h, row_vec(mix_norm[i]), conv_w_in[j].astype(BF16), row_vec(conv_b_in[j]), tm)
            w3 = _chunked(jnp.pad(conv_w_dw[j], ((0, _HALO - CONV_WIDTH), (0, 0))), nc)
            z = _dwconv_ln_swish(y.reshape(batch, seq, d), w3, _chunked(conv_b_dw[j], nc),
                                 _chunked(conv_ln_g[j], nc), _chunked(conv_ln_b[j], nc), tm=256)
            h = _proj_residual(h, z.reshape(t, d), conv_w_out[j].astype(BF16), row_vec(conv_b_out[j]), tm,
                               "conv_out")
        h = _mlp(h, row_vec(mlp_norm[i]), mlp_w1[i].astype(BF16), mlp_w2[i].astype(BF16), tm=512, tf=1024)
        gf = row_vec(final_norm) if i == depth - 1 else None
        h = _pe(h, row_vec(pe_gate_norm[i]), pe_gate[i].astype(BF16), p[i].reshape(t, -1),
                pe_proj[i].astype(BF16), gf, tm)
    return h.reshape(batch, seq, d)
```

```python
import functools
import math

import jax
import jax.numpy as jnp
from jax import lax
from jax.experimental import pallas as pl
from jax.experimental.pallas import tpu as pltpu

F32 = jnp.float32
BF16 = jnp.bfloat16
I32 = jnp.int32

CHUNK = 64
N_HEADS = 16
HEAD_DIM = 128
N_KV_HEADS = 4
GROUP = N_HEADS // N_KV_HEADS
N_IDX_HEADS = 8
IDX_DIM = 128
TOPK_MAX = 256
KEY_FRAC = 4
ROPE_THETA = 500000.0
ROPE_FRAC = 4
CONV_WIDTH = 31
NORM_EPS = 1e-6
Q_DIM = N_HEADS * HEAD_DIM
KV_DIM = N_KV_HEADS * HEAD_DIM
IDXQ_DIM = N_IDX_HEADS * IDX_DIM

LANES = 128
SUBLANES = 8
INT_MIN = -(2 ** 31)
NEG_BIAS = -1e30
LOG2E = math.log2(math.e)

VMEM_LIMIT = 56 * 1024 * 1024


def _params(sem, vmem=VMEM_LIMIT):
    return pltpu.CompilerParams(dimension_semantics=sem, vmem_limit_bytes=vmem)


def _resident(shape):
    nd = len(shape)
    return pl.BlockSpec(shape, lambda *_: (0,) * nd, pipeline_mode=pl.Buffered(1))


def _rms(x, g):
    ms = jnp.mean(x * x, axis=-1, keepdims=True)
    return x * lax.rsqrt(ms + NORM_EPS) * g


_C_Q = 0
_C_K = _C_Q + Q_DIM
_C_QI = _C_K + KV_DIM
_C_KI = _C_QI + IDXQ_DIM
_C_V = _C_KI + IDX_DIM
_C_WI = _C_V + KV_DIM
_C_END = _C_WI + LANES


def _dsa_proj_kernel(h_ref, g_ref, w_ref, cos_ref, sa_ref, sb_ref,
                     qt_ref, k_ref, vt_ref, qit_ref, ki_ref, wit_ref):
    u = _rms(h_ref[...], g_ref[...]).astype(BF16)
    cos = cos_ref[...]
    sa = sa_ref[...]
    sb = sb_ref[...]
    rd = HEAD_DIM // ROPE_FRAC
    half = rd // 2

    def rope(y):
        return (y * cos + pltpu.roll(y, LANES - half, 1) * sa + pltpu.roll(y, half, 1) * sb)

    def proj(c0, n):
        return jnp.dot(u, w_ref[:, c0:c0 + n], preferred_element_type=F32)

    qscale = (HEAD_DIM ** -0.5) * LOG2E
    for c in range(N_HEADS // 4):
        y = proj(_C_Q + 512 * c, 512)
        for j in range(4):
            qt_ref[4 * c + j] = (rope(y[:, 128 * j:128 * (j + 1)]) * qscale).T.astype(BF16)
    y = proj(_C_K, KV_DIM)
    for j in range(N_KV_HEADS):
        k_ref[:, 128 * j:128 * (j + 1)] = rope(y[:, 128 * j:128 * (j + 1)]).astype(BF16)
    for c in range(N_IDX_HEADS // 4):
        y = proj(_C_QI + 512 * c, 512)
        for j in range(4):
            qit_ref[4 * c + j] = (rope(y[:, 128 * j:128 * (j + 1)]) * (IDX_DIM ** -0.5)).T.astype(BF16)
    ki_ref[...] = rope(proj(_C_KI, IDX_DIM)).astype(BF16)
    y = proj(_C_V, KV_DIM)
    for j in range(N_KV_HEADS):
        vt_ref[128 * j:128 * (j + 1), :] = y[:, 128 * j:128 * (j + 1)].T.astype(BF16)
    wit_ref[...] = (proj(_C_WI, LANES) * (N_IDX_HEADS ** -0.5)).T[:N_IDX_HEADS]


def _dsa_proj(h, g, w_all, cos, sa, sb, seq, tm):
    t, d = h.shape
    nseq = seq // tm
    row = lambda i: (i, 0)
    tab = pl.BlockSpec((tm, LANES), lambda i: (i % nseq, 0))
    return pl.pallas_call(
        _dsa_proj_kernel,
        grid=(t // tm,),
        in_specs=[pl.BlockSpec((tm, d), row), _resident((1, d)), _resident(w_all.shape), tab, tab, tab],
        out_specs=[
            pl.BlockSpec((N_HEADS, HEAD_DIM, tm), lambda i: (0, 0, i)),
            pl.BlockSpec((tm, KV_DIM), row),
            pl.BlockSpec((KV_DIM, tm), lambda i: (0, i)),
            pl.BlockSpec((N_IDX_HEADS, IDX_DIM, tm), lambda i: (0, 0, i)),
            pl.BlockSpec((tm, IDX_DIM), row),
            pl.BlockSpec((N_IDX_HEADS, tm), lambda i: (0, i)),
        ],
        out_shape=[
            jax.ShapeDtypeStruct((N_HEADS, HEAD_DIM, t), BF16),
            jax.ShapeDtypeStruct((t, KV_DIM), BF16),
            jax.ShapeDtypeStruct((KV_DIM, t), BF16),
            jax.ShapeDtypeStruct((N_IDX_HEADS, IDX_DIM, t), BF16),
            jax.ShapeDtypeStruct((t, IDX_DIM), BF16),
            jax.ShapeDtypeStruct((N_IDX_HEADS, t), F32),
        ],
        compiler_params=_params(("arbitrary",)),
        name="dsa_proj",
    )(h, g, w_all, cos, sa, sb)


_KT = 256


def _topk_mask_kernel(qit_ref, ki_ref, wit_ref, bias_ref, sc_ref, *, tq, tk, topk, seq):
    i = pl.program_id(1)
    q0 = i * tq
    n_adm = q0 + tq
    nkt = (n_adm + tk - 1) // tk
    nct = n_adm // _KT
    kf = float(topk)
    lane = lax.broadcasted_iota(I32, (1, tq), 1)
    adm_end = q0 + (lane // CHUNK + 1) * CHUNK

    def score_tile(kt, carry):
        k0 = pl.multiple_of(kt * tk, tk)
        kb = ki_ref[pl.ds(k0, tk), :]
        r = jnp.dot(kb, qit_ref[0], preferred_element_type=F32)
        acc = None
        for h in range(N_IDX_HEADS):
            r_next = (jnp.dot(kb, qit_ref[h + 1], preferred_element_type=F32)
                      if h + 1 < N_IDX_HEADS else None)
            term = wit_ref[h:h + 1, :] * jnp.maximum(r, 0.0)
            acc = term if acc is None else acc + term
            r = r_next
        bits = pltpu.bitcast(acc, I32)
        key = bits ^ ((bits >> 31) & 0x7FFFFFFF)
        key = jnp.where(acc == 0.0, 0, key)
        kidx = k0 + lax.broadcasted_iota(I32, (tk, tq), 0)
        sc_ref[pl.ds(k0, tk), :] = jnp.where(kidx < adm_end, key, INT_MIN)
        return carry

    lax.fori_loop(0, nkt, score_tile, 0)

    row_kt = lax.broadcasted_iota(I32, (_KT, tq), 0)

    def count(pred):
        def body(c, acc):
            r0 = pl.multiple_of(c * _KT, _KT)
            key = sc_ref[pl.ds(r0, _KT), :]
            ones = jnp.where(pred(key, r0 + row_kt), 1.0, 0.0)
            return acc + jnp.sum(ones.reshape(_KT // SUBLANES, SUBLANES, tq), axis=0)
        acc = lax.fori_loop(0, nct, body, jnp.zeros((SUBLANES, tq), F32))
        return jnp.sum(acc, axis=0, keepdims=True)

    def emit(keep):
        def body(c, carry):
            r0 = pl.multiple_of(c * _KT, _KT)
            key = sc_ref[pl.ds(r0, _KT), :]
            bias_ref[0, 0, pl.ds(r0, _KT), :] = jnp.where(keep(key, r0 + row_kt), 0.0, NEG_BIAS).astype(BF16)
            return carry
        lax.fori_loop(0, nct, body, 0)

    def bit_pass(p, tau):
        cand = tau ^ (jnp.int32(1) << (31 - p))
        cnt = count(lambda key, kidx: key >= cand)
        return jnp.where(cnt >= kf, cand, tau)

    tau = lax.fori_loop(0, 32, bit_pass, jnp.full((1, tq), INT_MIN, I32))
    tau = jnp.maximum(tau, INT_MIN + 1)
    cnt_ge = count(lambda key, kidx: key >= tau)
    has_tie = jnp.max(cnt_ge) > kf

    def fill(c, carry):
        r0 = pl.multiple_of(c * _KT, _KT)
        bias_ref[0, 0, pl.ds(r0, _KT), :] = jnp.full((_KT, tq), NEG_BIAS, BF16)
        return carry

    lax.fori_loop(nct, seq // _KT, fill, 0)

    @pl.when(jnp.logical_not(has_tie))
    def _():
        emit(lambda key, kidx: key >= tau)

    @pl.when(has_tie)
    def _():
        need = kf - count(lambda key, kidx: key > tau)
        nbits = max(1, (seq - 1).bit_length())

        def idx_pass(p, j0):
            cand = j0 + (jnp.int32(1) << (nbits - 1 - p))
            c_lt = count(lambda key, kidx: jnp.logical_and(key == tau, kidx < cand))
            return jnp.where(c_lt < need, cand, j0)

        j0 = lax.fori_loop(0, nbits, idx_pass, jnp.zeros((1, tq), I32))
        emit(lambda key, kidx: jnp.logical_or(key > tau, jnp.logical_and(key == tau, kidx <= j0)))


def _topk_mask(qit, ki, wit, batch, seq, topk, tq, tk, tq_attn):
    assert tq % _KT == 0 and tq_attn % tq == 0 and seq % tk == 0
    nq = seq // tq
    per = tq_attn // tq
    kern = functools.partial(_topk_mask_kernel, tq=tq, tk=tk, topk=topk, seq=seq)
    return pl.pallas_call(
        kern,
        grid=(batch, nq),
        in_specs=[
            pl.BlockSpec((N_IDX_HEADS, IDX_DIM, tq), lambda b, i: (0, 0, b * nq + i)),
            pl.BlockSpec((seq, IDX_DIM), lambda b, i: (b, 0)),
            pl.BlockSpec((N_IDX_HEADS, tq), lambda b, i: (0, b * nq + i)),
        ],
        out_specs=pl.BlockSpec((1, 1, seq, tq), lambda b, i: (b, i // per, 0, i % per)),
        out_shape=jax.ShapeDtypeStruct((batch, seq // tq_attn, seq, tq_attn), BF16),
        scratch_shapes=[pltpu.VMEM((seq, tq), I32)],
        compiler_params=_params(("arbitrary", "arbitrary")),
        name="topk_mask",
    )(qit, ki, wit)


def _attn_kernel(qt_ref, k_ref, vt_ref, bias_ref, o_ref, m_ref, l_ref, acc_ref, *, tq, tk):
    i = pl.program_id(1)
    kt = pl.program_id(2)
    nkt = (i * tq + tq + tk - 1) // tk

    @pl.when(kt == 0)
    def _():
        m_ref[...] = jnp.full(m_ref.shape, NEG_BIAS, F32)
        l_ref[...] = jnp.zeros(l_ref.shape, F32)
        acc_ref[...] = jnp.zeros(acc_ref.shape, F32)

    @pl.when(kt < nkt)
    def _():
        bias = bias_ref[0, 0].astype(F32)

        def scores(h):
            g = h // GROUP
            return jnp.dot(k_ref[:, HEAD_DIM * g:HEAD_DIM * (g + 1)], qt_ref[h], preferred_element_type=F32)

        s = scores(0)
        for h in range(N_HEADS):
            s_next = scores(h + 1) if h + 1 < N_HEADS else None
            g = h // GROUP
            s = s + bias
            m_prev = m_ref[h]
            m_new = jnp.maximum(m_prev, jnp.max(s, axis=0, keepdims=True))
            alpha = jnp.exp2(m_prev - m_new)
            p = jnp.exp2(s - m_new)
            l_ref[h] = alpha * l_ref[h] + jnp.sum(p, axis=0, keepdims=True)
            pv = jnp.dot(vt_ref[HEAD_DIM * g:HEAD_DIM * (g + 1), :], p.astype(BF16),
                         preferred_element_type=F32)
            acc_ref[h] = alpha * acc_ref[h] + pv
            m_ref[h] = m_new
            s = s_next

    @pl.when(kt == pl.num_programs(2) - 1)
    def _():
        for h in range(N_HEADS):
            o_ref[:, HEAD_DIM * h:HEAD_DIM * (h + 1)] = (acc_ref[h] / l_ref[h]).T.astype(BF16)


def _attention(qt, k, vt, bias, batch, seq, tq, tk):
    nq = seq // tq
    nk = seq // tk
    t = batch * seq

    def last_tile(i):
        return (i * tq + tq + tk - 1) // tk - 1

    kern = functools.partial(_attn_kernel, tq=tq, tk=tk)
    return pl.pallas_call(
        kern,
        grid=(batch, nq, nk),
        in_specs=[
            pl.BlockSpec((N_HEADS, HEAD_DIM, tq), lambda b, i, j: (0, 0, b * nq + i)),
            pl.BlockSpec((tk, KV_DIM), lambda b, i, j: (b * nk + jnp.minimum(j, last_tile(i)), 0)),
            pl.BlockSpec((KV_DIM, tk), lambda b, i, j: (0, b * nk + jnp.minimum(j, last_tile(i)))),
            pl.BlockSpec((1, 1, tk, tq), lambda b, i, j: (b, i, jnp.minimum(j, last_tile(i)), 0)),
        ],
        out_specs=pl.BlockSpec((tq, Q_DIM), lambda b, i, j: (b * nq + i, 0)),
        out_shape=jax.ShapeDtypeStruct((t, Q_DIM), BF16),
        scratch_shapes=[
            pltpu.VMEM((N_HEADS, 1, tq), F32),
            pltpu.VMEM((N_HEADS, 1, tq), F32),
            pltpu.VMEM((N_HEADS, HEAD_DIM, tq), F32),
        ],
        compiler_params=_params(("arbitrary", "arbitrary", "arbitrary")),
        name="masked_attention",
    )(qt, k, vt, bias)


def _proj_residual_kernel(h_ref, x_ref, w_ref, *rest, has_bias, tn):
    if has_bias:
        b_ref, o_ref = rest
    else:
        (o_ref,) = rest
    x = x_ref[...]
    for c in range(w_ref.shape[1] // tn):
        cols = slice(c * tn, (c + 1) * tn)
        y = h_ref[:, cols] + jnp.dot(x, w_ref[:, cols], preferred_element_type=F32)
        if has_bias:
            y = y + b_ref[:, cols]
        o_ref[:, cols] = y


def _proj_residual(h, x, w, b, tm, name):
    t, d = h.shape
    kdim = x.shape[1]
    row = lambda i: (i, 0)
    in_specs = [pl.BlockSpec((tm, d), row), pl.BlockSpec((tm, kdim), row), _resident(w.shape)]
    args = [h, x, w]
    if b is not None:
        in_specs.append(_resident((1, d)))
        args.append(b)
    kern = functools.partial(_proj_residual_kernel, has_bias=b is not None, tn=512)
    return pl.pallas_call(
        kern,
        grid=(t // tm,),
        in_specs=in_specs,
        out_specs=pl.BlockSpec((tm, d), row),
        out_shape=jax.ShapeDtypeStruct((t, d), F32),
        compiler_params=_params(("arbitrary",)),
        name=name,
    )(*args)


def _mlp_kernel(h_ref, g_ref, w1_ref, w2_ref, o_ref, u_ref):
    @pl.when(pl.program_id(1) == 0)
    def _():
        h = h_ref[...]
        u_ref[...] = _rms(h, g_ref[...]).astype(BF16)
        o_ref[...] = h

    a = jnp.dot(u_ref[...], w1_ref[...], preferred_element_type=F32)
    a = jnp.square(jnp.maximum(a, 0.0)).astype(BF16)
    o_ref[...] += jnp.dot(a, w2_ref[...], preferred_element_type=F32)


def _mlp(h, g, w1, w2, tm, tf):
    t, d = h.shape
    f = w1.shape[1]
    return pl.pallas_call(
        _mlp_kernel,
        grid=(t // tm, f // tf),
        in_specs=[
            pl.BlockSpec((tm, d), lambda i, j: (i, 0)),
            _resident((1, d)),
            pl.BlockSpec((d, tf), lambda i, j: (0, j)),
            pl.BlockSpec((tf, d), lambda i, j: (j, 0)),
        ],
        out_specs=pl.BlockSpec((tm, d), lambda i, j: (i, 0)),
        out_shape=jax.ShapeDtypeStruct((t, d), F32),
        scratch_shapes=[pltpu.VMEM((tm, d), BF16)],
        compiler_params=_params(("arbitrary", "arbitrary")),
        name="sqrelu_mlp",
    )(h, g, w1, w2)


def _pe_kernel(h_ref, g_ref, wg_ref, p_ref, wp_ref, *rest, final, tn):
    if final:
        gf_ref, o_ref = rest
    else:
        (o_ref,) = rest
    h = h_ref[...]
    u = _rms(h, g_ref[...]).astype(BF16)
    pb = p_ref[...].astype(BF16)
    for c in range(wg_ref.shape[1] // tn):
        cols = slice(c * tn, (c + 1) * tn)
        gate = jax.nn.sigmoid(jnp.dot(u, wg_ref[:, cols], preferred_element_type=F32))
        emb = jnp.dot(pb, wp_ref[:, cols], preferred_element_type=F32)
        o_ref[:, cols] = h_ref[:, cols] + emb * gate
    if final:
        o_ref[...] = _rms(o_ref[...], gf_ref[...])


def _pe(h, g, wg, p, wp, gf, tm):
    t, d = h.shape
    row = lambda i: (i, 0)
    in_specs = [pl.BlockSpec((tm, d), row), _resident((1, d)), _resident(wg.shape),
                pl.BlockSpec((tm, p.shape[1]), row), _resident(wp.shape)]
    args = [h, g, wg, p, wp]
    if gf is not None:
        in_specs.append(_resident((1, d)))
        args.append(gf)
    kern = functools.partial(_pe_kernel, final=gf is not None, tn=512)
    return pl.pallas_call(
        kern,
        grid=(t // tm,),
        in_specs=in_specs,
        out_specs=pl.BlockSpec((tm, d), row),
        out_shape=jax.ShapeDtypeStruct((t, d), F32),
        compiler_params=_params(("arbitrary",)),
        name="gated_embedding",
    )(*args)


def _conv_in_kernel(h_ref, g_ref, w_ref, b_ref, y_ref, *, tn):
    d = y_ref.shape[1]
    u = _rms(h_ref[...], g_ref[...]).astype(BF16)
    for c in range(d // tn):
        a = jnp.dot(u, w_ref[:, c * tn:(c + 1) * tn], preferred_element_type=F32) + b_ref[:, c * tn:(c + 1) * tn]
        gt = (jnp.dot(u, w_ref[:, d + c * tn:d + (c + 1) * tn], preferred_element_type=F32)
              + b_ref[:, d + c * tn:d + (c + 1) * tn])
        y_ref[:, c * tn:(c + 1) * tn] = a * jax.nn.sigmoid(gt)


def _conv_in(h, g, w, b, tm):
    t, d = h.shape
    row = lambda i: (i, 0)
    return pl.pallas_call(
        functools.partial(_conv_in_kernel, tn=512),
        grid=(t // tm,),
        in_specs=[pl.BlockSpec((tm, d), row), _resident((1, d)), _resident(w.shape), _resident(b.shape)],
        out_specs=pl.BlockSpec((tm, d), row),
        out_shape=jax.ShapeDtypeStruct((t, d), F32),
        compiler_params=_params(("arbitrary",)),
        name="conv_in_glu",
    )(h, g, w, b)


_HALO = 32
_CONV_RS = 64


def _dwconv_kernel(y_ref, halo_ref, w_ref, bdw_ref, lg_ref, lb_ref, z_ref, ext_ref, cv_ref, *, tm):
    d = y_ref.shape[2]
    nc = d // LANES
    first = pl.program_id(1) == 0
    for c in range(nc):
        cols = slice(c * LANES, (c + 1) * LANES)
        ext_ref[c, 0:_HALO] = jnp.where(first, 0.0, halo_ref[0, :, cols])
        ext_ref[c, _HALO:_HALO + tm] = y_ref[0, :, cols]

    off = _HALO - (CONV_WIDTH - 1)

    def chunk(c, carry):
        for rs in range(tm // _CONV_RS):
            r0 = rs * _CONV_RS
            acc = jnp.broadcast_to(bdw_ref[c], (_CONV_RS, LANES))
            for j in range(CONV_WIDTH):
                acc = acc + w_ref[c, j:j + 1, :] * ext_ref[c, r0 + off + j:r0 + off + j + _CONV_RS, :]
            cv_ref[c, r0:r0 + _CONV_RS] = acc
        return carry

    lax.fori_loop(0, nc, chunk, 0)

    s1 = cv_ref[0]
    for c in range(1, nc):
        s1 = s1 + cv_ref[c]
    mu = jnp.sum(s1, axis=1, keepdims=True) / d
    s2 = jnp.square(cv_ref[0] - mu)
    for c in range(1, nc):
        s2 = s2 + jnp.square(cv_ref[c] - mu)
    rstd = lax.rsqrt(jnp.sum(s2, axis=1, keepdims=True) / d + NORM_EPS)
    for c in range(nc):
        yn = (cv_ref[c] - mu) * rstd * lg_ref[c] + lb_ref[c]
        z_ref[0, :, c * LANES:(c + 1) * LANES] = (yn * jax.nn.sigmoid(yn)).astype(BF16)


def _dwconv_ln_swish(y3, w3, bdw3, lg3, lb3, tm):
    batch, seq, d = y3.shape
    nc = d // LANES
    hb = tm // _HALO
    kern = functools.partial(_dwconv_kernel, tm=tm)
    return pl.pallas_call(
        kern,
        grid=(batch, seq // tm),
        in_specs=[
            pl.BlockSpec((1, tm, d), lambda b, i: (b, i, 0)),
            pl.BlockSpec((1, _HALO, d), lambda b, i: (b, jnp.maximum(i * hb - 1, 0), 0)),
            _resident(w3.shape), _resident(bdw3.shape), _resident(lg3.shape), _resident(lb3.shape),
        ],
        out_specs=pl.BlockSpec((1, tm, d), lambda b, i: (b, i, 0)),
        out_shape=jax.ShapeDtypeStruct((batch, seq, d), BF16),
        scratch_shapes=[
            pltpu.VMEM((nc, _HALO + tm, LANES), F32),
            pltpu.VMEM((nc, tm, LANES), F32),
        ],
        compiler_params=_params(("arbitrary", "arbitrary")),
        name="dwconv_ln_swish",
    )(y3, y3, w3, bdw3, lg3, lb3)


def _rope_tables(seq):
    rd = HEAD_DIM // ROPE_FRAC
    half = rd // 2
    pos = jnp.arange(seq, dtype=F32)
    inv = ROPE_THETA ** (-jnp.arange(half, dtype=F32) * 2.0 / rd)
    ang = pos[:, None] * inv[None, :]
    cos = jnp.cos(ang)
    sin = jnp.sin(ang)
    ones = jnp.ones((seq, HEAD_DIM - rd), F32)
    zeros = jnp.zeros((seq, HEAD_DIM - rd), F32)
    zh = jnp.zeros((seq, half), F32)
    cos_t = jnp.concatenate([cos, cos, ones], axis=1)
    sa_t = jnp.concatenate([-sin, zh, zeros], axis=1)
    sb_t = jnp.concatenate([zh, sin, zeros], axis=1)
    return cos_t, sa_t, sb_t


def _chunked(v, nc):
    v2 = v.reshape(-1, v.shape[-1])
    return v2.reshape(v2.shape[0], nc, LANES).transpose(1, 0, 2)


def kernel(x, p, mix_norm, mlp_norm, mlp_w1, mlp_w2, pe_proj, pe_gate_norm, pe_gate, dsa_w_in, dsa_w_out, conv_w_in, conv_b_in, conv_w_dw, conv_b_dw, conv_ln_g, conv_ln_b, conv_w_out, conv_b_out, final_norm):
    batch, seq, d = x.shape
    depth = p.shape[0]
    t = batch * seq
    topk = min(TOPK_MAX, seq // KEY_FRAC)
    tm = 512
    h = x.reshape(t, d)
    row_vec = lambda v: v.reshape(1, -1)

    for i in range(depth):
        j = i // 2
        if i % 2 == 0:
            w = dsa_w_in[j]
            wq, wk, wv, wqi, wki, wwi = jnp.split(
                w, [Q_DIM, Q_DIM + KV_DIM, Q_DIM + 2 * KV_DIM, Q_DIM + 2 * KV_DIM + IDXQ_DIM,
                    Q_DIM + 2 * KV_DIM + IDXQ_DIM + IDX_DIM], axis=1)
            wwi = jnp.pad(wwi, ((0, 0), (0, LANES - N_IDX_HEADS)))
            w_all = jnp.concatenate([wq, wk, wqi, wki, wv, wwi], axis=1).astype(BF16)
            cos_t, sa_t, sb_t = _rope_tables(seq)
            qt, k, vt, qit, ki, wit = _dsa_proj(h, row_vec(mix_norm[i]), w_all, cos_t, sa_t, sb_t, seq, tm)
            tq_attn = min(512, seq)
            bias = _topk_mask(qit, ki, wit, batch, seq, topk, tq=256, tk=512, tq_attn=tq_attn)
            o = _attention(qt, k, vt, bias, batch, seq, tq=tq_attn, tk=512)
            h = _proj_residual(h, o, dsa_w_out[j].astype(BF16), None, tm, "dsa_out")
        else:
            nc = d // LANES
            y = _conv_in(h, row_vec(mix_norm[i]), conv_w_in[j].astype(BF16), row_vec(conv_b_in[j]), tm)
            w3 = _chunked(jnp.pad(conv_w_dw[j], ((0, _HALO - CONV_WIDTH), (0, 0))), nc)
            z = _dwconv_ln_swish(y.reshape(batch, seq, d), w3, _chunked(conv_b_dw[j], nc),
                                 _chunked(conv_ln_g[j], nc), _chunked(conv_ln_b[j], nc), tm=256)
            h = _proj_residual(h, z.reshape(t, d), conv_w_out[j].astype(BF16), row_vec(conv_b_out[j]), tm,
                               "conv_out")
        h = _mlp(h, row_vec(mlp_norm[i]), mlp_w1[i].astype(BF16), mlp_w2[i].astype(BF16), tm=512, tf=1024)
        gf = row_vec(final_norm) if i == depth - 1 else None
        h = _pe(h, row_vec(pe_gate_norm[i]), pe_gate[i].astype(BF16), p[i].reshape(t, -1),
                pe_proj[i].astype(BF16), gf, tm)
    return h.reshape(batch, seq, d)
```

```python
import functools
import math

import jax
import jax.numpy as jnp
from jax import lax
from jax.experimental import pallas as pl
from jax.experimental.pallas import tpu as pltpu

F32 = jnp.float32
BF16 = jnp.bfloat16
I32 = jnp.int32

CHUNK = 64
N_HEADS = 16
HEAD_DIM = 128
N_KV_HEADS = 4
GROUP = N_HEADS // N_KV_HEADS
N_IDX_HEADS = 8
IDX_DIM = 128
TOPK_MAX = 256
KEY_FRAC = 4
ROPE_THETA = 500000.0
ROPE_FRAC = 4
CONV_WIDTH = 31
NORM_EPS = 1e-6
Q_DIM = N_HEADS * HEAD_DIM
KV_DIM = N_KV_HEADS * HEAD_DIM
IDXQ_DIM = N_IDX_HEADS * IDX_DIM

LANES = 128
SUBLANES = 8
PACKED_SUBLANES = 16
V_ROWS = HEAD_DIM + PACKED_SUBLANES
F32_MIN_NORMAL = 1.1754943508222875e-38
BF16_MIN_NORMAL_BITS = 0x0080
INT_MIN = -(2 ** 31)
NEG_BIAS = -1e30
LOG2E = math.log2(math.e)

VMEM_LIMIT = 56 * 1024 * 1024


def _params(sem, vmem=VMEM_LIMIT):
    return pltpu.CompilerParams(dimension_semantics=sem, vmem_limit_bytes=vmem)


def _resident(shape):
    nd = len(shape)
    return pl.BlockSpec(shape, lambda *_: (0,) * nd, pipeline_mode=pl.Buffered(1))


def _rms(x, g):
    ms = jnp.mean(x * x, axis=-1, keepdims=True)
    return x * lax.rsqrt(ms + NORM_EPS) * g


_C_Q = 0
_C_K = _C_Q + Q_DIM
_C_QI = _C_K + KV_DIM
_C_KI = _C_QI + IDXQ_DIM
_C_V = _C_KI + IDX_DIM
_C_WI = _C_V + KV_DIM
_C_END = _C_WI + LANES


def _dsa_proj_kernel(h_ref, g_ref, w_ref, cos_ref, sa_ref, sb_ref,
                     qt_ref, k_ref, vt_ref, qit_ref, ki_ref, wit_ref):
    u = _rms(h_ref[...], g_ref[...]).astype(BF16)
    cos = cos_ref[...]
    sa = sa_ref[...]
    sb = sb_ref[...]
    rd = HEAD_DIM // ROPE_FRAC
    half = rd // 2

    def rope(y):
        return (y * cos + pltpu.roll(y, LANES - half, 1) * sa + pltpu.roll(y, half, 1) * sb)

    def proj(c0, n):
        return jnp.dot(u, w_ref[:, c0:c0 + n], preferred_element_type=F32)

    qscale = (HEAD_DIM ** -0.5) * LOG2E
    for c in range(N_HEADS // 4):
        y = proj(_C_Q + 512 * c, 512)
        for j in range(4):
            qt_ref[4 * c + j] = (rope(y[:, 128 * j:128 * (j + 1)]) * qscale).T.astype(BF16)
    y = proj(_C_K, KV_DIM)
    for j in range(N_KV_HEADS):
        k_ref[j] = rope(y[:, 128 * j:128 * (j + 1)]).astype(BF16)
    for c in range(N_IDX_HEADS // 4):
        y = proj(_C_QI + 512 * c, 512)
        for j in range(4):
            qit_ref[4 * c + j] = (rope(y[:, 128 * j:128 * (j + 1)]) * (IDX_DIM ** -0.5)).T.astype(BF16)
    ki_ref[...] = rope(proj(_C_KI, IDX_DIM)).astype(BF16)
    y = proj(_C_V, KV_DIM)
    for j in range(N_KV_HEADS):
        vt_ref[j, 0:HEAD_DIM, :] = y[:, 128 * j:128 * (j + 1)].T.astype(BF16)
        vt_ref[j, HEAD_DIM:V_ROWS, :] = jnp.ones((V_ROWS - HEAD_DIM, vt_ref.shape[2]), BF16)
    wit_ref[...] = (proj(_C_WI, LANES) * (N_IDX_HEADS ** -0.5)).T[:N_IDX_HEADS]


def _dsa_proj(h, g, w_all, cos, sa, sb, seq, tm):
    t, d = h.shape
    nseq = seq // tm
    row = lambda i: (i, 0)
    tab = pl.BlockSpec((tm, LANES), lambda i: (i % nseq, 0))
    return pl.pallas_call(
        _dsa_proj_kernel,
        grid=(t // tm,),
        in_specs=[pl.BlockSpec((tm, d), row), _resident((1, d)), _resident(w_all.shape), tab, tab, tab],
        out_specs=[
            pl.BlockSpec((N_HEADS, HEAD_DIM, tm), lambda i: (0, 0, i)),
            pl.BlockSpec((N_KV_HEADS, tm, HEAD_DIM), lambda i: (0, i, 0)),
            pl.BlockSpec((N_KV_HEADS, V_ROWS, tm), lambda i: (0, 0, i)),
            pl.BlockSpec((N_IDX_HEADS, IDX_DIM, tm), lambda i: (0, 0, i)),
            pl.BlockSpec((tm, IDX_DIM), row),
            pl.BlockSpec((N_IDX_HEADS, tm), lambda i: (0, i)),
        ],
        out_shape=[
            jax.ShapeDtypeStruct((N_HEADS, HEAD_DIM, t), BF16),
            jax.ShapeDtypeStruct((N_KV_HEADS, t, HEAD_DIM), BF16),
            jax.ShapeDtypeStruct((N_KV_HEADS, V_ROWS, t), BF16),
            jax.ShapeDtypeStruct((N_IDX_HEADS, IDX_DIM, t), BF16),
            jax.ShapeDtypeStruct((t, IDX_DIM), BF16),
            jax.ShapeDtypeStruct((N_IDX_HEADS, t), F32),
        ],
        compiler_params=_params(("arbitrary",)),
        name="dsa_proj",
    )(h, g, w_all, cos, sa, sb)


_KT = 256


def _fold(y):
    n = y.shape[0]
    parts = [y[j] for j in range(min(4, n))]
    for j in range(4, n):
        parts[j % 4] = parts[j % 4] + y[j]
    while len(parts) > 1:
        parts = [parts[j] + parts[j + 1] if j + 1 < len(parts) else parts[j] for j in range(0, len(parts), 2)]
    return parts[0]


def _topk_mask_kernel(qit_ref, ki_ref, wit_ref, bias_ref, sc_ref, hi_ref, *, tq, tk, topk, seq):
    i = pl.program_id(1)
    q0 = i * tq
    n_adm = q0 + tq
    nkt = (n_adm + tk - 1) // tk
    nct = n_adm // _KT
    kf = float(topk)
    lane = lax.broadcasted_iota(I32, (1, tq), 1)
    adm_end = q0 + (lane // CHUNK + 1) * CHUNK

    def score_tile(kt, carry):
        k0 = pl.multiple_of(kt * tk, tk)
        kb = ki_ref[pl.ds(k0, tk), :]
        acc = None
        for h in range(N_IDX_HEADS):
            r = jnp.dot(kb, qit_ref[h], preferred_element_type=F32)
            term = wit_ref[h:h + 1, :] * jnp.maximum(r, 0.0)
            acc = term if acc is None else acc + term
        acc = jnp.where(jnp.abs(acc) < F32_MIN_NORMAL, 0.0, acc)
        bits = pltpu.bitcast(acc, I32)
        key = bits ^ ((bits >> 31) & 0x7FFFFFFF)
        adm = (k0 + lax.broadcasted_iota(I32, (tk, tq), 0)) < adm_end
        sc_ref[pl.ds(k0, tk), :] = jnp.where(adm, key, INT_MIN)
        top = pltpu.bitcast(bits & jnp.int32(-65536), F32)
        hi_ref[pl.ds(k0, tk), :] = jnp.where(adm, top, -jnp.inf).astype(BF16)
        return carry

    lax.fori_loop(0, nkt, score_tile, 0)

    row_kt = lax.broadcasted_iota(I32, (_KT, tq), 0)

    def count(pred):
        def body(c, acc):
            r0 = pl.multiple_of(c * _KT, _KT)
            key = sc_ref[pl.ds(r0, _KT), :]
            ones = jnp.where(pred(key, r0 + row_kt), 1.0, 0.0)
            return acc + _fold(ones.reshape(_KT // SUBLANES, SUBLANES, tq))
        acc = lax.fori_loop(0, nct, body, jnp.zeros((SUBLANES, tq), F32))
        return jnp.sum(acc, axis=0, keepdims=True)

    def count_hi(cand):
        def body(c, acc):
            r0 = pl.multiple_of(c * _KT, _KT)
            ones = jnp.where(hi_ref[pl.ds(r0, _KT), :] >= cand, jnp.ones((), BF16), jnp.zeros((), BF16))
            part = _fold(ones.reshape(_KT // PACKED_SUBLANES, PACKED_SUBLANES, tq))
            return acc + part.astype(F32)
        acc = lax.fori_loop(0, nct, body, jnp.zeros((PACKED_SUBLANES, tq), F32))
        return jnp.sum(acc, axis=0, keepdims=True)

    def emit(keep):
        def body(c, carry):
            r0 = pl.multiple_of(c * _KT, _KT)
            key = sc_ref[pl.ds(r0, _KT), :]
            bias_ref[0, 0, pl.ds(r0, _KT), :] = jnp.where(keep(key, r0 + row_kt), 0.0, NEG_BIAS).astype(BF16)
            return carry
        lax.fori_loop(0, nct, body, 0)

    def hi_pass(p, state):
        u, cnt_u = state
        cand_u = u | (jnp.int32(1) << (15 - p))
        c16 = cand_u - 32768
        c16 = jnp.where(jnp.logical_and(c16 > 0, c16 < BF16_MIN_NORMAL_BITS), BF16_MIN_NORMAL_BITS, c16)
        raw = jnp.where(c16 >= 0, c16, c16 ^ 0x7FFF) << 16
        cnt = count_hi(pltpu.bitcast(raw, F32).astype(BF16))
        ok = cnt >= kf
        return jnp.where(ok, cand_u, u), jnp.where(ok, cnt, cnt_u)

    many = jnp.full((1, tq), float(seq + 1), F32)
    u, cnt_u = lax.fori_loop(0, 16, hi_pass, (jnp.zeros((1, tq), I32), many))

    few = adm_end <= topk

    def unsettled(cnt):
        return jnp.max(jnp.where(jnp.logical_or(few, cnt == kf), 0.0, 1.0)) > 0.0

    def lo_cond(state):
        p, _, _, go = state
        return jnp.logical_and(p < 16, go)

    def lo_pass(state):
        p, tau, cnt_t, _ = state
        cand = tau | (jnp.int32(1) << (15 - p))
        cnt = count(lambda key, kidx: key >= cand)
        ok = cnt >= kf
        cnt_t = jnp.where(ok, cnt, cnt_t)
        return p + 1, jnp.where(ok, cand, tau), cnt_t, unsettled(cnt_t)

    _, tau, cnt_ge, _ = lax.while_loop(lo_cond, lo_pass, (jnp.int32(0), (u - 32768) << 16, cnt_u, unsettled(cnt_u)))
    tau = jnp.maximum(tau, INT_MIN + 1)
    has_tie = jnp.max(jnp.where(few, 0.0, cnt_ge)) > kf

    def fill(c, carry):
        r0 = pl.multiple_of(c * _KT, _KT)
        bias_ref[0, 0, pl.ds(r0, _KT), :] = jnp.full((_KT, tq), NEG_BIAS, BF16)
        return carry

    lax.fori_loop(nct, seq // _KT, fill, 0)

    @pl.when(jnp.logical_not(has_tie))
    def _():
        emit(lambda key, kidx: key >= tau)

    @pl.when(has_tie)
    def _():
        need = kf - count(lambda key, kidx: key > tau)
        nbits = max(1, (seq - 1).bit_length())

        def idx_pass(p, j0):
            cand = j0 + (jnp.int32(1) << (nbits - 1 - p))
            c_lt = count(lambda key, kidx: jnp.logical_and(key == tau, kidx < cand))
            return jnp.where(c_lt < need, cand, j0)

        j0 = lax.fori_loop(0, nbits, idx_pass, jnp.zeros((1, tq), I32))
        emit(lambda key, kidx: jnp.logical_or(key > tau, jnp.logical_and(key == tau, kidx <= j0)))


def _topk_mask(qit, ki, wit, batch, seq, topk, tq, tk, tq_attn):
    assert tq % _KT == 0 and tq_attn % tq == 0 and seq % tk == 0
    nq = seq // tq
    per = tq_attn // tq
    kern = functools.partial(_topk_mask_kernel, tq=tq, tk=tk, topk=topk, seq=seq)
    return pl.pallas_call(
        kern,
        grid=(batch, nq),
        in_specs=[
            pl.BlockSpec((N_IDX_HEADS, IDX_DIM, tq), lambda b, i: (0, 0, b * nq + i)),
            pl.BlockSpec((seq, IDX_DIM), lambda b, i: (b, 0)),
            pl.BlockSpec((N_IDX_HEADS, tq), lambda b, i: (0, b * nq + i)),
        ],
        out_specs=pl.BlockSpec((1, 1, seq, tq), lambda b, i: (b, i // per, 0, i % per)),
        out_shape=jax.ShapeDtypeStruct((batch, seq // tq_attn, seq, tq_attn), BF16),
        scratch_shapes=[pltpu.VMEM((seq, tq), I32), pltpu.VMEM((seq, tq), BF16)],
        compiler_params=_params(("arbitrary", "arbitrary")),
        name="topk_mask",
    )(qit, ki, wit)


_HP = 2


def _attn_kernel(qt_ref, k_ref, vt_ref, bias_ref, o_ref, m_ref, l_ref, acc_ref, sa_ref, sb_ref, bf_ref, *, tq, tk):
    i = pl.program_id(1)
    kt = pl.program_id(2)
    nkt = (i * tq + tq + tk - 1) // tk
    nstage = N_HEADS // _HP

    @pl.when(kt == 0)
    def _():
        m_ref[...] = jnp.full(m_ref.shape, NEG_BIAS, F32)
        l_ref[...] = jnp.zeros(l_ref.shape, F32)
        acc_ref[...] = jnp.zeros(acc_ref.shape, F32)

    @pl.when(kt < nkt)
    def _():
        bf_ref[...] = bias_ref[0, 0].astype(F32)

        def scores(j, buf):
            for r in range(_HP):
                h = j * _HP + r
                buf[r] = jnp.dot(k_ref[h // GROUP], qt_ref[h], preferred_element_type=F32) + bf_ref[...]

        def softmax_pv(j, buf):
            for r in range(_HP):
                h = j * _HP + r
                s = buf[r]
                m_prev = m_ref[h]
                m_new = jnp.maximum(m_prev, jnp.max(s, axis=0, keepdims=True))
                alpha = jnp.exp2(m_prev - m_new)
                p = jnp.exp2(s - m_new).astype(BF16)
                pv = jnp.dot(vt_ref[h // GROUP], p, preferred_element_type=F32)
                l_ref[h] = alpha * l_ref[h] + pv[HEAD_DIM:HEAD_DIM + 1]
                acc_ref[h] = alpha * acc_ref[h] + pv[:HEAD_DIM]
                m_ref[h] = m_new

        scores(0, sa_ref)

        def two_stages(jj, carry):
            j = 2 * jj
            scores(j + 1, sb_ref)
            softmax_pv(j, sa_ref)
            scores(j + 2, sa_ref)
            softmax_pv(j + 1, sb_ref)
            return carry

        lax.fori_loop(0, nstage // 2 - 1, two_stages, 0)
        scores(nstage - 1, sb_ref)
        softmax_pv(nstage - 2, sa_ref)
        softmax_pv(nstage - 1, sb_ref)

    @pl.when(kt == pl.num_programs(2) - 1)
    def _():
        for h in range(N_HEADS):
            o_ref[:, HEAD_DIM * h:HEAD_DIM * (h + 1)] = (acc_ref[h] / l_ref[h]).T.astype(BF16)


def _attention(qt, k, vt, bias, batch, seq, tq, tk):
    nq = seq // tq
    nk = seq // tk
    t = batch * seq

    def last_tile(i):
        return (i * tq + tq + tk - 1) // tk - 1

    kern = functools.partial(_attn_kernel, tq=tq, tk=tk)
    return pl.pallas_call(
        kern,
        grid=(batch, nq, nk),
        in_specs=[
            pl.BlockSpec((N_HEADS, HEAD_DIM, tq), lambda b, i, j: (0, 0, b * nq + i)),
            pl.BlockSpec((N_KV_HEADS, tk, HEAD_DIM), lambda b, i, j: (0, b * nk + jnp.minimum(j, last_tile(i)), 0)),
            pl.BlockSpec((N_KV_HEADS, V_ROWS, tk), lambda b, i, j: (0, 0, b * nk + jnp.minimum(j, last_tile(i)))),
            pl.BlockSpec((1, 1, tk, tq), lambda b, i, j: (b, i, jnp.minimum(j, last_tile(i)), 0)),
        ],
        out_specs=pl.BlockSpec((tq, Q_DIM), lambda b, i, j: (b * nq + i, 0)),
        out_shape=jax.ShapeDtypeStruct((t, Q_DIM), BF16),
        scratch_shapes=[
            pltpu.VMEM((N_HEADS, 1, tq), F32),
            pltpu.VMEM((N_HEADS, 1, tq), F32),
            pltpu.VMEM((N_HEADS, HEAD_DIM, tq), F32),
            pltpu.VMEM((_HP, tk, tq), F32),
            pltpu.VMEM((_HP, tk, tq), F32),
            pltpu.VMEM((tk, tq), F32),
        ],
        compiler_params=_params(("arbitrary", "arbitrary", "arbitrary")),
        name="masked_attention",
    )(qt, k, vt, bias)


def _proj_residual_kernel(h_ref, x_ref, w_ref, *rest, has_bias, tn):
    if has_bias:
        b_ref, o_ref = rest
    else:
        (o_ref,) = rest
    x = x_ref[...]
    for c in range(w_ref.shape[1] // tn):
        cols = slice(c * tn, (c + 1) * tn)
        y = h_ref[:, cols] + jnp.dot(x, w_ref[:, cols], preferred_element_type=F32)
        if has_bias:
            y = y + b_ref[:, cols]
        o_ref[:, cols] = y


def _proj_residual(h, x, w, b, tm, name):
    t, d = h.shape
    kdim = x.shape[1]
    row = lambda i: (i, 0)
    in_specs = [pl.BlockSpec((tm, d), row), pl.BlockSpec((tm, kdim), row), _resident(w.shape)]
    args = [h, x, w]
    if b is not None:
        in_specs.append(_resident((1, d)))
        args.append(b)
    kern = functools.partial(_proj_residual_kernel, has_bias=b is not None, tn=512)
    return pl.pallas_call(
        kern,
        grid=(t // tm,),
        in_specs=in_specs,
        out_specs=pl.BlockSpec((tm, d), row),
        out_shape=jax.ShapeDtypeStruct((t, d), F32),
        compiler_params=_params(("arbitrary",)),
        name=name,
    )(*args)


def _mlp_kernel(h_ref, g_ref, w1_ref, w2_ref, o_ref, u_ref):
    @pl.when(pl.program_id(1) == 0)
    def _():
        h = h_ref[...]
        u_ref[...] = _rms(h, g_ref[...]).astype(BF16)
        o_ref[...] = h

    a = jnp.dot(u_ref[...], w1_ref[...], preferred_element_type=F32)
    a = jnp.square(jnp.maximum(a, 0.0)).astype(BF16)
    o_ref[...] += jnp.dot(a, w2_ref[...], preferred_element_type=F32)


def _mlp(h, g, w1, w2, tm, tf):
    t, d = h.shape
    f = w1.shape[1]
    return pl.pallas_call(
        _mlp_kernel,
        grid=(t // tm, f // tf),
        in_specs=[
            pl.BlockSpec((tm, d), lambda i, j: (i, 0)),
            _resident((1, d)),
            pl.BlockSpec((d, tf), lambda i, j: (0, j)),
            pl.BlockSpec((tf, d), lambda i, j: (j, 0)),
        ],
        out_specs=pl.BlockSpec((tm, d), lambda i, j: (i, 0)),
        out_shape=jax.ShapeDtypeStruct((t, d), F32),
        scratch_shapes=[pltpu.VMEM((tm, d), BF16)],
        compiler_params=_params(("arbitrary", "arbitrary")),
        name="sqrelu_mlp",
    )(h, g, w1, w2)


def _pe_kernel(h_ref, g_ref, wg_ref, p_ref, wp_ref, *rest, final, tn):
    if final:
        gf_ref, o_ref = rest
    else:
        (o_ref,) = rest
    h = h_ref[...]
    u = _rms(h, g_ref[...]).astype(BF16)
    pb = p_ref[...].astype(BF16)
    for c in range(wg_ref.shape[1] // tn):
        cols = slice(c * tn, (c + 1) * tn)
        gate = jax.nn.sigmoid(jnp.dot(u, wg_ref[:, cols], preferred_element_type=F32))
        emb = jnp.dot(pb, wp_ref[:, cols], preferred_element_type=F32)
        o_ref[:, cols] = h_ref[:, cols] + emb * gate
    if final:
        o_ref[...] = _rms(o_ref[...], gf_ref[...])


def _pe(h, g, wg, p, wp, gf, tm):
    t, d = h.shape
    row = lambda i: (i, 0)
    in_specs = [pl.BlockSpec((tm, d), row), _resident((1, d)), _resident(wg.shape),
                pl.BlockSpec((tm, p.shape[1]), row), _resident(wp.shape)]
    args = [h, g, wg, p, wp]
    if gf is not None:
        in_specs.append(_resident((1, d)))
        args.append(gf)
    kern = functools.partial(_pe_kernel, final=gf is not None, tn=512)
    return pl.pallas_call(
        kern,
        grid=(t // tm,),
        in_specs=in_specs,
        out_specs=pl.BlockSpec((tm, d), row),
        out_shape=jax.ShapeDtypeStruct((t, d), F32),
        compiler_params=_params(("arbitrary",)),
        name="gated_embedding",
    )(*args)


def _conv_in_kernel(h_ref, g_ref, w_ref, b_ref, y_ref, *, tn):
    d = y_ref.shape[1]
    u = _rms(h_ref[...], g_ref[...]).astype(BF16)
    for c in range(d // tn):
        a = jnp.dot(u, w_ref[:, c * tn:(c + 1) * tn], preferred_element_type=F32) + b_ref[:, c * tn:(c + 1) * tn]
        gt = (jnp.dot(u, w_ref[:, d + c * tn:d + (c + 1) * tn], preferred_element_type=F32)
              + b_ref[:, d + c * tn:d + (c + 1) * tn])
        y_ref[:, c * tn:(c + 1) * tn] = a * jax.nn.sigmoid(gt)


def _conv_in(h, g, w, b, tm):
    t, d = h.shape
    row = lambda i: (i, 0)
    return pl.pallas_call(
        functools.partial(_conv_in_kernel, tn=512),
        grid=(t // tm,),
        in_specs=[pl.BlockSpec((tm, d), row), _resident((1, d)), _resident(w.shape), _resident(b.shape)],
        out_specs=pl.BlockSpec((tm, d), row),
        out_shape=jax.ShapeDtypeStruct((t, d), F32),
        compiler_params=_params(("arbitrary",)),
        name="conv_in_glu",
    )(h, g, w, b)


_HALO = 32
_CONV_RS = 64


def _dwconv_kernel(y_ref, halo_ref, w_ref, bdw_ref, lg_ref, lb_ref, z_ref, ext_ref, cv_ref, *, tm):
    d = y_ref.shape[2]
    nc = d // LANES
    first = pl.program_id(1) == 0
    for c in range(nc):
        cols = slice(c * LANES, (c + 1) * LANES)
        ext_ref[c, 0:_HALO] = jnp.where(first, 0.0, halo_ref[0, :, cols])
        ext_ref[c, _HALO:_HALO + tm] = y_ref[0, :, cols]

    off = _HALO - (CONV_WIDTH - 1)

    def chunk(c, carry):
        for rs in range(tm // _CONV_RS):
            r0 = rs * _CONV_RS
            acc = jnp.broadcast_to(bdw_ref[c], (_CONV_RS, LANES))
            for j in range(CONV_WIDTH):
                acc = acc + w_ref[c, j:j + 1, :] * ext_ref[c, r0 + off + j:r0 + off + j + _CONV_RS, :]
            cv_ref[c, r0:r0 + _CONV_RS] = acc
        return carry

    lax.fori_loop(0, nc, chunk, 0)

    s1 = cv_ref[0]
    for c in range(1, nc):
        s1 = s1 + cv_ref[c]
    mu = jnp.sum(s1, axis=1, keepdims=True) / d
    s2 = jnp.square(cv_ref[0] - mu)
    for c in range(1, nc):
        s2 = s2 + jnp.square(cv_ref[c] - mu)
    rstd = lax.rsqrt(jnp.sum(s2, axis=1, keepdims=True) / d + NORM_EPS)
    for c in range(nc):
        yn = (cv_ref[c] - mu) * rstd * lg_ref[c] + lb_ref[c]
        z_ref[0, :, c * LANES:(c + 1) * LANES] = (yn * jax.nn.sigmoid(yn)).astype(BF16)


def _dwconv_ln_swish(y3, w3, bdw3, lg3, lb3, tm):
    batch, seq, d = y3.shape
    nc = d // LANES
    hb = tm // _HALO
    kern = functools.partial(_dwconv_kernel, tm=tm)
    return pl.pallas_call(
        kern,
        grid=(batch, seq // tm),
        in_specs=[
            pl.BlockSpec((1, tm, d), lambda b, i: (b, i, 0)),
            pl.BlockSpec((1, _HALO, d), lambda b, i: (b, jnp.maximum(i * hb - 1, 0), 0)),
            _resident(w3.shape), _resident(bdw3.shape), _resident(lg3.shape), _resident(lb3.shape),
        ],
        out_specs=pl.BlockSpec((1, tm, d), lambda b, i: (b, i, 0)),
        out_shape=jax.ShapeDtypeStruct((batch, seq, d), BF16),
        scratch_shapes=[
            pltpu.VMEM((nc, _HALO + tm, LANES), F32),
            pltpu.VMEM((nc, tm, LANES), F32),
        ],
        compiler_params=_params(("arbitrary", "arbitrary")),
        name="dwconv_ln_swish",
    )(y3, y3, w3, bdw3, lg3, lb3)


def _rope_tables(seq):
    rd = HEAD_DIM // ROPE_FRAC
    half = rd // 2
    pos = jnp.arange(seq, dtype=F32)
    inv = ROPE_THETA ** (-jnp.arange(half, dtype=F32) * 2.0 / rd)
    ang = pos[:, None] * inv[None, :]
    cos = jnp.cos(ang)
    sin = jnp.sin(ang)
    ones = jnp.ones((seq, HEAD_DIM - rd), F32)
    zeros = jnp.zeros((seq, HEAD_DIM - rd), F32)
    zh = jnp.zeros((seq, half), F32)
    cos_t = jnp.concatenate([cos, cos, ones], axis=1)
    sa_t = jnp.concatenate([-sin, zh, zeros], axis=1)
    sb_t = jnp.concatenate([zh, sin, zeros], axis=1)
    return cos_t, sa_t, sb_t


def _chunked(v, nc):
    v2 = v.reshape(-1, v.shape[-1])
    return v2.reshape(v2.shape[0], nc, LANES).transpose(1, 0, 2)


def kernel(x, p, mix_norm, mlp_norm, mlp_w1, mlp_w2, pe_proj, pe_gate_norm, pe_gate, dsa_w_in, dsa_w_out, conv_w_in, conv_b_in, conv_w_dw, conv_b_dw, conv_ln_g, conv_ln_b, conv_w_out, conv_b_out, final_norm):
    batch, seq, d = x.shape
    depth = p.shape[0]
    t = batch * seq
    topk = min(TOPK_MAX, seq // KEY_FRAC)
    tm = 512
    h = x.reshape(t, d)
    row_vec = lambda v: v.reshape(1, -1)

    for i in range(depth):
        j = i // 2
        if i % 2 == 0:
            w = dsa_w_in[j]
            wq, wk, wv, wqi, wki, wwi = jnp.split(
                w, [Q_DIM, Q_DIM + KV_DIM, Q_DIM + 2 * KV_DIM, Q_DIM + 2 * KV_DIM + IDXQ_DIM,
                    Q_DIM + 2 * KV_DIM + IDXQ_DIM + IDX_DIM], axis=1)
            wwi = jnp.pad(wwi, ((0, 0), (0, LANES - N_IDX_HEADS)))
            w_all = jnp.concatenate([wq, wk, wqi, wki, wv, wwi], axis=1).astype(BF16)
            cos_t, sa_t, sb_t = _rope_tables(seq)
            qt, k, vt, qit, ki, wit = _dsa_proj(h, row_vec(mix_norm[i]), w_all, cos_t, sa_t, sb_t, seq, tm)
            tq_attn = min(512, seq)
            bias = _topk_mask(qit, ki, wit, batch, seq, topk, tq=256, tk=512, tq_attn=tq_attn)
            o = _attention(qt, k, vt, bias, batch, seq, tq=tq_attn, tk=512)
            h = _proj_residual(h, o, dsa_w_out[j].astype(BF16), None, tm, "dsa_out")
        else:
            nc = d // LANES
            y = _conv_in(h, row_vec(mix_norm[i]), conv_w_in[j].astype(BF16), row_vec(conv_b_in[j]), tm)
            w3 = _chunked(jnp.pad(conv_w_dw[j], ((0, _HALO - CONV_WIDTH), (0, 0))), nc)
            z = _dwconv_ln_swish(y.reshape(batch, seq, d), w3, _chunked(conv_b_dw[j], nc),
                                 _chunked(conv_ln_g[j], nc), _chunked(conv_ln_b[j], nc), tm=256)
            h = _proj_residual(h, z.reshape(t, d), conv_w_out[j].astype(BF16), row_vec(conv_b_out[j]), tm,
                               "conv_out")
        h = _mlp(h, row_vec(mlp_norm[i]), mlp_w1[i].astype(BF16), mlp_w2[i].astype(BF16), tm=512, tf=1024)
        gf = row_vec(final_norm) if i == depth - 1 else None
        h = _pe(h, row_vec(pe_gate_norm[i]), pe_gate[i].astype(BF16), p[i].reshape(t, -1),
                pe_proj[i].astype(BF16), gf, tm)
    return h.reshape(batch, seq, d)
```

```python
import functools
import math

import jax
import jax.numpy as jnp
from jax import lax
from jax.experimental import pallas as pl
from jax.experimental.pallas import tpu as pltpu

F32 = jnp.float32
BF16 = jnp.bfloat16
I32 = jnp.int32

CHUNK = 64
N_HEADS = 16
HEAD_DIM = 128
N_KV_HEADS = 4
GROUP = N_HEADS // N_KV_HEADS
N_IDX_HEADS = 8
IDX_DIM = 128
TOPK_MAX = 256
KEY_FRAC = 4
ROPE_THETA = 500000.0
ROPE_FRAC = 4
CONV_WIDTH = 31
NORM_EPS = 1e-6
Q_DIM = N_HEADS * HEAD_DIM
KV_DIM = N_KV_HEADS * HEAD_DIM
IDXQ_DIM = N_IDX_HEADS * IDX_DIM

LANES = 128
SUBLANES = 8
PACKED_SUBLANES = 16
V_ROWS = HEAD_DIM + PACKED_SUBLANES
F32_MIN_NORMAL = 1.1754943508222875e-38
BF16_MIN_NORMAL_BITS = 0x0080
INT_MIN = -(2 ** 31)
NEG_BIAS = -1e30
LOG2E = math.log2(math.e)

VMEM_LIMIT = 56 * 1024 * 1024


def _params(sem, vmem=VMEM_LIMIT):
    return pltpu.CompilerParams(dimension_semantics=sem, vmem_limit_bytes=vmem)


def _resident(shape):
    nd = len(shape)
    return pl.BlockSpec(shape, lambda *_: (0,) * nd, pipeline_mode=pl.Buffered(1))


def _rms(x, g):
    ms = jnp.mean(x * x, axis=-1, keepdims=True)
    return x * lax.rsqrt(ms + NORM_EPS) * g


_C_Q = 0
_C_K = _C_Q + Q_DIM
_C_QI = _C_K + KV_DIM
_C_KI = _C_QI + IDXQ_DIM
_C_V = _C_KI + IDX_DIM
_C_WI = _C_V + KV_DIM
_C_END = _C_WI + LANES


def _dsa_proj_kernel(h_ref, g_ref, w_ref, cos_ref, sa_ref, sb_ref,
                     qt_ref, k_ref, vt_ref, qit_ref, ki_ref, wit_ref):
    u = _rms(h_ref[...], g_ref[...]).astype(BF16)
    cos = cos_ref[...]
    sa = sa_ref[...]
    sb = sb_ref[...]
    rd = HEAD_DIM // ROPE_FRAC
    half = rd // 2

    def rope(y):
        return (y * cos + pltpu.roll(y, LANES - half, 1) * sa + pltpu.roll(y, half, 1) * sb)

    def proj(c0, n):
        return jnp.dot(u, w_ref[:, c0:c0 + n], preferred_element_type=F32)

    qscale = (HEAD_DIM ** -0.5) * LOG2E
    for c in range(N_HEADS // 4):
        y = proj(_C_Q + 512 * c, 512)
        for j in range(4):
            qt_ref[4 * c + j] = (rope(y[:, 128 * j:128 * (j + 1)]) * qscale).T.astype(BF16)
    y = proj(_C_K, KV_DIM)
    for j in range(N_KV_HEADS):
        k_ref[j] = rope(y[:, 128 * j:128 * (j + 1)]).astype(BF16)
    for c in range(N_IDX_HEADS // 4):
        y = proj(_C_QI + 512 * c, 512)
        for j in range(4):
            qit_ref[4 * c + j] = (rope(y[:, 128 * j:128 * (j + 1)]) * (IDX_DIM ** -0.5)).T.astype(BF16)
    ki_ref[...] = rope(proj(_C_KI, IDX_DIM)).astype(BF16)
    y = proj(_C_V, KV_DIM)
    for j in range(N_KV_HEADS):
        vt_ref[j, 0:HEAD_DIM, :] = y[:, 128 * j:128 * (j + 1)].T.astype(BF16)
        vt_ref[j, HEAD_DIM:V_ROWS, :] = jnp.ones((V_ROWS - HEAD_DIM, vt_ref.shape[2]), BF16)
    wit_ref[...] = (proj(_C_WI, LANES) * (N_IDX_HEADS ** -0.5)).T[:N_IDX_HEADS]


def _dsa_proj(h, g, w_all, cos, sa, sb, seq, tm):
    t, d = h.shape
    nseq = seq // tm
    row = lambda i: (i, 0)
    tab = pl.BlockSpec((tm, LANES), lambda i: (i % nseq, 0))
    return pl.pallas_call(
        _dsa_proj_kernel,
        grid=(t // tm,),
        in_specs=[pl.BlockSpec((tm, d), row), _resident((1, d)), _resident(w_all.shape), tab, tab, tab],
        out_specs=[
            pl.BlockSpec((N_HEADS, HEAD_DIM, tm), lambda i: (0, 0, i)),
            pl.BlockSpec((N_KV_HEADS, tm, HEAD_DIM), lambda i: (0, i, 0)),
            pl.BlockSpec((N_KV_HEADS, V_ROWS, tm), lambda i: (0, 0, i)),
            pl.BlockSpec((N_IDX_HEADS, IDX_DIM, tm), lambda i: (0, 0, i)),
            pl.BlockSpec((tm, IDX_DIM), row),
            pl.BlockSpec((N_IDX_HEADS, tm), lambda i: (0, i)),
        ],
        out_shape=[
            jax.ShapeDtypeStruct((N_HEADS, HEAD_DIM, t), BF16),
            jax.ShapeDtypeStruct((N_KV_HEADS, t, HEAD_DIM), BF16),
            jax.ShapeDtypeStruct((N_KV_HEADS, V_ROWS, t), BF16),
            jax.ShapeDtypeStruct((N_IDX_HEADS, IDX_DIM, t), BF16),
            jax.ShapeDtypeStruct((t, IDX_DIM), BF16),
            jax.ShapeDtypeStruct((N_IDX_HEADS, t), F32),
        ],
        compiler_params=_params(("arbitrary",)),
        name="dsa_proj",
    )(h, g, w_all, cos, sa, sb)


_KT = 256


def _fold(y):
    n = y.shape[0]
    parts = [y[j] for j in range(min(4, n))]
    for j in range(4, n):
        parts[j % 4] = parts[j % 4] + y[j]
    while len(parts) > 1:
        parts = [parts[j] + parts[j + 1] if j + 1 < len(parts) else parts[j] for j in range(0, len(parts), 2)]
    return parts[0]


def _topk_mask_kernel(qit_ref, ki_ref, wit_ref, bias_ref, sc_ref, hi_ref, *, tq, tk, topk, seq):
    i = pl.program_id(1)
    q0 = i * tq
    n_adm = q0 + tq
    nkt = (n_adm + tk - 1) // tk
    nct = n_adm // _KT
    kf = float(topk)
    lane = lax.broadcasted_iota(I32, (1, tq), 1)
    adm_end = q0 + (lane // CHUNK + 1) * CHUNK

    def score_tile(kt, masked):
        k0 = pl.multiple_of(kt * tk, tk)
        kb = ki_ref[pl.ds(k0, tk), :]
        acc = None
        for h in range(N_IDX_HEADS):
            r = jnp.dot(kb, qit_ref[h], preferred_element_type=F32)
            term = wit_ref[h:h + 1, :] * jnp.maximum(r, 0.0)
            acc = term if acc is None else acc + term
        acc = jnp.where(jnp.abs(acc) < F32_MIN_NORMAL, 0.0, acc)
        bits = pltpu.bitcast(acc, I32)
        key = bits ^ ((bits >> 31) & 0x7FFFFFFF)
        top = pltpu.bitcast(bits & jnp.int32(-65536), F32)
        if masked:
            adm = (k0 + lax.broadcasted_iota(I32, (tk, tq), 0)) < adm_end
            key = jnp.where(adm, key, INT_MIN)
            top = jnp.where(adm, top, -jnp.inf)
        sc_ref[pl.ds(k0, tk), :] = key
        hi_ref[pl.ds(k0, tk), :] = top.astype(BF16)

    def interior(kt, carry):
        score_tile(kt, False)
        return carry

    def boundary(kt, carry):
        score_tile(kt, True)
        return carry

    n_int = q0 // tk
    lax.fori_loop(0, n_int, interior, 0)
    lax.fori_loop(n_int, nkt, boundary, 0)

    row_kt = lax.broadcasted_iota(I32, (_KT, tq), 0)

    def count(pred):
        def body(c, acc):
            r0 = pl.multiple_of(c * _KT, _KT)
            key = sc_ref[pl.ds(r0, _KT), :]
            ones = jnp.where(pred(key, r0 + row_kt), 1.0, 0.0)
            return acc + _fold(ones.reshape(_KT // SUBLANES, SUBLANES, tq))
        acc = lax.fori_loop(0, nct, body, jnp.zeros((SUBLANES, tq), F32))
        return jnp.sum(acc, axis=0, keepdims=True)

    def count_hi(cand):
        assert tk // PACKED_SUBLANES < 256
        def body(c, acc):
            r0 = pl.multiple_of(c * tk, tk)
            ones = jnp.where(hi_ref[pl.ds(r0, tk), :] >= cand, jnp.ones((), BF16), jnp.zeros((), BF16))
            part = _fold(ones.reshape(tk // PACKED_SUBLANES, PACKED_SUBLANES, tq))
            return acc + part.astype(F32)
        acc = lax.fori_loop(0, nkt, body, jnp.zeros((PACKED_SUBLANES, tq), F32))
        return jnp.sum(acc, axis=0, keepdims=True)

    def emit(keep):
        def body(c, carry):
            r0 = pl.multiple_of(c * _KT, _KT)
            key = sc_ref[pl.ds(r0, _KT), :]
            bias_ref[0, 0, pl.ds(r0, _KT), :] = jnp.where(keep(key, r0 + row_kt), 0.0, NEG_BIAS).astype(BF16)
            return carry
        lax.fori_loop(0, nct, body, 0)

    def hi_pass(p, state):
        u, cnt_u = state
        cand_u = u | (jnp.int32(1) << (15 - p))
        c16 = cand_u - 32768
        c16 = jnp.where(jnp.logical_and(c16 > 0, c16 < BF16_MIN_NORMAL_BITS), BF16_MIN_NORMAL_BITS, c16)
        raw = jnp.where(c16 >= 0, c16, c16 ^ 0x7FFF) << 16
        cnt = count_hi(pltpu.bitcast(raw, F32).astype(BF16))
        ok = cnt >= kf
        return jnp.where(ok, cand_u, u), jnp.where(ok, cnt, cnt_u)

    many = jnp.full((1, tq), float(seq + 1), F32)
    u, cnt_u = lax.fori_loop(0, 16, hi_pass, (jnp.zeros((1, tq), I32), many))

    few = adm_end <= topk

    def unsettled(cnt):
        return jnp.max(jnp.where(jnp.logical_or(few, cnt == kf), 0.0, 1.0)) > 0.0

    def lo_cond(state):
        p, _, _, go = state
        return jnp.logical_and(p < 16, go)

    def lo_pass(state):
        p, tau, cnt_t, _ = state
        cand = tau | (jnp.int32(1) << (15 - p))
        cnt = count(lambda key, kidx: key >= cand)
        ok = cnt >= kf
        cnt_t = jnp.where(ok, cnt, cnt_t)
        return p + 1, jnp.where(ok, cand, tau), cnt_t, unsettled(cnt_t)

    _, tau, cnt_ge, _ = lax.while_loop(lo_cond, lo_pass, (jnp.int32(0), (u - 32768) << 16, cnt_u, unsettled(cnt_u)))
    tau = jnp.maximum(tau, INT_MIN + 1)
    has_tie = jnp.max(jnp.where(few, 0.0, cnt_ge)) > kf

    def fill(c, carry):
        r0 = pl.multiple_of(c * _KT, _KT)
        bias_ref[0, 0, pl.ds(r0, _KT), :] = jnp.full((_KT, tq), NEG_BIAS, BF16)
        return carry

    lax.fori_loop(nct, seq // _KT, fill, 0)

    @pl.when(jnp.logical_not(has_tie))
    def _():
        emit(lambda key, kidx: key >= tau)

    @pl.when(has_tie)
    def _():
        need = kf - count(lambda key, kidx: key > tau)
        nbits = max(1, (seq - 1).bit_length())

        def idx_pass(p, j0):
            cand = j0 + (jnp.int32(1) << (nbits - 1 - p))
            c_lt = count(lambda key, kidx: jnp.logical_and(key == tau, kidx < cand))
            return jnp.where(c_lt < need, cand, j0)

        j0 = lax.fori_loop(0, nbits, idx_pass, jnp.zeros((1, tq), I32))
        emit(lambda key, kidx: jnp.logical_or(key > tau, jnp.logical_and(key == tau, kidx <= j0)))


def _topk_mask(qit, ki, wit, batch, seq, topk, tq, tk, tq_attn):
    assert tq % _KT == 0 and tq_attn % tq == 0 and seq % tk == 0
    nq = seq // tq
    per = tq_attn // tq
    kern = functools.partial(_topk_mask_kernel, tq=tq, tk=tk, topk=topk, seq=seq)
    return pl.pallas_call(
        kern,
        grid=(batch, nq),
        in_specs=[
            pl.BlockSpec((N_IDX_HEADS, IDX_DIM, tq), lambda b, i: (0, 0, b * nq + i)),
            pl.BlockSpec((seq, IDX_DIM), lambda b, i: (b, 0)),
            pl.BlockSpec((N_IDX_HEADS, tq), lambda b, i: (0, b * nq + i)),
        ],
        out_specs=pl.BlockSpec((1, 1, seq, tq), lambda b, i: (b, i // per, 0, i % per)),
        out_shape=jax.ShapeDtypeStruct((batch, seq // tq_attn, seq, tq_attn), BF16),
        scratch_shapes=[pltpu.VMEM((seq, tq), I32), pltpu.VMEM((seq, tq), BF16)],
        compiler_params=_params(("arbitrary", "arbitrary")),
        name="topk_mask",
    )(qit, ki, wit)


_HP = 2


def _attn_kernel(qi_ref, kj_ref, qt_ref, k_ref, vt_ref, bias_ref, o_ref,
                 m_ref, l_ref, acc_ref, sa_ref, sb_ref, ma_ref, mb_ref, bf_ref, *, tq, tk):
    n = pl.program_id(1)
    i = qi_ref[n]
    kt = kj_ref[n]
    last = (i * tq + tq + tk - 1) // tk - 1
    nstage = N_HEADS // _HP

    @pl.when(kt == 0)
    def _():
        m_ref[...] = jnp.full(m_ref.shape, NEG_BIAS, F32)
        l_ref[...] = jnp.zeros(l_ref.shape, F32)
        acc_ref[...] = jnp.zeros(acc_ref.shape, F32)

    bf_ref[...] = bias_ref[0, 0].astype(F32)

    def scores(j, buf, mx):
        for r in range(_HP):
            h = j * _HP + r
            sb = jnp.dot(k_ref[h // GROUP], qt_ref[h], preferred_element_type=F32) + bf_ref[...]
            buf[r] = sb
            mx[r] = jnp.max(sb, axis=0, keepdims=True)

    def softmax_pv(j, buf, mx):
        for r in range(_HP):
            h = j * _HP + r
            m_prev = m_ref[h]
            m_new = jnp.maximum(m_prev, mx[r])
            alpha = jnp.exp2(m_prev - m_new)
            p = jnp.exp2(buf[r] - m_new).astype(BF16)
            pv = jnp.dot(vt_ref[h // GROUP], p, preferred_element_type=F32)
            l_ref[h] = alpha * l_ref[h] + pv[HEAD_DIM:HEAD_DIM + 1]
            acc_ref[h] = alpha * acc_ref[h] + pv[:HEAD_DIM]
            m_ref[h] = m_new

    scores(0, sa_ref, ma_ref)

    def two_stages(jj, carry):
        j = 2 * jj
        scores(j + 1, sb_ref, mb_ref)
        softmax_pv(j, sa_ref, ma_ref)
        scores(j + 2, sa_ref, ma_ref)
        softmax_pv(j + 1, sb_ref, mb_ref)
        return carry

    lax.fori_loop(0, nstage // 2 - 1, two_stages, 0)
    scores(nstage - 1, sb_ref, mb_ref)
    softmax_pv(nstage - 2, sa_ref, ma_ref)
    softmax_pv(nstage - 1, sb_ref, mb_ref)

    @pl.when(kt == last)
    def _():
        for h in range(N_HEADS):
            o_ref[:, HEAD_DIM * h:HEAD_DIM * (h + 1)] = (acc_ref[h] / l_ref[h]).T.astype(BF16)


def _attention(qt, k, vt, bias, batch, seq, tq, tk):
    nq = seq // tq
    nk = seq // tk
    t = batch * seq
    pairs = [(i, j) for i in range(nq) for j in range((i * tq + tq + tk - 1) // tk)]
    qi = jnp.asarray([p[0] for p in pairs], I32)
    kj = jnp.asarray([p[1] for p in pairs], I32)
    kern = functools.partial(_attn_kernel, tq=tq, tk=tk)
    grid_spec = pltpu.PrefetchScalarGridSpec(
        num_scalar_prefetch=2,
        grid=(batch, len(pairs)),
        in_specs=[
            pl.BlockSpec((N_HEADS, HEAD_DIM, tq), lambda b, n, qi, kj: (0, 0, b * nq + qi[n])),
            pl.BlockSpec((N_KV_HEADS, tk, HEAD_DIM), lambda b, n, qi, kj: (0, b * nk + kj[n], 0)),
            pl.BlockSpec((N_KV_HEADS, V_ROWS, tk), lambda b, n, qi, kj: (0, 0, b * nk + kj[n])),
            pl.BlockSpec((1, 1, tk, tq), lambda b, n, qi, kj: (b, qi[n], kj[n], 0)),
        ],
        out_specs=pl.BlockSpec((tq, Q_DIM), lambda b, n, qi, kj: (b * nq + qi[n], 0)),
        scratch_shapes=[
            pltpu.VMEM((N_HEADS, 1, tq), F32),
            pltpu.VMEM((N_HEADS, 1, tq), F32),
            pltpu.VMEM((N_HEADS, HEAD_DIM, tq), F32),
            pltpu.VMEM((_HP, tk, tq), F32),
            pltpu.VMEM((_HP, tk, tq), F32),
            pltpu.VMEM((_HP, 1, tq), F32),
            pltpu.VMEM((_HP, 1, tq), F32),
            pltpu.VMEM((tk, tq), F32),
        ],
    )
    return pl.pallas_call(
        kern,
        grid_spec=grid_spec,
        out_shape=jax.ShapeDtypeStruct((t, Q_DIM), BF16),
        compiler_params=_params(("arbitrary", "arbitrary")),
        name="masked_attention",
    )(qi, kj, qt, k, vt, bias)


def _proj_residual_kernel(h_ref, x_ref, w_ref, *rest, has_bias, tn):
    if has_bias:
        b_ref, o_ref = rest
    else:
        (o_ref,) = rest
    x = x_ref[...]
    for c in range(w_ref.shape[1] // tn):
        cols = slice(c * tn, (c + 1) * tn)
        y = h_ref[:, cols] + jnp.dot(x, w_ref[:, cols], preferred_element_type=F32)
        if has_bias:
            y = y + b_ref[:, cols]
        o_ref[:, cols] = y


def _proj_residual(h, x, w, b, tm, name):
    t, d = h.shape
    kdim = x.shape[1]
    row = lambda i: (i, 0)
    in_specs = [pl.BlockSpec((tm, d), row), pl.BlockSpec((tm, kdim), row), _resident(w.shape)]
    args = [h, x, w]
    if b is not None:
        in_specs.append(_resident((1, d)))
        args.append(b)
    kern = functools.partial(_proj_residual_kernel, has_bias=b is not None, tn=512)
    return pl.pallas_call(
        kern,
        grid=(t // tm,),
        in_specs=in_specs,
        out_specs=pl.BlockSpec((tm, d), row),
        out_shape=jax.ShapeDtypeStruct((t, d), F32),
        compiler_params=_params(("arbitrary",)),
        name=name,
    )(*args)


def _mlp_kernel(h_ref, g_ref, w1_ref, w2_ref, o_ref, u_ref):
    @pl.when(pl.program_id(1) == 0)
    def _():
        h = h_ref[...]
        u_ref[...] = _rms(h, g_ref[...]).astype(BF16)
        o_ref[...] = h

    a = jnp.dot(u_ref[...], w1_ref[...], preferred_element_type=F32)
    a = jnp.square(jnp.maximum(a, 0.0)).astype(BF16)
    o_ref[...] += jnp.dot(a, w2_ref[...], preferred_element_type=F32)


def _mlp(h, g, w1, w2, tm, tf):
    t, d = h.shape
    f = w1.shape[1]
    return pl.pallas_call(
        _mlp_kernel,
        grid=(t // tm, f // tf),
        in_specs=[
            pl.BlockSpec((tm, d), lambda i, j: (i, 0)),
            _resident((1, d)),
            pl.BlockSpec((d, tf), lambda i, j: (0, j)),
            pl.BlockSpec((tf, d), lambda i, j: (j, 0)),
        ],
        out_specs=pl.BlockSpec((tm, d), lambda i, j: (i, 0)),
        out_shape=jax.ShapeDtypeStruct((t, d), F32),
        scratch_shapes=[pltpu.VMEM((tm, d), BF16)],
        compiler_params=_params(("arbitrary", "arbitrary")),
        name="sqrelu_mlp",
    )(h, g, w1, w2)


def _pe_kernel(h_ref, g_ref, wg_ref, p_ref, wp_ref, *rest, final, tn):
    if final:
        gf_ref, o_ref = rest
    else:
        (o_ref,) = rest
    h = h_ref[...]
    u = _rms(h, g_ref[...]).astype(BF16)
    pb = p_ref[...].astype(BF16)
    for c in range(wg_ref.shape[1] // tn):
        cols = slice(c * tn, (c + 1) * tn)
        gate = jax.nn.sigmoid(jnp.dot(u, wg_ref[:, cols], preferred_element_type=F32))
        emb = jnp.dot(pb, wp_ref[:, cols], preferred_element_type=F32)
        o_ref[:, cols] = h_ref[:, cols] + emb * gate
    if final:
        o_ref[...] = _rms(o_ref[...], gf_ref[...])


def _pe(h, g, wg, p, layer, wp, gf, tm):
    t, d = h.shape
    row = lambda i: (i, 0)
    in_specs = [pl.BlockSpec((tm, d), row), _resident((1, d)), _resident(wg.shape),
                pl.BlockSpec((None, tm, p.shape[2]), lambda r: (layer, r, 0)), _resident(wp.shape)]
    args = [h, g, wg, p, wp]
    if gf is not None:
        in_specs.append(_resident((1, d)))
        args.append(gf)
    kern = functools.partial(_pe_kernel, final=gf is not None, tn=512)
    return pl.pallas_call(
        kern,
        grid=(t // tm,),
        in_specs=in_specs,
        out_specs=pl.BlockSpec((tm, d), row),
        out_shape=jax.ShapeDtypeStruct((t, d), F32),
        compiler_params=_params(("arbitrary",)),
        name="gated_embedding",
    )(*args)


def _conv_in_kernel(h_ref, g_ref, w_ref, b_ref, y_ref, *, tn):
    d = y_ref.shape[1]
    u = _rms(h_ref[...], g_ref[...]).astype(BF16)
    for c in range(d // tn):
        a = jnp.dot(u, w_ref[:, c * tn:(c + 1) * tn], preferred_element_type=F32) + b_ref[:, c * tn:(c + 1) * tn]
        gt = (jnp.dot(u, w_ref[:, d + c * tn:d + (c + 1) * tn], preferred_element_type=F32)
              + b_ref[:, d + c * tn:d + (c + 1) * tn])
        y_ref[:, c * tn:(c + 1) * tn] = a * jax.nn.sigmoid(gt)


def _conv_in(h, g, w, b, tm):
    t, d = h.shape
    row = lambda i: (i, 0)
    return pl.pallas_call(
        functools.partial(_conv_in_kernel, tn=512),
        grid=(t // tm,),
        in_specs=[pl.BlockSpec((tm, d), row), _resident((1, d)), _resident(w.shape), _resident(b.shape)],
        out_specs=pl.BlockSpec((tm, d), row),
        out_shape=jax.ShapeDtypeStruct((t, d), F32),
        compiler_params=_params(("arbitrary",)),
        name="conv_in_glu",
    )(h, g, w, b)


_HALO = 32
_CONV_RS = 64


def _dwconv_kernel(y_ref, halo_ref, w_ref, bdw_ref, lg_ref, lb_ref, z_ref, ext_ref, cv_ref, *, tm):
    d = y_ref.shape[2]
    nc = d // LANES
    first = pl.program_id(1) == 0
    for c in range(nc):
        cols = slice(c * LANES, (c + 1) * LANES)
        ext_ref[c, 0:_HALO] = jnp.where(first, 0.0, halo_ref[0, :, cols])
        ext_ref[c, _HALO:_HALO + tm] = y_ref[0, :, cols]

    off = _HALO - (CONV_WIDTH - 1)

    def chunk(c, carry):
        for rs in range(tm // _CONV_RS):
            r0 = rs * _CONV_RS
            acc = jnp.broadcast_to(bdw_ref[c], (_CONV_RS, LANES))
            for j in range(CONV_WIDTH):
                acc = acc + w_ref[c, j:j + 1, :] * ext_ref[c, r0 + off + j:r0 + off + j + _CONV_RS, :]
            cv_ref[c, r0:r0 + _CONV_RS] = acc
        return carry

    lax.fori_loop(0, nc, chunk, 0)

    s1 = cv_ref[0]
    for c in range(1, nc):
        s1 = s1 + cv_ref[c]
    mu = jnp.sum(s1, axis=1, keepdims=True) / d
    s2 = jnp.square(cv_ref[0] - mu)
    for c in range(1, nc):
        s2 = s2 + jnp.square(cv_ref[c] - mu)
    rstd = lax.rsqrt(jnp.sum(s2, axis=1, keepdims=True) / d + NORM_EPS)
    for c in range(nc):
        yn = (cv_ref[c] - mu) * rstd * lg_ref[c] + lb_ref[c]
        z_ref[0, :, c * LANES:(c + 1) * LANES] = (yn * jax.nn.sigmoid(yn)).astype(BF16)


def _dwconv_ln_swish(y3, w3, bdw3, lg3, lb3, tm):
    batch, seq, d = y3.shape
    nc = d // LANES
    hb = tm // _HALO
    kern = functools.partial(_dwconv_kernel, tm=tm)
    return pl.pallas_call(
        kern,
        grid=(batch, seq // tm),
        in_specs=[
            pl.BlockSpec((1, tm, d), lambda b, i: (b, i, 0)),
            pl.BlockSpec((1, _HALO, d), lambda b, i: (b, jnp.maximum(i * hb - 1, 0), 0)),
            _resident(w3.shape), _resident(bdw3.shape), _resident(lg3.shape), _resident(lb3.shape),
        ],
        out_specs=pl.BlockSpec((1, tm, d), lambda b, i: (b, i, 0)),
        out_shape=jax.ShapeDtypeStruct((batch, seq, d), BF16),
        scratch_shapes=[
            pltpu.VMEM((nc, _HALO + tm, LANES), F32),
            pltpu.VMEM((nc, tm, LANES), F32),
        ],
        compiler_params=_params(("arbitrary", "arbitrary")),
        name="dwconv_ln_swish",
    )(y3, y3, w3, bdw3, lg3, lb3)


def _rope_tables(seq):
    rd = HEAD_DIM // ROPE_FRAC
    half = rd // 2
    pos = jnp.arange(seq, dtype=F32)
    inv = ROPE_THETA ** (-jnp.arange(half, dtype=F32) * 2.0 / rd)
    ang = pos[:, None] * inv[None, :]
    cos = jnp.cos(ang)
    sin = jnp.sin(ang)
    ones = jnp.ones((seq, HEAD_DIM - rd), F32)
    zeros = jnp.zeros((seq, HEAD_DIM - rd), F32)
    zh = jnp.zeros((seq, half), F32)
    cos_t = jnp.concatenate([cos, cos, ones], axis=1)
    sa_t = jnp.concatenate([-sin, zh, zeros], axis=1)
    sb_t = jnp.concatenate([zh, sin, zeros], axis=1)
    return cos_t, sa_t, sb_t


def _chunked(v, nc):
    v2 = v.reshape(-1, v.shape[-1])
    return v2.reshape(v2.shape[0], nc, LANES).transpose(1, 0, 2)


def kernel(x, p, mix_norm, mlp_norm, mlp_w1, mlp_w2, pe_proj, pe_gate_norm, pe_gate, dsa_w_in, dsa_w_out, conv_w_in, conv_b_in, conv_w_dw, conv_b_dw, conv_ln_g, conv_ln_b, conv_w_out, conv_b_out, final_norm):
    batch, seq, d = x.shape
    depth = p.shape[0]
    t = batch * seq
    topk = min(TOPK_MAX, seq // KEY_FRAC)
    tm = 512
    h = x.reshape(t, d)
    row_vec = lambda v: v.reshape(1, -1)

    for i in range(depth):
        j = i // 2
        if i % 2 == 0:
            w = dsa_w_in[j]
            wq, wk, wv, wqi, wki, wwi = jnp.split(
                w, [Q_DIM, Q_DIM + KV_DIM, Q_DIM + 2 * KV_DIM, Q_DIM + 2 * KV_DIM + IDXQ_DIM,
                    Q_DIM + 2 * KV_DIM + IDXQ_DIM + IDX_DIM], axis=1)
            wwi = jnp.pad(wwi, ((0, 0), (0, LANES - N_IDX_HEADS)))
            w_all = jnp.concatenate([wq, wk, wqi, wki, wv, wwi], axis=1).astype(BF16)
            cos_t, sa_t, sb_t = _rope_tables(seq)
            qt, k, vt, qit, ki, wit = _dsa_proj(h, row_vec(mix_norm[i]), w_all, cos_t, sa_t, sb_t, seq, tm)
            tq_attn = min(512, seq)
            bias = _topk_mask(qit, ki, wit, batch, seq, topk, tq=256, tk=512, tq_attn=tq_attn)
            o = _attention(qt, k, vt, bias, batch, seq, tq=tq_attn, tk=512)
            h = _proj_residual(h, o, dsa_w_out[j].astype(BF16), None, tm, "dsa_out")
        else:
            nc = d // LANES
            y = _conv_in(h, row_vec(mix_norm[i]), conv_w_in[j].astype(BF16), row_vec(conv_b_in[j]), tm)
            w3 = _chunked(jnp.pad(conv_w_dw[j], ((0, _HALO - CONV_WIDTH), (0, 0))), nc)
            z = _dwconv_ln_swish(y.reshape(batch, seq, d), w3, _chunked(conv_b_dw[j], nc),
                                 _chunked(conv_ln_g[j], nc), _chunked(conv_ln_b[j], nc), tm=256)
            h = _proj_residual(h, z.reshape(t, d), conv_w_out[j].astype(BF16), row_vec(conv_b_out[j]), tm,
                               "conv_out")
        h = _mlp(h, row_vec(mlp_norm[i]), mlp_w1[i].astype(BF16), mlp_w2[i].astype(BF16), tm=512, tf=1024)
        gf = row_vec(final_norm) if i == depth - 1 else None
        h = _pe(h, row_vec(pe_gate_norm[i]), pe_gate[i].astype(BF16), p.reshape(depth, t, -1), i,
                pe_proj[i].astype(BF16), gf, tm)
    return h.reshape(batch, seq, d)
```

```python
import functools
import math

import jax
import jax.numpy as jnp
from jax import lax
from jax.experimental import pallas as pl
from jax.experimental.pallas import tpu as pltpu

F32 = jnp.float32
BF16 = jnp.bfloat16
I32 = jnp.int32

CHUNK = 64
N_HEADS = 16
HEAD_DIM = 128
N_KV_HEADS = 4
GROUP = N_HEADS // N_KV_HEADS
N_IDX_HEADS = 8
IDX_DIM = 128
TOPK_MAX = 256
KEY_FRAC = 4
ROPE_THETA = 500000.0
ROPE_FRAC = 4
CONV_WIDTH = 31
NORM_EPS = 1e-6
Q_DIM = N_HEADS * HEAD_DIM
KV_DIM = N_KV_HEADS * HEAD_DIM
IDXQ_DIM = N_IDX_HEADS * IDX_DIM

LANES = 128
SUBLANES = 8
PACKED_SUBLANES = 16
V_ROWS = HEAD_DIM + PACKED_SUBLANES
F32_MIN_NORMAL = 1.1754943508222875e-38
BF16_MIN_NORMAL_BITS = 0x0080
INT_MIN = -(2 ** 31)
NEG_BIAS = -1e30
LOG2E = math.log2(math.e)

VMEM_LIMIT = 56 * 1024 * 1024


def _params(sem, vmem=VMEM_LIMIT):
    return pltpu.CompilerParams(dimension_semantics=sem, vmem_limit_bytes=vmem)


def _resident(shape):
    nd = len(shape)
    return pl.BlockSpec(shape, lambda *_: (0,) * nd, pipeline_mode=pl.Buffered(1))


def _rms(x, g):
    ms = jnp.mean(x * x, axis=-1, keepdims=True)
    return x * lax.rsqrt(ms + NORM_EPS) * g


_C_Q = 0
_C_K = _C_Q + Q_DIM
_C_QI = _C_K + KV_DIM
_C_KI = _C_QI + IDXQ_DIM
_C_V = _C_KI + IDX_DIM
_C_WI = _C_V + KV_DIM
_C_END = _C_WI + LANES


def _dsa_proj_kernel(h_ref, g_ref, w_ref, cos_ref, sa_ref, sb_ref,
                     qt_ref, k_ref, vt_ref, qit_ref, ki_ref, wit_ref):
    u = _rms(h_ref[...], g_ref[...]).astype(BF16)
    cos = cos_ref[...]
    sa = sa_ref[...]
    sb = sb_ref[...]
    rd = HEAD_DIM // ROPE_FRAC
    half = rd // 2

    def rope(y):
        return (y * cos + pltpu.roll(y, LANES - half, 1) * sa + pltpu.roll(y, half, 1) * sb)

    def proj(c0, n):
        return jnp.dot(u, w_ref[:, c0:c0 + n], preferred_element_type=F32)

    qscale = (HEAD_DIM ** -0.5) * LOG2E
    for c in range(N_HEADS // 4):
        y = proj(_C_Q + 512 * c, 512)
        for j in range(4):
            qt_ref[4 * c + j] = (rope(y[:, 128 * j:128 * (j + 1)]) * qscale).T.astype(BF16)
    y = proj(_C_K, KV_DIM)
    for j in range(N_KV_HEADS):
        k_ref[j] = rope(y[:, 128 * j:128 * (j + 1)]).astype(BF16)
    for c in range(N_IDX_HEADS // 4):
        y = proj(_C_QI + 512 * c, 512)
        for j in range(4):
            qit_ref[4 * c + j] = (rope(y[:, 128 * j:128 * (j + 1)]) * (IDX_DIM ** -0.5)).T.astype(BF16)
    ki_ref[...] = rope(proj(_C_KI, IDX_DIM)).astype(BF16)
    y = proj(_C_V, KV_DIM)
    for j in range(N_KV_HEADS):
        vt_ref[j, 0:HEAD_DIM, :] = y[:, 128 * j:128 * (j + 1)].T.astype(BF16)
        vt_ref[j, HEAD_DIM:V_ROWS, :] = jnp.ones((V_ROWS - HEAD_DIM, vt_ref.shape[2]), BF16)
    wit_ref[...] = (proj(_C_WI, LANES) * (N_IDX_HEADS ** -0.5)).T[:N_IDX_HEADS]


def _dsa_proj(h, g, w_all, cos, sa, sb, seq, tm):
    t, d = h.shape
    nseq = seq // tm
    row = lambda i: (i, 0)
    tab = pl.BlockSpec((tm, LANES), lambda i: (i % nseq, 0))
    return pl.pallas_call(
        _dsa_proj_kernel,
        grid=(t // tm,),
        in_specs=[pl.BlockSpec((tm, d), row), _resident((1, d)), _resident(w_all.shape), tab, tab, tab],
        out_specs=[
            pl.BlockSpec((N_HEADS, HEAD_DIM, tm), lambda i: (0, 0, i)),
            pl.BlockSpec((N_KV_HEADS, tm, HEAD_DIM), lambda i: (0, i, 0)),
            pl.BlockSpec((N_KV_HEADS, V_ROWS, tm), lambda i: (0, 0, i)),
            pl.BlockSpec((N_IDX_HEADS, IDX_DIM, tm), lambda i: (0, 0, i)),
            pl.BlockSpec((tm, IDX_DIM), row),
            pl.BlockSpec((N_IDX_HEADS, tm), lambda i: (0, i)),
        ],
        out_shape=[
            jax.ShapeDtypeStruct((N_HEADS, HEAD_DIM, t), BF16),
            jax.ShapeDtypeStruct((N_KV_HEADS, t, HEAD_DIM), BF16),
            jax.ShapeDtypeStruct((N_KV_HEADS, V_ROWS, t), BF16),
            jax.ShapeDtypeStruct((N_IDX_HEADS, IDX_DIM, t), BF16),
            jax.ShapeDtypeStruct((t, IDX_DIM), BF16),
            jax.ShapeDtypeStruct((N_IDX_HEADS, t), F32),
        ],
        compiler_params=_params(("arbitrary",)),
        name="dsa_proj",
    )(h, g, w_all, cos, sa, sb)


_KT = 256


def _fold(y):
    n = y.shape[0]
    parts = [y[j] for j in range(min(4, n))]
    for j in range(4, n):
        parts[j % 4] = parts[j % 4] + y[j]
    while len(parts) > 1:
        parts = [parts[j] + parts[j + 1] if j + 1 < len(parts) else parts[j] for j in range(0, len(parts), 2)]
    return parts[0]


def _topk_mask_kernel(qit_ref, ki_ref, wit_ref, bias_ref, sc_ref, hi_ref, *, tq, tk, topk, seq):
    i = pl.program_id(1)
    q0 = i * tq
    n_adm = q0 + tq
    nkt = (n_adm + tk - 1) // tk
    nct = n_adm // _KT
    kf = float(topk)
    lane = lax.broadcasted_iota(I32, (1, tq), 1)
    adm_end = q0 + (lane // CHUNK + 1) * CHUNK

    def score_tile(kt, masked):
        k0 = pl.multiple_of(kt * tk, tk)
        kb = ki_ref[pl.ds(k0, tk), :]
        acc = None
        for h in range(N_IDX_HEADS):
            r = jnp.dot(kb, qit_ref[h], preferred_element_type=F32)
            term = wit_ref[h:h + 1, :] * jnp.maximum(r, 0.0)
            acc = term if acc is None else acc + term
        acc = jnp.where(jnp.abs(acc) < F32_MIN_NORMAL, 0.0, acc)
        bits = pltpu.bitcast(acc, I32)
        key = bits ^ ((bits >> 31) & 0x7FFFFFFF)
        top = pltpu.bitcast(bits & jnp.int32(-65536), F32)
        if masked:
            adm = (k0 + lax.broadcasted_iota(I32, (tk, tq), 0)) < adm_end
            key = jnp.where(adm, key, INT_MIN)
            top = jnp.where(adm, top, -jnp.inf)
        sc_ref[pl.ds(k0, tk), :] = key
        hi_ref[pl.ds(k0, tk), :] = top.astype(BF16)

    def interior(kt, carry):
        score_tile(kt, False)
        return carry

    def boundary(kt, carry):
        score_tile(kt, True)
        return carry

    n_int = q0 // tk
    lax.fori_loop(0, n_int, interior, 0)
    lax.fori_loop(n_int, nkt, boundary, 0)

    row_kt = lax.broadcasted_iota(I32, (_KT, tq), 0)

    def count(pred):
        def body(c, acc):
            r0 = pl.multiple_of(c * _KT, _KT)
            key = sc_ref[pl.ds(r0, _KT), :]
            ones = jnp.where(pred(key, r0 + row_kt), 1.0, 0.0)
            return acc + _fold(ones.reshape(_KT // SUBLANES, SUBLANES, tq))
        acc = lax.fori_loop(0, nct, body, jnp.zeros((SUBLANES, tq), F32))
        return jnp.sum(acc, axis=0, keepdims=True)

    def count_hi(cand):
        assert tk // PACKED_SUBLANES < 256
        def body(c, acc):
            r0 = pl.multiple_of(c * tk, tk)
            ones = jnp.where(hi_ref[pl.ds(r0, tk), :] >= cand, jnp.ones((), BF16), jnp.zeros((), BF16))
            part = _fold(ones.reshape(tk // PACKED_SUBLANES, PACKED_SUBLANES, tq))
            return acc + part.astype(F32)
        acc = lax.fori_loop(0, nkt, body, jnp.zeros((PACKED_SUBLANES, tq), F32))
        return jnp.sum(acc, axis=0, keepdims=True)

    def emit(keep):
        def body(c, carry):
            r0 = pl.multiple_of(c * _KT, _KT)
            key = sc_ref[pl.ds(r0, _KT), :]
            bias_ref[0, 0, pl.ds(r0, _KT), :] = jnp.where(keep(key, r0 + row_kt), 0.0, NEG_BIAS).astype(BF16)
            return carry
        lax.fori_loop(0, nct, body, 0)

    def hi_pass(p, state):
        u, cnt_u = state
        cand_u = u | (jnp.int32(1) << (15 - p))
        c16 = cand_u - 32768
        c16 = jnp.where(jnp.logical_and(c16 > 0, c16 < BF16_MIN_NORMAL_BITS), BF16_MIN_NORMAL_BITS, c16)
        raw = jnp.where(c16 >= 0, c16, c16 ^ 0x7FFF) << 16
        cnt = count_hi(pltpu.bitcast(raw, F32).astype(BF16))
        ok = cnt >= kf
        return jnp.where(ok, cand_u, u), jnp.where(ok, cnt, cnt_u)

    many = jnp.full((1, tq), float(seq + 1), F32)
    u, cnt_u = lax.fori_loop(0, 16, hi_pass, (jnp.zeros((1, tq), I32), many))

    few = adm_end <= topk

    def unsettled(cnt):
        return jnp.max(jnp.where(jnp.logical_or(few, cnt == kf), 0.0, 1.0)) > 0.0

    def lo_cond(state):
        p, _, _, go = state
        return jnp.logical_and(p < 16, go)

    def lo_pass(state):
        p, tau, cnt_t, _ = state
        cand = tau | (jnp.int32(1) << (15 - p))
        cnt = count(lambda key, kidx: key >= cand)
        ok = cnt >= kf
        cnt_t = jnp.where(ok, cnt, cnt_t)
        return p + 1, jnp.where(ok, cand, tau), cnt_t, unsettled(cnt_t)

    _, tau, cnt_ge, _ = lax.while_loop(lo_cond, lo_pass, (jnp.int32(0), (u - 32768) << 16, cnt_u, unsettled(cnt_u)))
    tau = jnp.maximum(tau, INT_MIN + 1)
    has_tie = jnp.max(jnp.where(few, 0.0, cnt_ge)) > kf

    def fill(c, carry):
        r0 = pl.multiple_of(c * _KT, _KT)
        bias_ref[0, 0, pl.ds(r0, _KT), :] = jnp.full((_KT, tq), NEG_BIAS, BF16)
        return carry

    lax.fori_loop(nct, seq // _KT, fill, 0)

    @pl.when(jnp.logical_not(has_tie))
    def _():
        emit(lambda key, kidx: key >= tau)

    @pl.when(has_tie)
    def _():
        tied = jnp.logical_and(jnp.logical_not(few), cnt_ge > kf)
        need = jnp.where(tied, kf - count(lambda key, kidx: key > tau), 0.0)
        max_need = jnp.max(need).astype(I32)
        nbits = max(1, (seq - 1).bit_length())

        def next_tied(after):
            def body(c, acc):
                r0 = pl.multiple_of(c * _KT, _KT)
                key = sc_ref[pl.ds(r0, _KT), :]
                kidx = r0 + row_kt
                v = jnp.where(jnp.logical_and(key == tau, kidx > after), kidx, seq)
                v = v.reshape(_KT // SUBLANES, SUBLANES, tq)
                for j in range(_KT // SUBLANES):
                    acc = jnp.minimum(acc, v[j])
                return acc
            acc = lax.fori_loop(0, nct, body, jnp.full((SUBLANES, tq), seq, I32))
            return jnp.min(acc.astype(F32), axis=0, keepdims=True).astype(I32)

        def by_scan():
            def step(r, j0):
                return jnp.where(r.astype(F32) < need, next_tied(j0), j0)
            return lax.fori_loop(0, max_need, step, jnp.full((1, tq), -1, I32))

        def by_bisect():
            def idx_pass(p, j0):
                cand = j0 + (jnp.int32(1) << (nbits - 1 - p))
                c_lt = count(lambda key, kidx: jnp.logical_and(key == tau, kidx < cand))
                return jnp.where(c_lt < need, cand, j0)
            return lax.fori_loop(0, nbits, idx_pass, jnp.zeros((1, tq), I32))

        j0 = lax.cond(max_need <= nbits, by_scan, by_bisect)
        j0 = jnp.where(tied, j0, seq)
        emit(lambda key, kidx: jnp.logical_or(key > tau, jnp.logical_and(key == tau, kidx <= j0)))


def _topk_mask(qit, ki, wit, batch, seq, topk, tq, tk, tq_attn):
    assert tq % _KT == 0 and tq_attn % tq == 0 and seq % tk == 0
    nq = seq // tq
    per = tq_attn // tq
    kern = functools.partial(_topk_mask_kernel, tq=tq, tk=tk, topk=topk, seq=seq)
    return pl.pallas_call(
        kern,
        grid=(batch, nq),
        in_specs=[
            pl.BlockSpec((N_IDX_HEADS, IDX_DIM, tq), lambda b, i: (0, 0, b * nq + i)),
            pl.BlockSpec((seq, IDX_DIM), lambda b, i: (b, 0)),
            pl.BlockSpec((N_IDX_HEADS, tq), lambda b, i: (0, b * nq + i)),
        ],
        out_specs=pl.BlockSpec((1, 1, seq, tq), lambda b, i: (b, i // per, 0, i % per)),
        out_shape=jax.ShapeDtypeStruct((batch, seq // tq_attn, seq, tq_attn), BF16),
        scratch_shapes=[pltpu.VMEM((seq, tq), I32), pltpu.VMEM((seq, tq), BF16)],
        compiler_params=_params(("arbitrary", "arbitrary")),
        name="topk_mask",
    )(qit, ki, wit)


_HP = 2


def _attn_kernel(qi_ref, kj_ref, qt_ref, k_ref, vt_ref, bias_ref, o_ref,
                 m_ref, l_ref, acc_ref, sa_ref, sb_ref, ma_ref, mb_ref, bf_ref, *, tq, tk):
    n = pl.program_id(1)
    i = qi_ref[n]
    kt = kj_ref[n]
    last = (i * tq + tq + tk - 1) // tk - 1
    nstage = N_HEADS // _HP

    @pl.when(kt == 0)
    def _():
        m_ref[...] = jnp.full(m_ref.shape, NEG_BIAS, F32)
        l_ref[...] = jnp.zeros(l_ref.shape, F32)
        acc_ref[...] = jnp.zeros(acc_ref.shape, F32)

    bf_ref[...] = bias_ref[0, 0].astype(F32)

    def scores(j, buf, mx):
        for r in range(_HP):
            h = j * _HP + r
            sb = jnp.dot(k_ref[h // GROUP], qt_ref[h], preferred_element_type=F32) + bf_ref[...]
            buf[r] = sb
            mx[r] = jnp.max(sb, axis=0, keepdims=True)

    def softmax_pv(j, buf, mx):
        for r in range(_HP):
            h = j * _HP + r
            m_prev = m_ref[h]
            m_new = jnp.maximum(m_prev, mx[r])
            alpha = jnp.exp2(m_prev - m_new)
            p = jnp.exp2(buf[r] - m_new).astype(BF16)
            pv = jnp.dot(vt_ref[h // GROUP], p, preferred_element_type=F32)
            l_ref[h] = alpha * l_ref[h] + pv[HEAD_DIM:HEAD_DIM + 1]
            acc_ref[h] = alpha * acc_ref[h] + pv[:HEAD_DIM]
            m_ref[h] = m_new

    scores(0, sa_ref, ma_ref)

    def two_stages(jj, carry):
        j = 2 * jj
        scores(j + 1, sb_ref, mb_ref)
        softmax_pv(j, sa_ref, ma_ref)
        scores(j + 2, sa_ref, ma_ref)
        softmax_pv(j + 1, sb_ref, mb_ref)
        return carry

    lax.fori_loop(0, nstage // 2 - 1, two_stages, 0)
    scores(nstage - 1, sb_ref, mb_ref)
    softmax_pv(nstage - 2, sa_ref, ma_ref)
    softmax_pv(nstage - 1, sb_ref, mb_ref)

    @pl.when(kt == last)
    def _():
        for h in range(N_HEADS):
            o_ref[:, HEAD_DIM * h:HEAD_DIM * (h + 1)] = (acc_ref[h] / l_ref[h]).T.astype(BF16)


def _attention(qt, k, vt, bias, batch, seq, tq, tk):
    nq = seq // tq
    nk = seq // tk
    t = batch * seq
    pairs = [(i, j) for i in range(nq) for j in range((i * tq + tq + tk - 1) // tk)]
    qi = jnp.asarray([p[0] for p in pairs], I32)
    kj = jnp.asarray([p[1] for p in pairs], I32)
    kern = functools.partial(_attn_kernel, tq=tq, tk=tk)
    grid_spec = pltpu.PrefetchScalarGridSpec(
        num_scalar_prefetch=2,
        grid=(batch, len(pairs)),
        in_specs=[
            pl.BlockSpec((N_HEADS, HEAD_DIM, tq), lambda b, n, qi, kj: (0, 0, b * nq + qi[n])),
            pl.BlockSpec((N_KV_HEADS, tk, HEAD_DIM), lambda b, n, qi, kj: (0, b * nk + kj[n], 0)),
            pl.BlockSpec((N_KV_HEADS, V_ROWS, tk), lambda b, n, qi, kj: (0, 0, b * nk + kj[n])),
            pl.BlockSpec((1, 1, tk, tq), lambda b, n, qi, kj: (b, qi[n], kj[n], 0)),
        ],
        out_specs=pl.BlockSpec((tq, Q_DIM), lambda b, n, qi, kj: (b * nq + qi[n], 0)),
        scratch_shapes=[
            pltpu.VMEM((N_HEADS, 1, tq), F32),
            pltpu.VMEM((N_HEADS, 1, tq), F32),
            pltpu.VMEM((N_HEADS, HEAD_DIM, tq), F32),
            pltpu.VMEM((_HP, tk, tq), F32),
            pltpu.VMEM((_HP, tk, tq), F32),
            pltpu.VMEM((_HP, 1, tq), F32),
            pltpu.VMEM((_HP, 1, tq), F32),
            pltpu.VMEM((tk, tq), F32),
        ],
    )
    return pl.pallas_call(
        kern,
        grid_spec=grid_spec,
        out_shape=jax.ShapeDtypeStruct((t, Q_DIM), BF16),
        compiler_params=_params(("arbitrary", "arbitrary")),
        name="masked_attention",
    )(qi, kj, qt, k, vt, bias)


def _proj_residual_kernel(h_ref, x_ref, w_ref, *rest, has_bias, tn):
    if has_bias:
        b_ref, o_ref = rest
    else:
        (o_ref,) = rest
    x = x_ref[...]
    for c in range(w_ref.shape[1] // tn):
        cols = slice(c * tn, (c + 1) * tn)
        y = h_ref[:, cols] + jnp.dot(x, w_ref[:, cols], preferred_element_type=F32)
        if has_bias:
            y = y + b_ref[:, cols]
        o_ref[:, cols] = y


def _proj_residual(h, x, w, b, tm, name):
    t, d = h.shape
    kdim = x.shape[1]
    row = lambda i: (i, 0)
    in_specs = [pl.BlockSpec((tm, d), row), pl.BlockSpec((tm, kdim), row), _resident(w.shape)]
    args = [h, x, w]
    if b is not None:
        in_specs.append(_resident((1, d)))
        args.append(b)
    kern = functools.partial(_proj_residual_kernel, has_bias=b is not None, tn=512)
    return pl.pallas_call(
        kern,
        grid=(t // tm,),
        in_specs=in_specs,
        out_specs=pl.BlockSpec((tm, d), row),
        out_shape=jax.ShapeDtypeStruct((t, d), F32),
        compiler_params=_params(("arbitrary",)),
        name=name,
    )(*args)


def _mlp_kernel(h_ref, g_ref, w1_ref, w2_ref, o_ref, u_ref):
    @pl.when(pl.program_id(1) == 0)
    def _():
        h = h_ref[...]
        u_ref[...] = _rms(h, g_ref[...]).astype(BF16)
        o_ref[...] = h

    a = jnp.dot(u_ref[...], w1_ref[...], preferred_element_type=F32)
    a = jnp.square(jnp.maximum(a, 0.0)).astype(BF16)
    o_ref[...] += jnp.dot(a, w2_ref[...], preferred_element_type=F32)


def _mlp(h, g, w1, w2, tm, tf):
    t, d = h.shape
    f = w1.shape[1]
    return pl.pallas_call(
        _mlp_kernel,
        grid=(t // tm, f // tf),
        in_specs=[
            pl.BlockSpec((tm, d), lambda i, j: (i, 0)),
            _resident((1, d)),
            pl.BlockSpec((d, tf), lambda i, j: (0, j)),
            pl.BlockSpec((tf, d), lambda i, j: (j, 0)),
        ],
        out_specs=pl.BlockSpec((tm, d), lambda i, j: (i, 0)),
        out_shape=jax.ShapeDtypeStruct((t, d), F32),
        scratch_shapes=[pltpu.VMEM((tm, d), BF16)],
        compiler_params=_params(("arbitrary", "arbitrary")),
        name="sqrelu_mlp",
    )(h, g, w1, w2)


def _pe_kernel(h_ref, g_ref, wg_ref, p_ref, wp_ref, *rest, final, tn):
    if final:
        gf_ref, o_ref = rest
    else:
        (o_ref,) = rest
    h = h_ref[...]
    u = _rms(h, g_ref[...]).astype(BF16)
    pb = p_ref[...].astype(BF16)
    for c in range(wg_ref.shape[1] // tn):
        cols = slice(c * tn, (c + 1) * tn)
        gate = jax.nn.sigmoid(jnp.dot(u, wg_ref[:, cols], preferred_element_type=F32))
        emb = jnp.dot(pb, wp_ref[:, cols], preferred_element_type=F32)
        o_ref[:, cols] = h_ref[:, cols] + emb * gate
    if final:
        o_ref[...] = _rms(o_ref[...], gf_ref[...])


def _pe(h, g, wg, p, layer, wp, gf, tm):
    t, d = h.shape
    row = lambda i: (i, 0)
    in_specs = [pl.BlockSpec((tm, d), row), _resident((1, d)), _resident(wg.shape),
                pl.BlockSpec((None, tm, p.shape[2]), lambda r: (layer, r, 0)), _resident(wp.shape)]
    args = [h, g, wg, p, wp]
    if gf is not None:
        in_specs.append(_resident((1, d)))
        args.append(gf)
    kern = functools.partial(_pe_kernel, final=gf is not None, tn=512)
    return pl.pallas_call(
        kern,
        grid=(t // tm,),
        in_specs=in_specs,
        out_specs=pl.BlockSpec((tm, d), row),
        out_shape=jax.ShapeDtypeStruct((t, d), F32),
        compiler_params=_params(("arbitrary",)),
        name="gated_embedding",
    )(*args)


def _conv_in_kernel(h_ref, g_ref, w_ref, b_ref, y_ref, *, tn):
    d = y_ref.shape[1]
    u = _rms(h_ref[...], g_ref[...]).astype(BF16)
    for c in range(d // tn):
        a = jnp.dot(u, w_ref[:, c * tn:(c + 1) * tn], preferred_element_type=F32) + b_ref[:, c * tn:(c + 1) * tn]
        gt = (jnp.dot(u, w_ref[:, d + c * tn:d + (c + 1) * tn], preferred_element_type=F32)
              + b_ref[:, d + c * tn:d + (c + 1) * tn])
        y_ref[:, c * tn:(c + 1) * tn] = a * jax.nn.sigmoid(gt)


def _conv_in(h, g, w, b, tm):
    t, d = h.shape
    row = lambda i: (i, 0)
    return pl.pallas_call(
        functools.partial(_conv_in_kernel, tn=512),
        grid=(t // tm,),
        in_specs=[pl.BlockSpec((tm, d), row), _resident((1, d)), _resident(w.shape), _resident(b.shape)],
        out_specs=pl.BlockSpec((tm, d), row),
        out_shape=jax.ShapeDtypeStruct((t, d), F32),
        compiler_params=_params(("arbitrary",)),
        name="conv_in_glu",
    )(h, g, w, b)


_HALO = 32
_CONV_RS = 64


def _dwconv_kernel(y_ref, halo_ref, w_ref, bdw_ref, lg_ref, lb_ref, z_ref, ext_ref, cv_ref, *, tm):
    d = y_ref.shape[2]
    nc = d // LANES
    first = pl.program_id(1) == 0
    for c in range(nc):
        cols = slice(c * LANES, (c + 1) * LANES)
        ext_ref[c, 0:_HALO] = jnp.where(first, 0.0, halo_ref[0, :, cols])
        ext_ref[c, _HALO:_HALO + tm] = y_ref[0, :, cols]

    off = _HALO - (CONV_WIDTH - 1)

    def chunk(c, carry):
        for rs in range(tm // _CONV_RS):
            r0 = rs * _CONV_RS
            acc = jnp.broadcast_to(bdw_ref[c], (_CONV_RS, LANES))
            for j in range(CONV_WIDTH):
                acc = acc + w_ref[c, j:j + 1, :] * ext_ref[c, r0 + off + j:r0 + off + j + _CONV_RS, :]
            cv_ref[c, r0:r0 + _CONV_RS] = acc
        return carry

    lax.fori_loop(0, nc, chunk, 0)

    s1 = cv_ref[0]
    for c in range(1, nc):
        s1 = s1 + cv_ref[c]
    mu = jnp.sum(s1, axis=1, keepdims=True) / d
    s2 = jnp.square(cv_ref[0] - mu)
    for c in range(1, nc):
        s2 = s2 + jnp.square(cv_ref[c] - mu)
    rstd = lax.rsqrt(jnp.sum(s2, axis=1, keepdims=True) / d + NORM_EPS)
    for c in range(nc):
        yn = (cv_ref[c] - mu) * rstd * lg_ref[c] + lb_ref[c]
        z_ref[0, :, c * LANES:(c + 1) * LANES] = (yn * jax.nn.sigmoid(yn)).astype(BF16)


def _dwconv_ln_swish(y3, w3, bdw3, lg3, lb3, tm):
    batch, seq, d = y3.shape
    nc = d // LANES
    hb = tm // _HALO
    kern = functools.partial(_dwconv_kernel, tm=tm)
    return pl.pallas_call(
        kern,
        grid=(batch, seq // tm),
        in_specs=[
            pl.BlockSpec((1, tm, d), lambda b, i: (b, i, 0)),
            pl.BlockSpec((1, _HALO, d), lambda b, i: (b, jnp.maximum(i * hb - 1, 0), 0)),
            _resident(w3.shape), _resident(bdw3.shape), _resident(lg3.shape), _resident(lb3.shape),
        ],
        out_specs=pl.BlockSpec((1, tm, d), lambda b, i: (b, i, 0)),
        out_shape=jax.ShapeDtypeStruct((batch, seq, d), BF16),
        scratch_shapes=[
            pltpu.VMEM((nc, _HALO + tm, LANES), F32),
            pltpu.VMEM((nc, tm, LANES), F32),
        ],
        compiler_params=_params(("arbitrary", "arbitrary")),
        name="dwconv_ln_swish",
    )(y3, y3, w3, bdw3, lg3, lb3)


def _rope_tables(seq):
    rd = HEAD_DIM // ROPE_FRAC
    half = rd // 2
    pos = jnp.arange(seq, dtype=F32)
    inv = ROPE_THETA ** (-jnp.arange(half, dtype=F32) * 2.0 / rd)
    ang = pos[:, None] * inv[None, :]
    cos = jnp.cos(ang)
    sin = jnp.sin(ang)
    ones = jnp.ones((seq, HEAD_DIM - rd), F32)
    zeros = jnp.zeros((seq, HEAD_DIM - rd), F32)
    zh = jnp.zeros((seq, half), F32)
    cos_t = jnp.concatenate([cos, cos, ones], axis=1)
    sa_t = jnp.concatenate([-sin, zh, zeros], axis=1)
    sb_t = jnp.concatenate([zh, sin, zeros], axis=1)
    return cos_t, sa_t, sb_t


def _chunked(v, nc):
    v2 = v.reshape(-1, v.shape[-1])
    return v2.reshape(v2.shape[0], nc, LANES).transpose(1, 0, 2)


def kernel(x, p, mix_norm, mlp_norm, mlp_w1, mlp_w2, pe_proj, pe_gate_norm, pe_gate, dsa_w_in, dsa_w_out, conv_w_in, conv_b_in, conv_w_dw, conv_b_dw, conv_ln_g, conv_ln_b, conv_w_out, conv_b_out, final_norm):
    batch, seq, d = x.shape
    depth = p.shape[0]
    t = batch * seq
    topk = min(TOPK_MAX, seq // KEY_FRAC)
    tm = 512
    h = x.reshape(t, d)
    row_vec = lambda v: v.reshape(1, -1)

    for i in range(depth):
        j = i // 2
        if i % 2 == 0:
            w = dsa_w_in[j]
            wq, wk, wv, wqi, wki, wwi = jnp.split(
                w, [Q_DIM, Q_DIM + KV_DIM, Q_DIM + 2 * KV_DIM, Q_DIM + 2 * KV_DIM + IDXQ_DIM,
                    Q_DIM + 2 * KV_DIM + IDXQ_DIM + IDX_DIM], axis=1)
            wwi = jnp.pad(wwi, ((0, 0), (0, LANES - N_IDX_HEADS)))
            w_all = jnp.concatenate([wq, wk, wqi, wki, wv, wwi], axis=1).astype(BF16)
            cos_t, sa_t, sb_t = _rope_tables(seq)
            qt, k, vt, qit, ki, wit = _dsa_proj(h, row_vec(mix_norm[i]), w_all, cos_t, sa_t, sb_t, seq, tm)
            tq_attn = min(512, seq)
            bias = _topk_mask(qit, ki, wit, batch, seq, topk, tq=256, tk=512, tq_attn=tq_attn)
            o = _attention(qt, k, vt, bias, batch, seq, tq=tq_attn, tk=512)
            h = _proj_residual(h, o, dsa_w_out[j].astype(BF16), None, tm, "dsa_out")
        else:
            nc = d // LANES
            y = _conv_in(h, row_vec(mix_norm[i]), conv_w_in[j].astype(BF16), row_vec(conv_b_in[j]), tm)
            w3 = _chunked(jnp.pad(conv_w_dw[j], ((0, _HALO - CONV_WIDTH), (0, 0))), nc)
            z = _dwconv_ln_swish(y.reshape(batch, seq, d), w3, _chunked(conv_b_dw[j], nc),
                                 _chunked(conv_ln_g[j], nc), _chunked(conv_ln_b[j], nc), tm=256)
            h = _proj_residual(h, z.reshape(t, d), conv_w_out[j].astype(BF16), row_vec(conv_b_out[j]), tm,
                               "conv_out")
        h = _mlp(h, row_vec(mlp_norm[i]), mlp_w1[i].astype(BF16), mlp_w2[i].astype(BF16), tm=512, tf=1024)
        gf = row_vec(final_norm) if i == depth - 1 else None
        h = _pe(h, row_vec(pe_gate_norm[i]), pe_gate[i].astype(BF16), p.reshape(depth, t, -1), i,
                pe_proj[i].astype(BF16), gf, tm)
    return h.reshape(batch, seq, d)
```

```python
import functools
import math

import jax
import jax.numpy as jnp
from jax import lax
from jax.experimental import pallas as pl
from jax.experimental.pallas import tpu as pltpu

F32 = jnp.float32
BF16 = jnp.bfloat16
I32 = jnp.int32

CHUNK = 64
N_HEADS = 16
HEAD_DIM = 128
N_KV_HEADS = 4
GROUP = N_HEADS // N_KV_HEADS
N_IDX_HEADS = 8
IDX_DIM = 128
TOPK_MAX = 256
KEY_FRAC = 4
ROPE_THETA = 500000.0
ROPE_FRAC = 4
CONV_WIDTH = 31
NORM_EPS = 1e-6
Q_DIM = N_HEADS * HEAD_DIM
KV_DIM = N_KV_HEADS * HEAD_DIM
IDXQ_DIM = N_IDX_HEADS * IDX_DIM

LANES = 128
SUBLANES = 8
PACKED_SUBLANES = 16
V_ROWS = HEAD_DIM + PACKED_SUBLANES
F32_MIN_NORMAL = 1.1754943508222875e-38
BF16_MIN_NORMAL_BITS = 0x0080
INT_MIN = -(2 ** 31)
NEG_BIAS = -1e30
LOG2E = math.log2(math.e)

VMEM_LIMIT = 56 * 1024 * 1024


def _params(sem, vmem=VMEM_LIMIT):
    return pltpu.CompilerParams(dimension_semantics=sem, vmem_limit_bytes=vmem)


def _resident(shape):
    nd = len(shape)
    return pl.BlockSpec(shape, lambda *_: (0,) * nd, pipeline_mode=pl.Buffered(1))


def _rms(x, g):
    ms = jnp.mean(x * x, axis=-1, keepdims=True)
    return x * lax.rsqrt(ms + NORM_EPS) * g


_C_Q = 0
_C_K = _C_Q + Q_DIM
_C_QI = _C_K + KV_DIM
_C_KI = _C_QI + IDXQ_DIM
_C_V = _C_KI + IDX_DIM
_C_WI = _C_V + KV_DIM
_C_END = _C_WI + LANES


def _dsa_proj_kernel(h_ref, g_ref, w_ref, cos_ref, sa_ref, sb_ref,
                     qt_ref, k_ref, vt_ref, qit_ref, ki_ref, wit_ref):
    u = _rms(h_ref[...], g_ref[...]).astype(BF16)
    cos = cos_ref[...]
    sa = sa_ref[...]
    sb = sb_ref[...]
    rd = HEAD_DIM // ROPE_FRAC
    half = rd // 2

    def rope(y):
        return (y * cos + pltpu.roll(y, LANES - half, 1) * sa + pltpu.roll(y, half, 1) * sb)

    def proj(c0, n):
        return jnp.dot(u, w_ref[:, c0:c0 + n], preferred_element_type=F32)

    qscale = (HEAD_DIM ** -0.5) * LOG2E
    for c in range(N_HEADS // 4):
        y = proj(_C_Q + 512 * c, 512)
        for j in range(4):
            qt_ref[4 * c + j] = (rope(y[:, 128 * j:128 * (j + 1)]) * qscale).T.astype(BF16)
    y = proj(_C_K, KV_DIM)
    for j in range(N_KV_HEADS):
        k_ref[j] = rope(y[:, 128 * j:128 * (j + 1)]).astype(BF16)
    for c in range(N_IDX_HEADS // 4):
        y = proj(_C_QI + 512 * c, 512)
        for j in range(4):
            qit_ref[4 * c + j] = (rope(y[:, 128 * j:128 * (j + 1)]) * (IDX_DIM ** -0.5)).T.astype(BF16)
    ki_ref[...] = rope(proj(_C_KI, IDX_DIM)).astype(BF16)
    y = proj(_C_V, KV_DIM)
    for j in range(N_KV_HEADS):
        vt_ref[j, 0:HEAD_DIM, :] = y[:, 128 * j:128 * (j + 1)].T.astype(BF16)
        vt_ref[j, HEAD_DIM:V_ROWS, :] = jnp.ones((V_ROWS - HEAD_DIM, vt_ref.shape[2]), BF16)
    wit_ref[...] = (proj(_C_WI, LANES) * (N_IDX_HEADS ** -0.5)).T[:N_IDX_HEADS]


def _dsa_proj(h, g, w_all, cos, sa, sb, seq, tm):
    t, d = h.shape
    nseq = seq // tm
    row = lambda i: (i, 0)
    tab = pl.BlockSpec((tm, LANES), lambda i: (i % nseq, 0))
    return pl.pallas_call(
        _dsa_proj_kernel,
        grid=(t // tm,),
        in_specs=[pl.BlockSpec((tm, d), row), _resident((1, d)), _resident(w_all.shape), tab, tab, tab],
        out_specs=[
            pl.BlockSpec((N_HEADS, HEAD_DIM, tm), lambda i: (0, 0, i)),
            pl.BlockSpec((N_KV_HEADS, tm, HEAD_DIM), lambda i: (0, i, 0)),
            pl.BlockSpec((N_KV_HEADS, V_ROWS, tm), lambda i: (0, 0, i)),
            pl.BlockSpec((N_IDX_HEADS, IDX_DIM, tm), lambda i: (0, 0, i)),
            pl.BlockSpec((tm, IDX_DIM), row),
            pl.BlockSpec((N_IDX_HEADS, tm), lambda i: (0, i)),
        ],
        out_shape=[
            jax.ShapeDtypeStruct((N_HEADS, HEAD_DIM, t), BF16),
            jax.ShapeDtypeStruct((N_KV_HEADS, t, HEAD_DIM), BF16),
            jax.ShapeDtypeStruct((N_KV_HEADS, V_ROWS, t), BF16),
            jax.ShapeDtypeStruct((N_IDX_HEADS, IDX_DIM, t), BF16),
            jax.ShapeDtypeStruct((t, IDX_DIM), BF16),
            jax.ShapeDtypeStruct((N_IDX_HEADS, t), F32),
        ],
        compiler_params=_params(("arbitrary",)),
        name="dsa_proj",
    )(h, g, w_all, cos, sa, sb)


_KT = 256


def _fold(y):
    n = y.shape[0]
    parts = [y[j] for j in range(min(4, n))]
    for j in range(4, n):
        parts[j % 4] = parts[j % 4] + y[j]
    while len(parts) > 1:
        parts = [parts[j] + parts[j + 1] if j + 1 < len(parts) else parts[j] for j in range(0, len(parts), 2)]
    return parts[0]


def _topk_mask_kernel(qit_ref, ki_ref, wit_ref, bias_ref, sc_ref, hi_ref, *, tq, tk, topk, seq):
    i = pl.program_id(1)
    q0 = i * tq
    n_adm = q0 + tq
    nkt = (n_adm + tk - 1) // tk
    nct = n_adm // _KT
    kf = float(topk)
    lane = lax.broadcasted_iota(I32, (1, tq), 1)
    adm_end = q0 + (lane // CHUNK + 1) * CHUNK

    def score_tile(kt, masked):
        k0 = pl.multiple_of(kt * tk, tk)
        kb = ki_ref[pl.ds(k0, tk), :]
        acc = None
        for h in range(N_IDX_HEADS):
            r = jnp.dot(kb, qit_ref[h], preferred_element_type=F32)
            term = wit_ref[h:h + 1, :] * jnp.maximum(r, 0.0)
            acc = term if acc is None else acc + term
        acc = jnp.where(jnp.abs(acc) < F32_MIN_NORMAL, 0.0, acc)
        bits = pltpu.bitcast(acc, I32)
        key = bits ^ ((bits >> 31) & 0x7FFFFFFF)
        top = pltpu.bitcast(bits & jnp.int32(-65536), F32)
        if masked:
            adm = (k0 + lax.broadcasted_iota(I32, (tk, tq), 0)) < adm_end
            key = jnp.where(adm, key, INT_MIN)
            top = jnp.where(adm, top, -jnp.inf)
        sc_ref[pl.ds(k0, tk), :] = key
        hi_ref[pl.ds(k0, tk), :] = top.astype(BF16)

    def interior(kt, carry):
        score_tile(kt, False)
        return carry

    def boundary(kt, carry):
        score_tile(kt, True)
        return carry

    n_int = q0 // tk
    lax.fori_loop(0, n_int, interior, 0)
    lax.fori_loop(n_int, nkt, boundary, 0)

    row_kt = lax.broadcasted_iota(I32, (_KT, tq), 0)

    def count(pred):
        def body(c, acc):
            r0 = pl.multiple_of(c * _KT, _KT)
            key = sc_ref[pl.ds(r0, _KT), :]
            ones = jnp.where(pred(key, r0 + row_kt), 1.0, 0.0)
            return acc + _fold(ones.reshape(_KT // SUBLANES, SUBLANES, tq))
        acc = lax.fori_loop(0, nct, body, jnp.zeros((SUBLANES, tq), F32))
        return jnp.sum(acc, axis=0, keepdims=True)

    def count_hi(cand):
        assert tk // PACKED_SUBLANES < 256
        def body(c, acc):
            r0 = pl.multiple_of(c * tk, tk)
            ones = jnp.where(hi_ref[pl.ds(r0, tk), :] >= cand, jnp.ones((), BF16), jnp.zeros((), BF16))
            part = _fold(ones.reshape(tk // PACKED_SUBLANES, PACKED_SUBLANES, tq))
            return acc + part.astype(F32)
        acc = lax.fori_loop(0, nkt, body, jnp.zeros((PACKED_SUBLANES, tq), F32))
        return jnp.sum(acc, axis=0, keepdims=True)

    def emit(keep):
        def body(c, carry):
            r0 = pl.multiple_of(c * _KT, _KT)
            key = sc_ref[pl.ds(r0, _KT), :]
            bias_ref[0, 0, pl.ds(r0, _KT), :] = jnp.where(keep(key, r0 + row_kt), 0.0, NEG_BIAS).astype(BF16)
            return carry
        lax.fori_loop(0, nct, body, 0)

    def hi_pass(p, state):
        u, cnt_u = state
        cand_u = u | (jnp.int32(1) << (15 - p))
        c16 = cand_u - 32768
        c16 = jnp.where(jnp.logical_and(c16 > 0, c16 < BF16_MIN_NORMAL_BITS), BF16_MIN_NORMAL_BITS, c16)
        raw = jnp.where(c16 >= 0, c16, c16 ^ 0x7FFF) << 16
        cnt = count_hi(pltpu.bitcast(raw, F32).astype(BF16))
        ok = cnt >= kf
        return jnp.where(ok, cand_u, u), jnp.where(ok, cnt, cnt_u)

    many = jnp.full((1, tq), float(seq + 1), F32)
    u, cnt_u = lax.fori_loop(0, 16, hi_pass, (jnp.zeros((1, tq), I32), many))

    few = adm_end <= topk

    def unsettled(cnt):
        return jnp.max(jnp.where(jnp.logical_or(few, cnt == kf), 0.0, 1.0)) > 0.0

    def lo_cond(state):
        p, _, _, go = state
        return jnp.logical_and(p < 16, go)

    def lo_pass(state):
        p, tau, cnt_t, _ = state
        cand = tau | (jnp.int32(1) << (15 - p))
        cnt = count(lambda key, kidx: key >= cand)
        ok = cnt >= kf
        cnt_t = jnp.where(ok, cnt, cnt_t)
        return p + 1, jnp.where(ok, cand, tau), cnt_t, unsettled(cnt_t)

    _, tau, cnt_ge, _ = lax.while_loop(lo_cond, lo_pass, (jnp.int32(0), (u - 32768) << 16, cnt_u, unsettled(cnt_u)))
    tau = jnp.maximum(tau, INT_MIN + 1)
    has_tie = jnp.max(jnp.where(few, 0.0, cnt_ge)) > kf

    def fill(c, carry):
        r0 = pl.multiple_of(c * _KT, _KT)
        bias_ref[0, 0, pl.ds(r0, _KT), :] = jnp.full((_KT, tq), NEG_BIAS, BF16)
        return carry

    lax.fori_loop(nct, seq // _KT, fill, 0)

    @pl.when(jnp.logical_not(has_tie))
    def _():
        emit(lambda key, kidx: key >= tau)

    @pl.when(has_tie)
    def _():
        tied = jnp.logical_and(jnp.logical_not(few), cnt_ge > kf)
        need = jnp.where(tied, kf - count(lambda key, kidx: key > tau), 0.0)
        max_need = jnp.max(need).astype(I32)
        nbits = max(1, (seq - 1).bit_length())

        def next_tied(after):
            def body(c, acc):
                r0 = pl.multiple_of(c * _KT, _KT)
                key = sc_ref[pl.ds(r0, _KT), :]
                kidx = r0 + row_kt
                v = jnp.where(jnp.logical_and(key == tau, kidx > after), kidx, seq)
                v = v.reshape(_KT // SUBLANES, SUBLANES, tq)
                for j in range(_KT // SUBLANES):
                    acc = jnp.minimum(acc, v[j])
                return acc
            acc = lax.fori_loop(0, nct, body, jnp.full((SUBLANES, tq), seq, I32))
            return jnp.min(acc.astype(F32), axis=0, keepdims=True).astype(I32)

        def by_scan():
            def step(r, j0):
                return jnp.where(r.astype(F32) < need, next_tied(j0), j0)
            return lax.fori_loop(0, max_need, step, jnp.full((1, tq), -1, I32))

        def by_bisect():
            def idx_pass(p, j0):
                cand = j0 + (jnp.int32(1) << (nbits - 1 - p))
                c_lt = count(lambda key, kidx: jnp.logical_and(key == tau, kidx < cand))
                return jnp.where(c_lt < need, cand, j0)
            return lax.fori_loop(0, nbits, idx_pass, jnp.zeros((1, tq), I32))

        j0 = lax.cond(max_need <= nbits, by_scan, by_bisect)
        j0 = jnp.where(tied, j0, seq)
        emit(lambda key, kidx: jnp.logical_or(key > tau, jnp.logical_and(key == tau, kidx <= j0)))


def _topk_mask(qit, ki, wit, batch, seq, topk, tq, tk, tq_attn):
    assert tq % _KT == 0 and tq_attn % tq == 0 and seq % tk == 0
    nq = seq // tq
    per = tq_attn // tq
    kern = functools.partial(_topk_mask_kernel, tq=tq, tk=tk, topk=topk, seq=seq)
    return pl.pallas_call(
        kern,
        grid=(batch, nq),
        in_specs=[
            pl.BlockSpec((N_IDX_HEADS, IDX_DIM, tq), lambda b, i: (0, 0, b * nq + i)),
            pl.BlockSpec((seq, IDX_DIM), lambda b, i: (b, 0)),
            pl.BlockSpec((N_IDX_HEADS, tq), lambda b, i: (0, b * nq + i)),
        ],
        out_specs=pl.BlockSpec((1, 1, seq, tq), lambda b, i: (b, i // per, 0, i % per)),
        out_shape=jax.ShapeDtypeStruct((batch, seq // tq_attn, seq, tq_attn), BF16),
        scratch_shapes=[pltpu.VMEM((seq, tq), I32), pltpu.VMEM((seq, tq), BF16)],
        compiler_params=_params(("arbitrary", "arbitrary")),
        name="topk_mask",
    )(qit, ki, wit)


_HP = 2


def _attn_kernel(qi_ref, kj_ref, qt_ref, k_ref, vt_ref, bias_ref, o_ref,
                 m_ref, l_ref, acc_ref, sa_ref, sb_ref, ma_ref, mb_ref, bf_ref, *, tq, tk):
    n = pl.program_id(1)
    i = qi_ref[n]
    kt = kj_ref[n]
    last = (i * tq + tq + tk - 1) // tk - 1
    nstage = N_HEADS // _HP

    @pl.when(kt == 0)
    def _():
        m_ref[...] = jnp.full(m_ref.shape, NEG_BIAS, F32)
        l_ref[...] = jnp.zeros(l_ref.shape, F32)
        acc_ref[...] = jnp.zeros(acc_ref.shape, F32)

    bf_ref[...] = bias_ref[0, 0].astype(F32)

    def scores(j, buf, mx):
        for r in range(_HP):
            h = j * _HP + r
            sb = jnp.dot(k_ref[h // GROUP], qt_ref[h], preferred_element_type=F32) + bf_ref[...]
            buf[r] = sb
            mx[r] = jnp.max(sb, axis=0, keepdims=True)

    def softmax_pv(j, buf, mx):
        for r in range(_HP):
            h = j * _HP + r
            m_prev = m_ref[h]
            m_new = jnp.maximum(m_prev, mx[r])
            alpha = jnp.exp2(m_prev - m_new)
            p = jnp.exp2(buf[r] - m_new).astype(BF16)
            pv = jnp.dot(vt_ref[h // GROUP], p, preferred_element_type=F32)
            l_ref[h] = alpha * l_ref[h] + pv[HEAD_DIM:HEAD_DIM + 1]
            acc_ref[h] = alpha * acc_ref[h] + pv[:HEAD_DIM]
            m_ref[h] = m_new

    scores(0, sa_ref, ma_ref)

    def two_stages(jj, carry):
        j = 2 * jj
        scores(j + 1, sb_ref, mb_ref)
        softmax_pv(j, sa_ref, ma_ref)
        scores(j + 2, sa_ref, ma_ref)
        softmax_pv(j + 1, sb_ref, mb_ref)
        return carry

    lax.fori_loop(0, nstage // 2 - 1, two_stages, 0)
    scores(nstage - 1, sb_ref, mb_ref)
    softmax_pv(nstage - 2, sa_ref, ma_ref)
    softmax_pv(nstage - 1, sb_ref, mb_ref)

    @pl.when(kt == last)
    def _():
        for h in range(N_HEADS):
            o_ref[:, HEAD_DIM * h:HEAD_DIM * (h + 1)] = (acc_ref[h] / l_ref[h]).T.astype(BF16)


def _attention(qt, k, vt, bias, batch, seq, tq, tk):
    nq = seq // tq
    nk = seq // tk
    t = batch * seq
    pairs = [(i, j) for i in range(nq) for j in range((i * tq + tq + tk - 1) // tk)]
    qi = jnp.asarray([p[0] for p in pairs], I32)
    kj = jnp.asarray([p[1] for p in pairs], I32)
    kern = functools.partial(_attn_kernel, tq=tq, tk=tk)
    grid_spec = pltpu.PrefetchScalarGridSpec(
        num_scalar_prefetch=2,
        grid=(batch, len(pairs)),
        in_specs=[
            pl.BlockSpec((N_HEADS, HEAD_DIM, tq), lambda b, n, qi, kj: (0, 0, b * nq + qi[n])),
            pl.BlockSpec((N_KV_HEADS, tk, HEAD_DIM), lambda b, n, qi, kj: (0, b * nk + kj[n], 0)),
            pl.BlockSpec((N_KV_HEADS, V_ROWS, tk), lambda b, n, qi, kj: (0, 0, b * nk + kj[n])),
            pl.BlockSpec((1, 1, tk, tq), lambda b, n, qi, kj: (b, qi[n], kj[n], 0)),
        ],
        out_specs=pl.BlockSpec((tq, Q_DIM), lambda b, n, qi, kj: (b * nq + qi[n], 0)),
        scratch_shapes=[
            pltpu.VMEM((N_HEADS, 1, tq), F32),
            pltpu.VMEM((N_HEADS, 1, tq), F32),
            pltpu.VMEM((N_HEADS, HEAD_DIM, tq), F32),
            pltpu.VMEM((_HP, tk, tq), F32),
            pltpu.VMEM((_HP, tk, tq), F32),
            pltpu.VMEM((_HP, 1, tq), F32),
            pltpu.VMEM((_HP, 1, tq), F32),
            pltpu.VMEM((tk, tq), F32),
        ],
    )
    return pl.pallas_call(
        kern,
        grid_spec=grid_spec,
        out_shape=jax.ShapeDtypeStruct((t, Q_DIM), BF16),
        compiler_params=_params(("arbitrary", "arbitrary")),
        name="masked_attention",
    )(qi, kj, qt, k, vt, bias)


def _proj_residual_kernel(h_ref, x_ref, w_ref, *rest, has_bias, tn):
    if has_bias:
        b_ref, o_ref = rest
    else:
        (o_ref,) = rest
    x = x_ref[...]
    for c in range(w_ref.shape[1] // tn):
        cols = slice(c * tn, (c + 1) * tn)
        y = h_ref[:, cols] + jnp.dot(x, w_ref[:, cols], preferred_element_type=F32)
        if has_bias:
            y = y + b_ref[:, cols]
        o_ref[:, cols] = y


def _proj_residual(h, x, w, b, tm, name):
    t, d = h.shape
    kdim = x.shape[1]
    row = lambda i: (i, 0)
    in_specs = [pl.BlockSpec((tm, d), row), pl.BlockSpec((tm, kdim), row), _resident(w.shape)]
    args = [h, x, w]
    if b is not None:
        in_specs.append(_resident((1, d)))
        args.append(b)
    kern = functools.partial(_proj_residual_kernel, has_bias=b is not None, tn=512)
    return pl.pallas_call(
        kern,
        grid=(t // tm,),
        in_specs=in_specs,
        out_specs=pl.BlockSpec((tm, d), row),
        out_shape=jax.ShapeDtypeStruct((t, d), F32),
        compiler_params=_params(("arbitrary",)),
        name=name,
    )(*args)


def _mlp_kernel(h_ref, g_ref, w1_ref, w2_ref, o_ref, u_ref):
    @pl.when(pl.program_id(1) == 0)
    def _():
        h = h_ref[...]
        u_ref[...] = _rms(h, g_ref[...]).astype(BF16)
        o_ref[...] = h

    a = jnp.dot(u_ref[...], w1_ref[...], preferred_element_type=F32)
    a = jnp.square(jnp.maximum(a, 0.0)).astype(BF16)
    o_ref[...] += jnp.dot(a, w2_ref[...], preferred_element_type=F32)


def _mlp(h, g, w1, w2, tm, tf):
    t, d = h.shape
    f = w1.shape[1]
    return pl.pallas_call(
        _mlp_kernel,
        grid=(t // tm, f // tf),
        in_specs=[
            pl.BlockSpec((tm, d), lambda i, j: (i, 0)),
            _resident((1, d)),
            pl.BlockSpec((d, tf), lambda i, j: (0, j)),
            pl.BlockSpec((tf, d), lambda i, j: (j, 0)),
        ],
        out_specs=pl.BlockSpec((tm, d), lambda i, j: (i, 0)),
        out_shape=jax.ShapeDtypeStruct((t, d), F32),
        scratch_shapes=[pltpu.VMEM((tm, d), BF16)],
        compiler_params=_params(("arbitrary", "arbitrary")),
        name="sqrelu_mlp",
    )(h, g, w1, w2)


def _pe_kernel(h_ref, g_ref, wg_ref, p_ref, wp_ref, *rest, final, tn):
    if final:
        gf_ref, o_ref = rest
    else:
        (o_ref,) = rest
    h = h_ref[...]
    u = _rms(h, g_ref[...]).astype(BF16)
    pb = p_ref[...].astype(BF16)
    for c in range(wg_ref.shape[1] // tn):
        cols = slice(c * tn, (c + 1) * tn)
        gate = jax.nn.sigmoid(jnp.dot(u, wg_ref[:, cols], preferred_element_type=F32))
        emb = jnp.dot(pb, wp_ref[:, cols], preferred_element_type=F32)
        o_ref[:, cols] = h_ref[:, cols] + emb * gate
    if final:
        o_ref[...] = _rms(o_ref[...], gf_ref[...])


def _pe(h, g, wg, p, layer, wp, gf, tm):
    t, d = h.shape
    row = lambda i: (i, 0)
    in_specs = [pl.BlockSpec((tm, d), row), _resident((1, d)), _resident(wg.shape),
                pl.BlockSpec((None, tm, p.shape[2]), lambda r: (layer, r, 0)), _resident(wp.shape)]
    args = [h, g, wg, p, wp]
    if gf is not None:
        in_specs.append(_resident((1, d)))
        args.append(gf)
    kern = functools.partial(_pe_kernel, final=gf is not None, tn=512)
    return pl.pallas_call(
        kern,
        grid=(t // tm,),
        in_specs=in_specs,
        out_specs=pl.BlockSpec((tm, d), row),
        out_shape=jax.ShapeDtypeStruct((t, d), F32),
        compiler_params=_params(("arbitrary",)),
        name="gated_embedding",
    )(*args)


def _conv_in_kernel(h_ref, g_ref, w_ref, b_ref, y_ref, *, tn):
    d = y_ref.shape[1]
    u = _rms(h_ref[...], g_ref[...]).astype(BF16)
    for c in range(d // tn):
        a = jnp.dot(u, w_ref[:, c * tn:(c + 1) * tn], preferred_element_type=F32) + b_ref[:, c * tn:(c + 1) * tn]
        gt = (jnp.dot(u, w_ref[:, d + c * tn:d + (c + 1) * tn], preferred_element_type=F32)
              + b_ref[:, d + c * tn:d + (c + 1) * tn])
        y_ref[:, c * tn:(c + 1) * tn] = a * jax.nn.sigmoid(gt)


def _conv_in(h, g, w, b, tm):
    t, d = h.shape
    row = lambda i: (i, 0)
    return pl.pallas_call(
        functools.partial(_conv_in_kernel, tn=512),
        grid=(t // tm,),
        in_specs=[pl.BlockSpec((tm, d), row), _resident((1, d)), _resident(w.shape), _resident(b.shape)],
        out_specs=pl.BlockSpec((tm, d), row),
        out_shape=jax.ShapeDtypeStruct((t, d), F32),
        compiler_params=_params(("arbitrary",)),
        name="conv_in_glu",
    )(h, g, w, b)


_HALO = 32
_CONV_RS = 64


def _dwconv_kernel(y_ref, halo_ref, w_ref, bdw_ref, lg_ref, lb_ref, z_ref, ext_ref, cv_ref, *, tm):
    d = y_ref.shape[2]
    nc = d // LANES
    first = pl.program_id(1) == 0
    for c in range(nc):
        cols = slice(c * LANES, (c + 1) * LANES)
        ext_ref[c, 0:_HALO] = jnp.where(first, 0.0, halo_ref[0, :, cols])
        ext_ref[c, _HALO:_HALO + tm] = y_ref[0, :, cols]

    off = _HALO - (CONV_WIDTH - 1)

    def chunk(c, carry):
        for rs in range(tm // _CONV_RS):
            r0 = rs * _CONV_RS
            acc = jnp.broadcast_to(bdw_ref[c], (_CONV_RS, LANES))
            for j in range(CONV_WIDTH):
                acc = acc + w_ref[c, j:j + 1, :] * ext_ref[c, r0 + off + j:r0 + off + j + _CONV_RS, :]
            cv_ref[c, r0:r0 + _CONV_RS] = acc
        return carry

    lax.fori_loop(0, nc, chunk, 0)

    s1 = cv_ref[0]
    for c in range(1, nc):
        s1 = s1 + cv_ref[c]
    mu = jnp.sum(s1, axis=1, keepdims=True) / d
    s2 = jnp.square(cv_ref[0] - mu)
    for c in range(1, nc):
        s2 = s2 + jnp.square(cv_ref[c] - mu)
    rstd = lax.rsqrt(jnp.sum(s2, axis=1, keepdims=True) / d + NORM_EPS)
    for c in range(nc):
        yn = (cv_ref[c] - mu) * rstd * lg_ref[c] + lb_ref[c]
        z_ref[0, :, c * LANES:(c + 1) * LANES] = (yn * jax.nn.sigmoid(yn)).astype(BF16)


def _dwconv_ln_swish(y3, w3, bdw3, lg3, lb3, tm):
    batch, seq, d = y3.shape
    nc = d // LANES
    hb = tm // _HALO
    kern = functools.partial(_dwconv_kernel, tm=tm)
    return pl.pallas_call(
        kern,
        grid=(batch, seq // tm),
        in_specs=[
            pl.BlockSpec((1, tm, d), lambda b, i: (b, i, 0)),
            pl.BlockSpec((1, _HALO, d), lambda b, i: (b, jnp.maximum(i * hb - 1, 0), 0)),
            _resident(w3.shape), _resident(bdw3.shape), _resident(lg3.shape), _resident(lb3.shape),
        ],
        out_specs=pl.BlockSpec((1, tm, d), lambda b, i: (b, i, 0)),
        out_shape=jax.ShapeDtypeStruct((batch, seq, d), BF16),
        scratch_shapes=[
            pltpu.VMEM((nc, _HALO + tm, LANES), F32),
            pltpu.VMEM((nc, tm, LANES), F32),
        ],
        compiler_params=_params(("arbitrary", "arbitrary")),
        name="dwconv_ln_swish",
    )(y3, y3, w3, bdw3, lg3, lb3)


def _rope_tables(seq):
    rd = HEAD_DIM // ROPE_FRAC
    half = rd // 2
    pos = jnp.arange(seq, dtype=F32)
    inv = ROPE_THETA ** (-jnp.arange(half, dtype=F32) * 2.0 / rd)
    ang = pos[:, None] * inv[None, :]
    cos = jnp.cos(ang)
    sin = jnp.sin(ang)
    ones = jnp.ones((seq, HEAD_DIM - rd), F32)
    zeros = jnp.zeros((seq, HEAD_DIM - rd), F32)
    zh = jnp.zeros((seq, half), F32)
    cos_t = jnp.concatenate([cos, cos, ones], axis=1)
    sa_t = jnp.concatenate([-sin, zh, zeros], axis=1)
    sb_t = jnp.concatenate([zh, sin, zeros], axis=1)
    return cos_t, sa_t, sb_t


def _chunked(v, nc):
    v2 = v.reshape(-1, v.shape[-1])
    return v2.reshape(v2.shape[0], nc, LANES).transpose(1, 0, 2)


def kernel(x, p, mix_norm, mlp_norm, mlp_w1, mlp_w2, pe_proj, pe_gate_norm, pe_gate, dsa_w_in, dsa_w_out, conv_w_in, conv_b_in, conv_w_dw, conv_b_dw, conv_ln_g, conv_ln_b, conv_w_out, conv_b_out, final_norm):
    batch, seq, d = x.shape
    depth = p.shape[0]
    t = batch * seq
    topk = min(TOPK_MAX, seq // KEY_FRAC)
    tm = 512
    h = x.reshape(t, d)
    row_vec = lambda v: v.reshape(1, -1)

    for i in range(depth):
        j = i // 2
        if i % 2 == 0:
            w = dsa_w_in[j]
            wq, wk, wv, wqi, wki, wwi = jnp.split(
                w, [Q_DIM, Q_DIM + KV_DIM, Q_DIM + 2 * KV_DIM, Q_DIM + 2 * KV_DIM + IDXQ_DIM,
                    Q_DIM + 2 * KV_DIM + IDXQ_DIM + IDX_DIM], axis=1)
            wwi = jnp.pad(wwi, ((0, 0), (0, LANES - N_IDX_HEADS)))
            w_all = jnp.concatenate([wq, wk, wqi, wki, wv, wwi], axis=1).astype(BF16)
            cos_t, sa_t, sb_t = _rope_tables(seq)
            qt, k, vt, qit, ki, wit = _dsa_proj(h, row_vec(mix_norm[i]), w_all, cos_t, sa_t, sb_t, seq, tm)
            tq_attn = min(512, seq)
            bias = _topk_mask(qit, ki, wit, batch, seq, topk, tq=tq_attn, tk=512, tq_attn=tq_attn)
            o = _attention(qt, k, vt, bias, batch, seq, tq=tq_attn, tk=512)
            h = _proj_residual(h, o, dsa_w_out[j].astype(BF16), None, tm, "dsa_out")
        else:
            nc = d // LANES
            y = _conv_in(h, row_vec(mix_norm[i]), conv_w_in[j].astype(BF16), row_vec(conv_b_in[j]), tm)
            w3 = _chunked(jnp.pad(conv_w_dw[j], ((0, _HALO - CONV_WIDTH), (0, 0))), nc)
            z = _dwconv_ln_swish(y.reshape(batch, seq, d), w3, _chunked(conv_b_dw[j], nc),
                                 _chunked(conv_ln_g[j], nc), _chunked(conv_ln_b[j], nc), tm=256)
            h = _proj_residual(h, z.reshape(t, d), conv_w_out[j].astype(BF16), row_vec(conv_b_out[j]), tm,
                               "conv_out")
        h = _mlp(h, row_vec(mlp_norm[i]), mlp_w1[i].astype(BF16), mlp_w2[i].astype(BF16), tm=1024, tf=512)
        gf = row_vec(final_norm) if i == depth - 1 else None
        h = _pe(h, row_vec(pe_gate_norm[i]), pe_gate[i].astype(BF16), p.reshape(depth, t, -1), i,
                pe_proj[i].astype(BF16), gf, tm)
    return h.reshape(batch, seq, d)
```

```python
import functools
import math

import jax
import jax.numpy as jnp
from jax import lax
from jax.experimental import pallas as pl
from jax.experimental.pallas import tpu as pltpu

F32 = jnp.float32
BF16 = jnp.bfloat16
I32 = jnp.int32

CHUNK = 64
N_HEADS = 16
HEAD_DIM = 128
N_KV_HEADS = 4
GROUP = N_HEADS // N_KV_HEADS
N_IDX_HEADS = 8
IDX_DIM = 128
TOPK_MAX = 256
KEY_FRAC = 4
ROPE_THETA = 500000.0
ROPE_FRAC = 4
CONV_WIDTH = 31
NORM_EPS = 1e-6
Q_DIM = N_HEADS * HEAD_DIM
KV_DIM = N_KV_HEADS * HEAD_DIM
IDXQ_DIM = N_IDX_HEADS * IDX_DIM

LANES = 128
SUBLANES = 8
PACKED_SUBLANES = 16
V_ROWS = HEAD_DIM + PACKED_SUBLANES
F32_MIN_NORMAL = 1.1754943508222875e-38
BF16_MIN_NORMAL_BITS = 0x0080
INT_MIN = -(2 ** 31)
NEG_BIAS = -1e30
LOG2E = math.log2(math.e)

VMEM_LIMIT = 56 * 1024 * 1024


def _params(sem, vmem=VMEM_LIMIT):
    return pltpu.CompilerParams(dimension_semantics=sem, vmem_limit_bytes=vmem)


def _resident(shape):
    nd = len(shape)
    return pl.BlockSpec(shape, lambda *_: (0,) * nd, pipeline_mode=pl.Buffered(1))


def _rms(x, g):
    ms = jnp.mean(x * x, axis=-1, keepdims=True)
    return x * lax.rsqrt(ms + NORM_EPS) * g


_C_Q = 0
_C_K = _C_Q + Q_DIM
_C_QI = _C_K + KV_DIM
_C_KI = _C_QI + IDXQ_DIM
_C_V = _C_KI + IDX_DIM
_C_WI = _C_V + KV_DIM
_C_END = _C_WI + LANES


def _dsa_proj_kernel(h_ref, g_ref, w_ref, cos_ref, sa_ref, sb_ref,
                     qt_ref, k_ref, vt_ref, qit_ref, ki_ref, wit_ref):
    u = _rms(h_ref[...], g_ref[...]).astype(BF16)
    cos = cos_ref[...]
    sa = sa_ref[...]
    sb = sb_ref[...]
    rd = HEAD_DIM // ROPE_FRAC
    half = rd // 2

    def rope(y):
        return (y * cos + pltpu.roll(y, LANES - half, 1) * sa + pltpu.roll(y, half, 1) * sb)

    def proj(c0, n):
        return jnp.dot(u, w_ref[:, c0:c0 + n], preferred_element_type=F32)

    qscale = (HEAD_DIM ** -0.5) * LOG2E
    for c in range(N_HEADS // 4):
        y = proj(_C_Q + 512 * c, 512)
        for j in range(4):
            qt_ref[4 * c + j] = (rope(y[:, 128 * j:128 * (j + 1)]) * qscale).T.astype(BF16)
    y = proj(_C_K, KV_DIM)
    for j in range(N_KV_HEADS):
        k_ref[j] = rope(y[:, 128 * j:128 * (j + 1)]).astype(BF16)
    for c in range(N_IDX_HEADS // 4):
        y = proj(_C_QI + 512 * c, 512)
        for j in range(4):
            qit_ref[4 * c + j] = (rope(y[:, 128 * j:128 * (j + 1)]) * (IDX_DIM ** -0.5)).T.astype(BF16)
    ki_ref[...] = rope(proj(_C_KI, IDX_DIM)).astype(BF16)
    y = proj(_C_V, KV_DIM)
    for j in range(N_KV_HEADS):
        vt_ref[j, 0:HEAD_DIM, :] = y[:, 128 * j:128 * (j + 1)].T.astype(BF16)
        vt_ref[j, HEAD_DIM:V_ROWS, :] = jnp.ones((V_ROWS - HEAD_DIM, vt_ref.shape[2]), BF16)
    wit_ref[...] = (proj(_C_WI, LANES) * (N_IDX_HEADS ** -0.5)).T[:N_IDX_HEADS]


def _dsa_proj(h, g, w_all, cos, sa, sb, seq, tm):
    t, d = h.shape
    nseq = seq // tm
    row = lambda i: (i, 0)
    tab = pl.BlockSpec((tm, LANES), lambda i: (i % nseq, 0))
    return pl.pallas_call(
        _dsa_proj_kernel,
        grid=(t // tm,),
        in_specs=[pl.BlockSpec((tm, d), row), _resident((1, d)), _resident(w_all.shape), tab, tab, tab],
        out_specs=[
            pl.BlockSpec((N_HEADS, HEAD_DIM, tm), lambda i: (0, 0, i)),
            pl.BlockSpec((N_KV_HEADS, tm, HEAD_DIM), lambda i: (0, i, 0)),
            pl.BlockSpec((N_KV_HEADS, V_ROWS, tm), lambda i: (0, 0, i)),
            pl.BlockSpec((N_IDX_HEADS, IDX_DIM, tm), lambda i: (0, 0, i)),
            pl.BlockSpec((tm, IDX_DIM), row),
            pl.BlockSpec((N_IDX_HEADS, tm), lambda i: (0, i)),
        ],
        out_shape=[
            jax.ShapeDtypeStruct((N_HEADS, HEAD_DIM, t), BF16),
            jax.ShapeDtypeStruct((N_KV_HEADS, t, HEAD_DIM), BF16),
            jax.ShapeDtypeStruct((N_KV_HEADS, V_ROWS, t), BF16),
            jax.ShapeDtypeStruct((N_IDX_HEADS, IDX_DIM, t), BF16),
            jax.ShapeDtypeStruct((t, IDX_DIM), BF16),
            jax.ShapeDtypeStruct((N_IDX_HEADS, t), F32),
        ],
        compiler_params=_params(("arbitrary",)),
        name="dsa_proj",
    )(h, g, w_all, cos, sa, sb)


_KT = 256


def _fold(y):
    n = y.shape[0]
    parts = [y[j] for j in range(min(4, n))]
    for j in range(4, n):
        parts[j % 4] = parts[j % 4] + y[j]
    while len(parts) > 1:
        parts = [parts[j] + parts[j + 1] if j + 1 < len(parts) else parts[j] for j in range(0, len(parts), 2)]
    return parts[0]


def _topk_mask_kernel(qit_ref, ki_ref, wit_ref, bias_ref, sc_ref, hi_ref, *, tq, tk, topk, seq):
    i = pl.program_id(1)
    q0 = i * tq
    n_adm = q0 + tq
    nkt = (n_adm + tk - 1) // tk
    nct = n_adm // _KT
    kf = float(topk)
    lane = lax.broadcasted_iota(I32, (1, tq), 1)
    adm_end = q0 + (lane // CHUNK + 1) * CHUNK

    def score_tile(kt, masked):
        k0 = pl.multiple_of(kt * tk, tk)
        kb = ki_ref[pl.ds(k0, tk), :]
        acc = None
        for h in range(N_IDX_HEADS):
            r = jnp.dot(kb, qit_ref[h], preferred_element_type=F32)
            term = wit_ref[h:h + 1, :] * jnp.maximum(r, 0.0)
            acc = term if acc is None else acc + term
        acc = jnp.where(jnp.abs(acc) < F32_MIN_NORMAL, 0.0, acc)
        bits = pltpu.bitcast(acc, I32)
        key = bits ^ ((bits >> 31) & 0x7FFFFFFF)
        top = pltpu.bitcast(bits & jnp.int32(-65536), F32)
        if masked:
            adm = (k0 + lax.broadcasted_iota(I32, (tk, tq), 0)) < adm_end
            key = jnp.where(adm, key, INT_MIN)
            top = jnp.where(adm, top, -jnp.inf)
        sc_ref[pl.ds(k0, tk), :] = key
        hi_ref[pl.ds(k0, tk), :] = top.astype(BF16)

    def interior(kt, carry):
        score_tile(kt, False)
        return carry

    def boundary(kt, carry):
        score_tile(kt, True)
        return carry

    n_int = q0 // tk
    lax.fori_loop(0, n_int, interior, 0)
    lax.fori_loop(n_int, nkt, boundary, 0)

    row_kt = lax.broadcasted_iota(I32, (_KT, tq), 0)

    def count(pred):
        def body(c, acc):
            r0 = pl.multiple_of(c * _KT, _KT)
            key = sc_ref[pl.ds(r0, _KT), :]
            ones = jnp.where(pred(key, r0 + row_kt), 1.0, 0.0)
            return acc + _fold(ones.reshape(_KT // SUBLANES, SUBLANES, tq))
        acc = lax.fori_loop(0, nct, body, jnp.zeros((SUBLANES, tq), F32))
        return jnp.sum(acc, axis=0, keepdims=True)

    def count_hi(cand):
        assert tk // PACKED_SUBLANES < 256
        def body(c, acc):
            r0 = pl.multiple_of(c * tk, tk)
            ones = jnp.where(hi_ref[pl.ds(r0, tk), :] >= cand, jnp.ones((), BF16), jnp.zeros((), BF16))
            part = _fold(ones.reshape(tk // PACKED_SUBLANES, PACKED_SUBLANES, tq))
            return acc + part.astype(F32)
        acc = lax.fori_loop(0, nkt, body, jnp.zeros((PACKED_SUBLANES, tq), F32))
        return jnp.sum(acc, axis=0, keepdims=True)

    def emit(keep):
        def body(c, carry):
            r0 = pl.multiple_of(c * _KT, _KT)
            key = sc_ref[pl.ds(r0, _KT), :]
            bias_ref[0, 0, pl.ds(r0, _KT), :] = jnp.where(keep(key, r0 + row_kt), 0.0, NEG_BIAS).astype(BF16)
            return carry
        lax.fori_loop(0, nct, body, 0)

    def hi_pass(p, state):
        u, cnt_u = state
        cand_u = u | (jnp.int32(1) << (15 - p))
        c16 = cand_u - 32768
        c16 = jnp.where(jnp.logical_and(c16 > 0, c16 < BF16_MIN_NORMAL_BITS), BF16_MIN_NORMAL_BITS, c16)
        raw = jnp.where(c16 >= 0, c16, c16 ^ 0x7FFF) << 16
        cnt = count_hi(pltpu.bitcast(raw, F32).astype(BF16))
        ok = cnt >= kf
        return jnp.where(ok, cand_u, u), jnp.where(ok, cnt, cnt_u)

    many = jnp.full((1, tq), float(seq + 1), F32)
    u, cnt_u = lax.fori_loop(0, 16, hi_pass, (jnp.zeros((1, tq), I32), many))

    few = adm_end <= topk

    def unsettled(cnt):
        return jnp.max(jnp.where(jnp.logical_or(few, cnt == kf), 0.0, 1.0)) > 0.0

    def lo_cond(state):
        p, _, _, go = state
        return jnp.logical_and(p < 16, go)

    def lo_pass(state):
        p, tau, cnt_t, _ = state
        cand = tau | (jnp.int32(1) << (15 - p))
        cnt = count(lambda key, kidx: key >= cand)
        ok = cnt >= kf
        cnt_t = jnp.where(ok, cnt, cnt_t)
        return p + 1, jnp.where(ok, cand, tau), cnt_t, unsettled(cnt_t)

    _, tau, cnt_ge, _ = lax.while_loop(lo_cond, lo_pass, (jnp.int32(0), (u - 32768) << 16, cnt_u, unsettled(cnt_u)))
    tau = jnp.maximum(tau, INT_MIN + 1)
    has_tie = jnp.max(jnp.where(few, 0.0, cnt_ge)) > kf

    def fill(c, carry):
        r0 = pl.multiple_of(c * _KT, _KT)
        bias_ref[0, 0, pl.ds(r0, _KT), :] = jnp.full((_KT, tq), NEG_BIAS, BF16)
        return carry

    lax.fori_loop(nct, seq // _KT, fill, 0)

    @pl.when(jnp.logical_not(has_tie))
    def _():
        emit(lambda key, kidx: key >= tau)

    @pl.when(has_tie)
    def _():
        tied = jnp.logical_and(jnp.logical_not(few), cnt_ge > kf)
        need = jnp.where(tied, kf - count(lambda key, kidx: key > tau), 0.0)
        max_need = jnp.max(need).astype(I32)
        nbits = max(1, (seq - 1).bit_length())

        def next_tied(after):
            def body(c, acc):
                r0 = pl.multiple_of(c * _KT, _KT)
                key = sc_ref[pl.ds(r0, _KT), :]
                kidx = r0 + row_kt
                v = jnp.where(jnp.logical_and(key == tau, kidx > after), kidx, seq)
                v = v.reshape(_KT // SUBLANES, SUBLANES, tq)
                for j in range(_KT // SUBLANES):
                    acc = jnp.minimum(acc, v[j])
                return acc
            acc = lax.fori_loop(0, nct, body, jnp.full((SUBLANES, tq), seq, I32))
            return jnp.min(acc.astype(F32), axis=0, keepdims=True).astype(I32)

        def by_scan():
            def step(r, j0):
                return jnp.where(r.astype(F32) < need, next_tied(j0), j0)
            return lax.fori_loop(0, max_need, step, jnp.full((1, tq), -1, I32))

        def by_bisect():
            def idx_pass(p, j0):
                cand = j0 + (jnp.int32(1) << (nbits - 1 - p))
                c_lt = count(lambda key, kidx: jnp.logical_and(key == tau, kidx < cand))
                return jnp.where(c_lt < need, cand, j0)
            return lax.fori_loop(0, nbits, idx_pass, jnp.zeros((1, tq), I32))

        j0 = lax.cond(max_need <= nbits, by_scan, by_bisect)
        j0 = jnp.where(tied, j0, seq)
        emit(lambda key, kidx: jnp.logical_or(key > tau, jnp.logical_and(key == tau, kidx <= j0)))


def _topk_mask(qit, ki, wit, batch, seq, topk, tq, tk, tq_attn):
    assert tq % _KT == 0 and tq_attn % tq == 0 and seq % tk == 0
    nq = seq // tq
    per = tq_attn // tq
    kern = functools.partial(_topk_mask_kernel, tq=tq, tk=tk, topk=topk, seq=seq)
    return pl.pallas_call(
        kern,
        grid=(batch, nq),
        in_specs=[
            pl.BlockSpec((N_IDX_HEADS, IDX_DIM, tq), lambda b, i: (0, 0, b * nq + i)),
            pl.BlockSpec((seq, IDX_DIM), lambda b, i: (b, 0)),
            pl.BlockSpec((N_IDX_HEADS, tq), lambda b, i: (0, b * nq + i)),
        ],
        out_specs=pl.BlockSpec((1, 1, seq, tq), lambda b, i: (b, i // per, 0, i % per)),
        out_shape=jax.ShapeDtypeStruct((batch, seq // tq_attn, seq, tq_attn), BF16),
        scratch_shapes=[pltpu.VMEM((seq, tq), I32), pltpu.VMEM((seq, tq), BF16)],
        compiler_params=_params(("arbitrary", "arbitrary")),
        name="topk_mask",
    )(qit, ki, wit)


_HP = 2


def _attn_kernel(qi_ref, kj_ref, qt_ref, k_ref, vt_ref, bias_ref, o_ref,
                 m_ref, l_ref, acc_ref, sa_ref, sb_ref, ma_ref, mb_ref, bf_ref, *, tq, tk):
    n = pl.program_id(1)
    i = qi_ref[n]
    kt = kj_ref[n]
    last = (i * tq + tq + tk - 1) // tk - 1
    nstage = N_HEADS // _HP

    @pl.when(kt == 0)
    def _():
        m_ref[...] = jnp.full(m_ref.shape, NEG_BIAS, F32)
        l_ref[...] = jnp.zeros(l_ref.shape, F32)
        acc_ref[...] = jnp.zeros(acc_ref.shape, F32)

    bf_ref[...] = bias_ref[0, 0].astype(F32)

    def scores(j, buf, mx):
        for r in range(_HP):
            h = j * _HP + r
            sb = jnp.dot(k_ref[h // GROUP], qt_ref[h], preferred_element_type=F32) + bf_ref[...]
            buf[r] = sb
            mx[r] = jnp.max(sb, axis=0, keepdims=True)

    def softmax_pv(j, buf, mx):
        for r in range(_HP):
            h = j * _HP + r
            m_prev = m_ref[h]
            m_new = jnp.maximum(m_prev, mx[r])
            alpha = jnp.exp2(m_prev - m_new)
            p = jnp.exp2(buf[r] - m_new).astype(BF16)
            pv = jnp.dot(vt_ref[h // GROUP], p, preferred_element_type=F32)
            l_ref[h] = alpha * l_ref[h] + pv[HEAD_DIM:HEAD_DIM + 1]
            acc_ref[h] = alpha * acc_ref[h] + pv[:HEAD_DIM]
            m_ref[h] = m_new

    scores(0, sa_ref, ma_ref)

    def two_stages(jj, carry):
        j = 2 * jj
        scores(j + 1, sb_ref, mb_ref)
        softmax_pv(j, sa_ref, ma_ref)
        scores(j + 2, sa_ref, ma_ref)
        softmax_pv(j + 1, sb_ref, mb_ref)
        return carry

    lax.fori_loop(0, nstage // 2 - 1, two_stages, 0)
    scores(nstage - 1, sb_ref, mb_ref)
    softmax_pv(nstage - 2, sa_ref, ma_ref)
    softmax_pv(nstage - 1, sb_ref, mb_ref)

    @pl.when(kt == last)
    def _():
        for h in range(N_HEADS):
            o_ref[:, HEAD_DIM * h:HEAD_DIM * (h + 1)] = (acc_ref[h] / l_ref[h]).T.astype(BF16)


def _attention(qt, k, vt, bias, batch, seq, tq, tk):
    nq = seq // tq
    nk = seq // tk
    t = batch * seq
    pairs = [(i, j) for i in range(nq) for j in range((i * tq + tq + tk - 1) // tk)]
    qi = jnp.asarray([p[0] for p in pairs], I32)
    kj = jnp.asarray([p[1] for p in pairs], I32)
    kern = functools.partial(_attn_kernel, tq=tq, tk=tk)
    grid_spec = pltpu.PrefetchScalarGridSpec(
        num_scalar_prefetch=2,
        grid=(batch, len(pairs)),
        in_specs=[
            pl.BlockSpec((N_HEADS, HEAD_DIM, tq), lambda b, n, qi, kj: (0, 0, b * nq + qi[n])),
            pl.BlockSpec((N_KV_HEADS, tk, HEAD_DIM), lambda b, n, qi, kj: (0, b * nk + kj[n], 0)),
            pl.BlockSpec((N_KV_HEADS, V_ROWS, tk), lambda b, n, qi, kj: (0, 0, b * nk + kj[n])),
            pl.BlockSpec((1, 1, tk, tq), lambda b, n, qi, kj: (b, qi[n], kj[n], 0)),
        ],
        out_specs=pl.BlockSpec((tq, Q_DIM), lambda b, n, qi, kj: (b * nq + qi[n], 0)),
        scratch_shapes=[
            pltpu.VMEM((N_HEADS, 1, tq), F32),
            pltpu.VMEM((N_HEADS, 1, tq), F32),
            pltpu.VMEM((N_HEADS, HEAD_DIM, tq), F32),
            pltpu.VMEM((_HP, tk, tq), F32),
            pltpu.VMEM((_HP, tk, tq), F32),
            pltpu.VMEM((_HP, 1, tq), F32),
            pltpu.VMEM((_HP, 1, tq), F32),
            pltpu.VMEM((tk, tq), F32),
        ],
    )
    return pl.pallas_call(
        kern,
        grid_spec=grid_spec,
        out_shape=jax.ShapeDtypeStruct((t, Q_DIM), BF16),
        compiler_params=_params(("arbitrary", "arbitrary")),
        name="masked_attention",
    )(qi, kj, qt, k, vt, bias)


def _proj_residual_kernel(h_ref, x_ref, w_ref, *rest, has_bias, tn):
    if has_bias:
        b_ref, o_ref = rest
    else:
        (o_ref,) = rest
    x = x_ref[...]
    for c in range(w_ref.shape[1] // tn):
        cols = slice(c * tn, (c + 1) * tn)
        y = h_ref[:, cols] + jnp.dot(x, w_ref[:, cols], preferred_element_type=F32)
        if has_bias:
            y = y + b_ref[:, cols]
        o_ref[:, cols] = y


def _proj_residual(h, x, w, b, tm, name):
    t, d = h.shape
    kdim = x.shape[1]
    row = lambda i: (i, 0)
    in_specs = [pl.BlockSpec((tm, d), row), pl.BlockSpec((tm, kdim), row), _resident(w.shape)]
    args = [h, x, w]
    if b is not None:
        in_specs.append(_resident((1, d)))
        args.append(b)
    kern = functools.partial(_proj_residual_kernel, has_bias=b is not None, tn=512)
    return pl.pallas_call(
        kern,
        grid=(t // tm,),
        in_specs=in_specs,
        out_specs=pl.BlockSpec((tm, d), row),
        out_shape=jax.ShapeDtypeStruct((t, d), F32),
        compiler_params=_params(("arbitrary",)),
        name=name,
    )(*args)


def _mlp_kernel(h_ref, g_ref, w1_ref, w2_ref, o_ref, u_ref):
    @pl.when(pl.program_id(1) == 0)
    def _():
        h = h_ref[...]
        u_ref[...] = _rms(h, g_ref[...]).astype(BF16)
        o_ref[...] = h

    a = jnp.dot(u_ref[...], w1_ref[...], preferred_element_type=F32)
    a = jnp.square(jnp.maximum(a, 0.0)).astype(BF16)
    o_ref[...] += jnp.dot(a, w2_ref[...], preferred_element_type=F32)


def _mlp(h, g, w1, w2, tm, tf):
    t, d = h.shape
    f = w1.shape[1]
    return pl.pallas_call(
        _mlp_kernel,
        grid=(t // tm, f // tf),
        in_specs=[
            pl.BlockSpec((tm, d), lambda i, j: (i, 0)),
            _resident((1, d)),
            pl.BlockSpec((d, tf), lambda i, j: (0, j)),
            pl.BlockSpec((tf, d), lambda i, j: (j, 0)),
        ],
        out_specs=pl.BlockSpec((tm, d), lambda i, j: (i, 0)),
        out_shape=jax.ShapeDtypeStruct((t, d), F32),
        scratch_shapes=[pltpu.VMEM((tm, d), BF16)],
        compiler_params=_params(("arbitrary", "arbitrary")),
        name="sqrelu_mlp",
    )(h, g, w1, w2)


def _pe_kernel(h_ref, g_ref, wg_ref, p_ref, wp_ref, *rest, final, tn):
    if final:
        gf_ref, o_ref = rest
    else:
        (o_ref,) = rest
    h = h_ref[...]
    u = _rms(h, g_ref[...]).astype(BF16)
    pb = p_ref[...].astype(BF16)
    for c in range(wg_ref.shape[1] // tn):
        cols = slice(c * tn, (c + 1) * tn)
        gate = jax.nn.sigmoid(jnp.dot(u, wg_ref[:, cols], preferred_element_type=F32))
        emb = jnp.dot(pb, wp_ref[:, cols], preferred_element_type=F32)
        o_ref[:, cols] = h_ref[:, cols] + emb * gate
    if final:
        o_ref[...] = _rms(o_ref[...], gf_ref[...])


def _pe(h, g, wg, p, layer, wp, gf, tm):
    t, d = h.shape
    row = lambda i: (i, 0)
    in_specs = [pl.BlockSpec((tm, d), row), _resident((1, d)), _resident(wg.shape),
                pl.BlockSpec((None, tm, p.shape[2]), lambda r: (layer, r, 0)), _resident(wp.shape)]
    args = [h, g, wg, p, wp]
    if gf is not None:
        in_specs.append(_resident((1, d)))
        args.append(gf)
    kern = functools.partial(_pe_kernel, final=gf is not None, tn=512)
    return pl.pallas_call(
        kern,
        grid=(t // tm,),
        in_specs=in_specs,
        out_specs=pl.BlockSpec((tm, d), row),
        out_shape=jax.ShapeDtypeStruct((t, d), F32),
        compiler_params=_params(("arbitrary",)),
        name="gated_embedding",
    )(*args)


def _conv_in_kernel(h_ref, g_ref, w_ref, b_ref, y_ref, *, tn):
    d = y_ref.shape[1]
    u = _rms(h_ref[...], g_ref[...]).astype(BF16)
    for c in range(d // tn):
        a = jnp.dot(u, w_ref[:, c * tn:(c + 1) * tn], preferred_element_type=F32) + b_ref[:, c * tn:(c + 1) * tn]
        gt = (jnp.dot(u, w_ref[:, d + c * tn:d + (c + 1) * tn], preferred_element_type=F32)
              + b_ref[:, d + c * tn:d + (c + 1) * tn])
        y_ref[:, c * tn:(c + 1) * tn] = a * jax.nn.sigmoid(gt)


def _conv_in(h, g, w, b, tm):
    t, d = h.shape
    row = lambda i: (i, 0)
    return pl.pallas_call(
        functools.partial(_conv_in_kernel, tn=512),
        grid=(t // tm,),
        in_specs=[pl.BlockSpec((tm, d), row), _resident((1, d)), _resident(w.shape), _resident(b.shape)],
        out_specs=pl.BlockSpec((tm, d), row),
        out_shape=jax.ShapeDtypeStruct((t, d), F32),
        compiler_params=_params(("arbitrary",)),
        name="conv_in_glu",
    )(h, g, w, b)


_HALO = 32
_CONV_RS = 64


def _dwconv_kernel(y_ref, halo_ref, w_ref, bdw_ref, lg_ref, lb_ref, z_ref, ext_ref, cv_ref, *, tm):
    d = y_ref.shape[2]
    nc = d // LANES
    first = pl.program_id(1) == 0
    for c in range(nc):
        cols = slice(c * LANES, (c + 1) * LANES)
        ext_ref[c, 0:_HALO] = jnp.where(first, 0.0, halo_ref[0, :, cols])
        ext_ref[c, _HALO:_HALO + tm] = y_ref[0, :, cols]

    off = _HALO - (CONV_WIDTH - 1)

    def chunk(c, carry):
        for rs in range(tm // _CONV_RS):
            r0 = rs * _CONV_RS
            acc = jnp.broadcast_to(bdw_ref[c], (_CONV_RS, LANES))
            for j in range(CONV_WIDTH):
                acc = acc + w_ref[c, j:j + 1, :] * ext_ref[c, r0 + off + j:r0 + off + j + _CONV_RS, :]
            cv_ref[c, r0:r0 + _CONV_RS] = acc
        return carry

    lax.fori_loop(0, nc, chunk, 0)

    s1 = cv_ref[0]
    for c in range(1, nc):
        s1 = s1 + cv_ref[c]
    mu = jnp.sum(s1, axis=1, keepdims=True) / d
    s2 = jnp.square(cv_ref[0] - mu)
    for c in range(1, nc):
        s2 = s2 + jnp.square(cv_ref[c] - mu)
    rstd = lax.rsqrt(jnp.sum(s2, axis=1, keepdims=True) / d + NORM_EPS)
    for c in range(nc):
        yn = (cv_ref[c] - mu) * rstd * lg_ref[c] + lb_ref[c]
        z_ref[0, :, c * LANES:(c + 1) * LANES] = (yn * jax.nn.sigmoid(yn)).astype(BF16)


def _dwconv_ln_swish(y3, w3, bdw3, lg3, lb3, tm):
    batch, seq, d = y3.shape
    nc = d // LANES
    hb = tm // _HALO
    kern = functools.partial(_dwconv_kernel, tm=tm)
    return pl.pallas_call(
        kern,
        grid=(batch, seq // tm),
        in_specs=[
            pl.BlockSpec((1, tm, d), lambda b, i: (b, i, 0)),
            pl.BlockSpec((1, _HALO, d), lambda b, i: (b, jnp.maximum(i * hb - 1, 0), 0)),
            _resident(w3.shape), _resident(bdw3.shape), _resident(lg3.shape), _resident(lb3.shape),
        ],
        out_specs=pl.BlockSpec((1, tm, d), lambda b, i: (b, i, 0)),
        out_shape=jax.ShapeDtypeStruct((batch, seq, d), BF16),
        scratch_shapes=[
            pltpu.VMEM((nc, _HALO + tm, LANES), F32),
            pltpu.VMEM((nc, tm, LANES), F32),
        ],
        compiler_params=_params(("arbitrary", "arbitrary")),
        name="dwconv_ln_swish",
    )(y3, y3, w3, bdw3, lg3, lb3)


def _rope_tables(seq):
    rd = HEAD_DIM // ROPE_FRAC
    half = rd // 2
    pos = jnp.arange(seq, dtype=F32)
    inv = ROPE_THETA ** (-jnp.arange(half, dtype=F32) * 2.0 / rd)
    ang = pos[:, None] * inv[None, :]
    cos = jnp.cos(ang)
    sin = jnp.sin(ang)
    ones = jnp.ones((seq, HEAD_DIM - rd), F32)
    zeros = jnp.zeros((seq, HEAD_DIM - rd), F32)
    zh = jnp.zeros((seq, half), F32)
    cos_t = jnp.concatenate([cos, cos, ones], axis=1)
    sa_t = jnp.concatenate([-sin, zh, zeros], axis=1)
    sb_t = jnp.concatenate([zh, sin, zeros], axis=1)
    return cos_t, sa_t, sb_t


def _chunked(v, nc):
    v2 = v.reshape(-1, v.shape[-1])
    return v2.reshape(v2.shape[0], nc, LANES).transpose(1, 0, 2)


def kernel(x, p, mix_norm, mlp_norm, mlp_w1, mlp_w2, pe_proj, pe_gate_norm, pe_gate, dsa_w_in, dsa_w_out, conv_w_in, conv_b_in, conv_w_dw, conv_b_dw, conv_ln_g, conv_ln_b, conv_w_out, conv_b_out, final_norm):
    batch, seq, d = x.shape
    depth = p.shape[0]
    t = batch * seq
    topk = min(TOPK_MAX, seq // KEY_FRAC)
    tm = 512
    tm_big = min(1024, t)
    h = x.reshape(t, d)
    row_vec = lambda v: v.reshape(1, -1)

    for i in range(depth):
        j = i // 2
        if i % 2 == 0:
            w = dsa_w_in[j]
            wq, wk, wv, wqi, wki, wwi = jnp.split(
                w, [Q_DIM, Q_DIM + KV_DIM, Q_DIM + 2 * KV_DIM, Q_DIM + 2 * KV_DIM + IDXQ_DIM,
                    Q_DIM + 2 * KV_DIM + IDXQ_DIM + IDX_DIM], axis=1)
            wwi = jnp.pad(wwi, ((0, 0), (0, LANES - N_IDX_HEADS)))
            w_all = jnp.concatenate([wq, wk, wqi, wki, wv, wwi], axis=1).astype(BF16)
            cos_t, sa_t, sb_t = _rope_tables(seq)
            qt, k, vt, qit, ki, wit = _dsa_proj(h, row_vec(mix_norm[i]), w_all, cos_t, sa_t, sb_t, seq, tm)
            tq_attn = min(512, seq)
            bias = _topk_mask(qit, ki, wit, batch, seq, topk, tq=256, tk=512, tq_attn=tq_attn)
            o = _attention(qt, k, vt, bias, batch, seq, tq=tq_attn, tk=512)
            h = _proj_residual(h, o, dsa_w_out[j].astype(BF16), None, tm_big, "dsa_out")
        else:
            nc = d // LANES
            y = _conv_in(h, row_vec(mix_norm[i]), conv_w_in[j].astype(BF16), row_vec(conv_b_in[j]), tm_big)
            w3 = _chunked(jnp.pad(conv_w_dw[j], ((0, _HALO - CONV_WIDTH), (0, 0))), nc)
            z = _dwconv_ln_swish(y.reshape(batch, seq, d), w3, _chunked(conv_b_dw[j], nc),
                                 _chunked(conv_ln_g[j], nc), _chunked(conv_ln_b[j], nc), tm=256)
            h = _proj_residual(h, z.reshape(t, d), conv_w_out[j].astype(BF16), row_vec(conv_b_out[j]), tm_big,
                               "conv_out")
        h = _mlp(h, row_vec(mlp_norm[i]), mlp_w1[i].astype(BF16), mlp_w2[i].astype(BF16), tm=512, tf=2048)
        gf = row_vec(final_norm) if i == depth - 1 else None
        h = _pe(h, row_vec(pe_gate_norm[i]), pe_gate[i].astype(BF16), p.reshape(depth, t, -1), i,
                pe_proj[i].astype(BF16), gf, tm_big)
    return h.reshape(batch, seq, d)
```

```python
import functools
import math

import jax
import jax.numpy as jnp
from jax import lax
from jax.experimental import pallas as pl
from jax.experimental.pallas import tpu as pltpu

F32 = jnp.float32
BF16 = jnp.bfloat16
I32 = jnp.int32

CHUNK = 64
N_HEADS = 16
HEAD_DIM = 128
N_KV_HEADS = 4
GROUP = N_HEADS // N_KV_HEADS
N_IDX_HEADS = 8
IDX_DIM = 128
TOPK_MAX = 256
KEY_FRAC = 4
ROPE_THETA = 500000.0
ROPE_FRAC = 4
CONV_WIDTH = 31
NORM_EPS = 1e-6
Q_DIM = N_HEADS * HEAD_DIM
KV_DIM = N_KV_HEADS * HEAD_DIM
IDXQ_DIM = N_IDX_HEADS * IDX_DIM

LANES = 128
SUBLANES = 8
PACKED_SUBLANES = 16
V_ROWS = HEAD_DIM + PACKED_SUBLANES
F32_MIN_NORMAL = 1.1754943508222875e-38
BF16_MIN_NORMAL_BITS = 0x0080
INT_MIN = -(2 ** 31)
NEG_BIAS = -1e30
LOG2E = math.log2(math.e)

VMEM_LIMIT = 56 * 1024 * 1024


def _params(sem, vmem=VMEM_LIMIT):
    return pltpu.CompilerParams(dimension_semantics=sem, vmem_limit_bytes=vmem)


def _resident(shape):
    nd = len(shape)
    return pl.BlockSpec(shape, lambda *_: (0,) * nd, pipeline_mode=pl.Buffered(1))


def _rms(x, g):
    ms = jnp.mean(x * x, axis=-1, keepdims=True)
    return x * lax.rsqrt(ms + NORM_EPS) * g


_C_Q = 0
_C_K = _C_Q + Q_DIM
_C_QI = _C_K + KV_DIM
_C_KI = _C_QI + IDXQ_DIM
_C_V = _C_KI + IDX_DIM
_C_WI = _C_V + KV_DIM
_C_END = _C_WI + LANES


def _dsa_proj_kernel(h_ref, g_ref, w_ref, cos_ref, sa_ref, sb_ref,
                     qt_ref, k_ref, vt_ref, qit_ref, ki_ref, wit_ref):
    u = _rms(h_ref[...], g_ref[...]).astype(BF16)
    cos = cos_ref[...]
    sa = sa_ref[...]
    sb = sb_ref[...]
    rd = HEAD_DIM // ROPE_FRAC
    half = rd // 2

    def rope(y):
        return (y * cos + pltpu.roll(y, LANES - half, 1) * sa + pltpu.roll(y, half, 1) * sb)

    def proj(c0, n):
        return jnp.dot(u, w_ref[:, c0:c0 + n], preferred_element_type=F32)

    qscale = (HEAD_DIM ** -0.5) * LOG2E
    for c in range(N_HEADS // 4):
        y = proj(_C_Q + 512 * c, 512)
        for j in range(4):
            qt_ref[4 * c + j] = (rope(y[:, 128 * j:128 * (j + 1)]) * qscale).T.astype(BF16)
    y = proj(_C_K, KV_DIM)
    for j in range(N_KV_HEADS):
        k_ref[j] = rope(y[:, 128 * j:128 * (j + 1)]).astype(BF16)
    for c in range(N_IDX_HEADS // 4):
        y = proj(_C_QI + 512 * c, 512)
        for j in range(4):
            qit_ref[4 * c + j] = (rope(y[:, 128 * j:128 * (j + 1)]) * (IDX_DIM ** -0.5)).T.astype(BF16)
    ki_ref[...] = rope(proj(_C_KI, IDX_DIM)).astype(BF16)
    y = proj(_C_V, KV_DIM)
    for j in range(N_KV_HEADS):
        vt_ref[j, 0:HEAD_DIM, :] = y[:, 128 * j:128 * (j + 1)].T.astype(BF16)
        vt_ref[j, HEAD_DIM:V_ROWS, :] = jnp.ones((V_ROWS - HEAD_DIM, vt_ref.shape[2]), BF16)
    wit_ref[...] = (proj(_C_WI, LANES) * (N_IDX_HEADS ** -0.5)).T[:N_IDX_HEADS]


def _dsa_proj(h, g, w_all, cos, sa, sb, seq, tm):
    t, d = h.shape
    nseq = seq // tm
    row = lambda i: (i, 0)
    tab = pl.BlockSpec((tm, LANES), lambda i: (i % nseq, 0))
    return pl.pallas_call(
        _dsa_proj_kernel,
        grid=(t // tm,),
        in_specs=[pl.BlockSpec((tm, d), row), _resident((1, d)), _resident(w_all.shape), tab, tab, tab],
        out_specs=[
            pl.BlockSpec((N_HEADS, HEAD_DIM, tm), lambda i: (0, 0, i)),
            pl.BlockSpec((N_KV_HEADS, tm, HEAD_DIM), lambda i: (0, i, 0)),
            pl.BlockSpec((N_KV_HEADS, V_ROWS, tm), lambda i: (0, 0, i)),
            pl.BlockSpec((N_IDX_HEADS, IDX_DIM, tm), lambda i: (0, 0, i)),
            pl.BlockSpec((tm, IDX_DIM), row),
            pl.BlockSpec((N_IDX_HEADS, tm), lambda i: (0, i)),
        ],
        out_shape=[
            jax.ShapeDtypeStruct((N_HEADS, HEAD_DIM, t), BF16),
            jax.ShapeDtypeStruct((N_KV_HEADS, t, HEAD_DIM), BF16),
            jax.ShapeDtypeStruct((N_KV_HEADS, V_ROWS, t), BF16),
            jax.ShapeDtypeStruct((N_IDX_HEADS, IDX_DIM, t), BF16),
            jax.ShapeDtypeStruct((t, IDX_DIM), BF16),
            jax.ShapeDtypeStruct((N_IDX_HEADS, t), F32),
        ],
        compiler_params=_params(("arbitrary",)),
        name="dsa_proj",
    )(h, g, w_all, cos, sa, sb)


_KT = 256


def _fold(y):
    n = y.shape[0]
    parts = [y[j] for j in range(min(4, n))]
    for j in range(4, n):
        parts[j % 4] = parts[j % 4] + y[j]
    while len(parts) > 1:
        parts = [parts[j] + parts[j + 1] if j + 1 < len(parts) else parts[j] for j in range(0, len(parts), 2)]
    return parts[0]


def _topk_mask_kernel(qit_ref, ki_ref, wit_ref, bias_ref, sc_ref, hi_ref, *, tq, tk, topk, seq):
    i = pl.program_id(1)
    q0 = i * tq
    n_adm = q0 + tq
    nkt = (n_adm + tk - 1) // tk
    nct = n_adm // _KT
    kf = float(topk)
    lane = lax.broadcasted_iota(I32, (1, tq), 1)
    adm_end = q0 + (lane // CHUNK + 1) * CHUNK

    def score_tile(kt, masked):
        k0 = pl.multiple_of(kt * tk, tk)
        kb = ki_ref[pl.ds(k0, tk), :]
        acc = None
        for h in range(N_IDX_HEADS):
            r = jnp.dot(kb, qit_ref[h], preferred_element_type=F32)
            term = wit_ref[h:h + 1, :] * jnp.maximum(r, 0.0)
            acc = term if acc is None else acc + term
        acc = jnp.where(jnp.abs(acc) < F32_MIN_NORMAL, 0.0, acc)
        bits = pltpu.bitcast(acc, I32)
        key = bits ^ ((bits >> 31) & 0x7FFFFFFF)
        top = pltpu.bitcast(bits & jnp.int32(-65536), F32)
        if masked:
            adm = (k0 + lax.broadcasted_iota(I32, (tk, tq), 0)) < adm_end
            key = jnp.where(adm, key, INT_MIN)
            top = jnp.where(adm, top, -jnp.inf)
        sc_ref[pl.ds(k0, tk), :] = key
        hi_ref[pl.ds(k0, tk), :] = top.astype(BF16)

    def interior(kt, carry):
        score_tile(kt, False)
        return carry

    def boundary(kt, carry):
        score_tile(kt, True)
        return carry

    n_int = q0 // tk
    lax.fori_loop(0, n_int, interior, 0)
    lax.fori_loop(n_int, nkt, boundary, 0)

    row_kt = lax.broadcasted_iota(I32, (_KT, tq), 0)

    def count(pred):
        def body(c, acc):
            r0 = pl.multiple_of(c * _KT, _KT)
            key = sc_ref[pl.ds(r0, _KT), :]
            ones = jnp.where(pred(key, r0 + row_kt), 1.0, 0.0)
            return acc + _fold(ones.reshape(_KT // SUBLANES, SUBLANES, tq))
        acc = lax.fori_loop(0, nct, body, jnp.zeros((SUBLANES, tq), F32))
        return jnp.sum(acc, axis=0, keepdims=True)

    def count_hi(cand):
        assert tk // PACKED_SUBLANES < 256
        def body(c, acc):
            r0 = pl.multiple_of(c * tk, tk)
            ones = jnp.where(hi_ref[pl.ds(r0, tk), :] >= cand, jnp.ones((), BF16), jnp.zeros((), BF16))
            part = _fold(ones.reshape(tk // PACKED_SUBLANES, PACKED_SUBLANES, tq))
            return acc + part.astype(F32)
        acc = lax.fori_loop(0, nkt, body, jnp.zeros((PACKED_SUBLANES, tq), F32))
        return jnp.sum(acc, axis=0, keepdims=True)

    def emit(keep):
        def body(c, carry):
            r0 = pl.multiple_of(c * _KT, _KT)
            key = sc_ref[pl.ds(r0, _KT), :]
            bias_ref[0, 0, pl.ds(r0, _KT), :] = jnp.where(keep(key, r0 + row_kt), 0.0, NEG_BIAS).astype(BF16)
            return carry
        lax.fori_loop(0, nct, body, 0)

    def hi_pass(p, state):
        u, cnt_u = state
        cand_u = u | (jnp.int32(1) << (15 - p))
        c16 = cand_u - 32768
        c16 = jnp.where(jnp.logical_and(c16 > 0, c16 < BF16_MIN_NORMAL_BITS), BF16_MIN_NORMAL_BITS, c16)
        raw = jnp.where(c16 >= 0, c16, c16 ^ 0x7FFF) << 16
        cnt = count_hi(pltpu.bitcast(raw, F32).astype(BF16))
        ok = cnt >= kf
        return jnp.where(ok, cand_u, u), jnp.where(ok, cnt, cnt_u)

    many = jnp.full((1, tq), float(seq + 1), F32)
    u, cnt_u = lax.fori_loop(0, 16, hi_pass, (jnp.zeros((1, tq), I32), many))

    few = adm_end <= topk

    def unsettled(cnt):
        return jnp.max(jnp.where(jnp.logical_or(few, cnt == kf), 0.0, 1.0)) > 0.0

    def lo_cond(state):
        p, _, _, go = state
        return jnp.logical_and(p < 16, go)

    def lo_pass(state):
        p, tau, cnt_t, _ = state
        for b in range(2):
            cand = tau | (jnp.int32(1) << (15 - p - b))
            cnt = count(lambda key, kidx: key >= cand)
            ok = cnt >= kf
            cnt_t = jnp.where(ok, cnt, cnt_t)
            tau = jnp.where(ok, cand, tau)
        return p + 2, tau, cnt_t, unsettled(cnt_t)

    _, tau, cnt_ge, _ = lax.while_loop(lo_cond, lo_pass, (jnp.int32(0), (u - 32768) << 16, cnt_u, unsettled(cnt_u)))
    tau = jnp.maximum(tau, INT_MIN + 1)
    has_tie = jnp.max(jnp.where(few, 0.0, cnt_ge)) > kf

    def fill(c, carry):
        r0 = pl.multiple_of(c * _KT, _KT)
        bias_ref[0, 0, pl.ds(r0, _KT), :] = jnp.full((_KT, tq), NEG_BIAS, BF16)
        return carry

    lax.fori_loop(nct, seq // _KT, fill, 0)

    @pl.when(jnp.logical_not(has_tie))
    def _():
        emit(lambda key, kidx: key >= tau)

    @pl.when(has_tie)
    def _():
        tied = jnp.logical_and(jnp.logical_not(few), cnt_ge > kf)
        need = jnp.where(tied, kf - count(lambda key, kidx: key > tau), 0.0)
        max_need = jnp.max(need).astype(I32)
        nbits = max(1, (seq - 1).bit_length())

        def next_tied(after):
            def body(c, acc):
                r0 = pl.multiple_of(c * _KT, _KT)
                key = sc_ref[pl.ds(r0, _KT), :]
                kidx = r0 + row_kt
                v = jnp.where(jnp.logical_and(key == tau, kidx > after), kidx, seq)
                v = v.reshape(_KT // SUBLANES, SUBLANES, tq)
                for j in range(_KT // SUBLANES):
                    acc = jnp.minimum(acc, v[j])
                return acc
            acc = lax.fori_loop(0, nct, body, jnp.full((SUBLANES, tq), seq, I32))
            return jnp.min(acc.astype(F32), axis=0, keepdims=True).astype(I32)

        def by_scan():
            def step(r, j0):
                return jnp.where(r.astype(F32) < need, next_tied(j0), j0)
            return lax.fori_loop(0, max_need, step, jnp.full((1, tq), -1, I32))

        def by_bisect():
            def idx_pass(p, j0):
                cand = j0 + (jnp.int32(1) << (nbits - 1 - p))
                c_lt = count(lambda key, kidx: jnp.logical_and(key == tau, kidx < cand))
                return jnp.where(c_lt < need, cand, j0)
            return lax.fori_loop(0, nbits, idx_pass, jnp.zeros((1, tq), I32))

        j0 = lax.cond(max_need <= nbits, by_scan, by_bisect)
        j0 = jnp.where(tied, j0, seq)
        emit(lambda key, kidx: jnp.logical_or(key > tau, jnp.logical_and(key == tau, kidx <= j0)))


def _topk_mask(qit, ki, wit, batch, seq, topk, tq, tk, tq_attn):
    assert tq % _KT == 0 and tq_attn % tq == 0 and seq % tk == 0
    nq = seq // tq
    per = tq_attn // tq
    kern = functools.partial(_topk_mask_kernel, tq=tq, tk=tk, topk=topk, seq=seq)
    return pl.pallas_call(
        kern,
        grid=(batch, nq),
        in_specs=[
            pl.BlockSpec((N_IDX_HEADS, IDX_DIM, tq), lambda b, i: (0, 0, b * nq + i)),
            pl.BlockSpec((seq, IDX_DIM), lambda b, i: (b, 0)),
            pl.BlockSpec((N_IDX_HEADS, tq), lambda b, i: (0, b * nq + i)),
        ],
        out_specs=pl.BlockSpec((1, 1, seq, tq), lambda b, i: (b, i // per, 0, i % per)),
        out_shape=jax.ShapeDtypeStruct((batch, seq // tq_attn, seq, tq_attn), BF16),
        scratch_shapes=[pltpu.VMEM((seq, tq), I32), pltpu.VMEM((seq, tq), BF16)],
        compiler_params=_params(("arbitrary", "arbitrary")),
        name="topk_mask",
    )(qit, ki, wit)


_HP = 2


def _attn_kernel(qi_ref, kj_ref, qt_ref, k_ref, vt_ref, bias_ref, o_ref,
                 m_ref, l_ref, acc_ref, sa_ref, sb_ref, ma_ref, mb_ref, bf_ref, *, tq, tk):
    n = pl.program_id(1)
    i = qi_ref[n]
    kt = kj_ref[n]
    last = (i * tq + tq + tk - 1) // tk - 1
    nstage = N_HEADS // _HP

    @pl.when(kt == 0)
    def _():
        m_ref[...] = jnp.full(m_ref.shape, NEG_BIAS, F32)
        l_ref[...] = jnp.zeros(l_ref.shape, F32)
        acc_ref[...] = jnp.zeros(acc_ref.shape, F32)

    bf_ref[...] = bias_ref[0, 0].astype(F32)

    def scores(j, buf, mx):
        for r in range(_HP):
            h = j * _HP + r
            sb = jnp.dot(k_ref[h // GROUP], qt_ref[h], preferred_element_type=F32) + bf_ref[...]
            buf[r] = sb
            mx[r] = jnp.max(sb, axis=0, keepdims=True)

    def softmax_pv(j, buf, mx):
        for r in range(_HP):
            h = j * _HP + r
            m_prev = m_ref[h]
            m_new = jnp.maximum(m_prev, mx[r])
            alpha = jnp.exp2(m_prev - m_new)
            p = jnp.exp2(buf[r] - m_new).astype(BF16)
            pv = jnp.dot(vt_ref[h // GROUP], p, preferred_element_type=F32)
            l_ref[h] = alpha * l_ref[h] + pv[HEAD_DIM:HEAD_DIM + 1]
            acc_ref[h] = alpha * acc_ref[h] + pv[:HEAD_DIM]
            m_ref[h] = m_new

    scores(0, sa_ref, ma_ref)

    def two_stages(jj, carry):
        j = 2 * jj
        scores(j + 1, sb_ref, mb_ref)
        softmax_pv(j, sa_ref, ma_ref)
        scores(j + 2, sa_ref, ma_ref)
        softmax_pv(j + 1, sb_ref, mb_ref)
        return carry

    lax.fori_loop(0, nstage // 2 - 1, two_stages, 0)
    scores(nstage - 1, sb_ref, mb_ref)
    softmax_pv(nstage - 2, sa_ref, ma_ref)
    softmax_pv(nstage - 1, sb_ref, mb_ref)

    @pl.when(kt == last)
    def _():
        for h in range(N_HEADS):
            o_ref[:, HEAD_DIM * h:HEAD_DIM * (h + 1)] = (acc_ref[h] / l_ref[h]).T.astype(BF16)


def _attention(qt, k, vt, bias, batch, seq, tq, tk):
    nq = seq // tq
    nk = seq // tk
    t = batch * seq
    pairs = [(i, j) for i in range(nq) for j in range((i * tq + tq + tk - 1) // tk)]
    qi = jnp.asarray([p[0] for p in pairs], I32)
    kj = jnp.asarray([p[1] for p in pairs], I32)
    kern = functools.partial(_attn_kernel, tq=tq, tk=tk)
    grid_spec = pltpu.PrefetchScalarGridSpec(
        num_scalar_prefetch=2,
        grid=(batch, len(pairs)),
        in_specs=[
            pl.BlockSpec((N_HEADS, HEAD_DIM, tq), lambda b, n, qi, kj: (0, 0, b * nq + qi[n])),
            pl.BlockSpec((N_KV_HEADS, tk, HEAD_DIM), lambda b, n, qi, kj: (0, b * nk + kj[n], 0)),
            pl.BlockSpec((N_KV_HEADS, V_ROWS, tk), lambda b, n, qi, kj: (0, 0, b * nk + kj[n])),
            pl.BlockSpec((1, 1, tk, tq), lambda b, n, qi, kj: (b, qi[n], kj[n], 0)),
        ],
        out_specs=pl.BlockSpec((tq, Q_DIM), lambda b, n, qi, kj: (b * nq + qi[n], 0)),
        scratch_shapes=[
            pltpu.VMEM((N_HEADS, 1, tq), F32),
            pltpu.VMEM((N_HEADS, 1, tq), F32),
            pltpu.VMEM((N_HEADS, HEAD_DIM, tq), F32),
            pltpu.VMEM((_HP, tk, tq), F32),
            pltpu.VMEM((_HP, tk, tq), F32),
            pltpu.VMEM((_HP, 1, tq), F32),
            pltpu.VMEM((_HP, 1, tq), F32),
            pltpu.VMEM((tk, tq), F32),
        ],
    )
    return pl.pallas_call(
        kern,
        grid_spec=grid_spec,
        out_shape=jax.ShapeDtypeStruct((t, Q_DIM), BF16),
        compiler_params=_params(("arbitrary", "arbitrary")),
        name="masked_attention",
    )(qi, kj, qt, k, vt, bias)


def _proj_residual_kernel(h_ref, x_ref, w_ref, *rest, has_bias, tn):
    if has_bias:
        b_ref, o_ref = rest
    else:
        (o_ref,) = rest
    x = x_ref[...]
    for c in range(w_ref.shape[1] // tn):
        cols = slice(c * tn, (c + 1) * tn)
        y = h_ref[:, cols] + jnp.dot(x, w_ref[:, cols], preferred_element_type=F32)
        if has_bias:
            y = y + b_ref[:, cols]
        o_ref[:, cols] = y


def _proj_residual(h, x, w, b, tm, name):
    t, d = h.shape
    kdim = x.shape[1]
    row = lambda i: (i, 0)
    in_specs = [pl.BlockSpec((tm, d), row), pl.BlockSpec((tm, kdim), row), _resident(w.shape)]
    args = [h, x, w]
    if b is not None:
        in_specs.append(_resident((1, d)))
        args.append(b)
    kern = functools.partial(_proj_residual_kernel, has_bias=b is not None, tn=512)
    return pl.pallas_call(
        kern,
        grid=(t // tm,),
        in_specs=in_specs,
        out_specs=pl.BlockSpec((tm, d), row),
        out_shape=jax.ShapeDtypeStruct((t, d), F32),
        compiler_params=_params(("arbitrary",)),
        name=name,
    )(*args)


def _mlp_kernel(h_ref, g_ref, w1_ref, w2_ref, o_ref, u_ref):
    @pl.when(pl.program_id(1) == 0)
    def _():
        h = h_ref[...]
        u_ref[...] = _rms(h, g_ref[...]).astype(BF16)
        o_ref[...] = h

    a = jnp.dot(u_ref[...], w1_ref[...], preferred_element_type=F32)
    a = jnp.square(jnp.maximum(a, 0.0)).astype(BF16)
    o_ref[...] += jnp.dot(a, w2_ref[...], preferred_element_type=F32)


def _mlp(h, g, w1, w2, tm, tf):
    t, d = h.shape
    f = w1.shape[1]
    return pl.pallas_call(
        _mlp_kernel,
        grid=(t // tm, f // tf),
        in_specs=[
            pl.BlockSpec((tm, d), lambda i, j: (i, 0)),
            _resident((1, d)),
            pl.BlockSpec((d, tf), lambda i, j: (0, j)),
            pl.BlockSpec((tf, d), lambda i, j: (j, 0)),
        ],
        out_specs=pl.BlockSpec((tm, d), lambda i, j: (i, 0)),
        out_shape=jax.ShapeDtypeStruct((t, d), F32),
        scratch_shapes=[pltpu.VMEM((tm, d), BF16)],
        compiler_params=_params(("arbitrary", "arbitrary")),
        name="sqrelu_mlp",
    )(h, g, w1, w2)


def _pe_kernel(h_ref, g_ref, wg_ref, p_ref, wp_ref, *rest, final, tn):
    if final:
        gf_ref, o_ref = rest
    else:
        (o_ref,) = rest
    h = h_ref[...]
    u = _rms(h, g_ref[...]).astype(BF16)
    pb = p_ref[...].astype(BF16)
    for c in range(wg_ref.shape[1] // tn):
        cols = slice(c * tn, (c + 1) * tn)
        gate = jax.nn.sigmoid(jnp.dot(u, wg_ref[:, cols], preferred_element_type=F32))
        emb = jnp.dot(pb, wp_ref[:, cols], preferred_element_type=F32)
        o_ref[:, cols] = h_ref[:, cols] + emb * gate
    if final:
        o_ref[...] = _rms(o_ref[...], gf_ref[...])


def _pe(h, g, wg, p, layer, wp, gf, tm):
    t, d = h.shape
    row = lambda i: (i, 0)
    in_specs = [pl.BlockSpec((tm, d), row), _resident((1, d)), _resident(wg.shape),
                pl.BlockSpec((None, tm, p.shape[2]), lambda r: (layer, r, 0)), _resident(wp.shape)]
    args = [h, g, wg, p, wp]
    if gf is not None:
        in_specs.append(_resident((1, d)))
        args.append(gf)
    kern = functools.partial(_pe_kernel, final=gf is not None, tn=512)
    return pl.pallas_call(
        kern,
        grid=(t // tm,),
        in_specs=in_specs,
        out_specs=pl.BlockSpec((tm, d), row),
        out_shape=jax.ShapeDtypeStruct((t, d), F32),
        compiler_params=_params(("arbitrary",)),
        name="gated_embedding",
    )(*args)


def _conv_in_kernel(h_ref, g_ref, w_ref, b_ref, y_ref, *, tn):
    d = y_ref.shape[1]
    u = _rms(h_ref[...], g_ref[...]).astype(BF16)
    for c in range(d // tn):
        a = jnp.dot(u, w_ref[:, c * tn:(c + 1) * tn], preferred_element_type=F32) + b_ref[:, c * tn:(c + 1) * tn]
        gt = (jnp.dot(u, w_ref[:, d + c * tn:d + (c + 1) * tn], preferred_element_type=F32)
              + b_ref[:, d + c * tn:d + (c + 1) * tn])
        y_ref[:, c * tn:(c + 1) * tn] = a * jax.nn.sigmoid(gt)


def _conv_in(h, g, w, b, tm):
    t, d = h.shape
    row = lambda i: (i, 0)
    return pl.pallas_call(
        functools.partial(_conv_in_kernel, tn=512),
        grid=(t // tm,),
        in_specs=[pl.BlockSpec((tm, d), row), _resident((1, d)), _resident(w.shape), _resident(b.shape)],
        out_specs=pl.BlockSpec((tm, d), row),
        out_shape=jax.ShapeDtypeStruct((t, d), F32),
        compiler_params=_params(("arbitrary",)),
        name="conv_in_glu",
    )(h, g, w, b)


_HALO = 32
_CONV_RS = 64


def _dwconv_kernel(y_ref, halo_ref, w_ref, bdw_ref, lg_ref, lb_ref, z_ref, ext_ref, cv_ref, *, tm):
    d = y_ref.shape[2]
    nc = d // LANES
    first = pl.program_id(1) == 0
    for c in range(nc):
        cols = slice(c * LANES, (c + 1) * LANES)
        ext_ref[c, 0:_HALO] = jnp.where(first, 0.0, halo_ref[0, :, cols])
        ext_ref[c, _HALO:_HALO + tm] = y_ref[0, :, cols]

    off = _HALO - (CONV_WIDTH - 1)

    def chunk(c, carry):
        for rs in range(tm // _CONV_RS):
            r0 = rs * _CONV_RS
            acc = jnp.broadcast_to(bdw_ref[c], (_CONV_RS, LANES))
            for j in range(CONV_WIDTH):
                acc = acc + w_ref[c, j:j + 1, :] * ext_ref[c, r0 + off + j:r0 + off + j + _CONV_RS, :]
            cv_ref[c, r0:r0 + _CONV_RS] = acc
        return carry

    lax.fori_loop(0, nc, chunk, 0)

    s1 = cv_ref[0]
    for c in range(1, nc):
        s1 = s1 + cv_ref[c]
    mu = jnp.sum(s1, axis=1, keepdims=True) / d
    s2 = jnp.square(cv_ref[0] - mu)
    for c in range(1, nc):
        s2 = s2 + jnp.square(cv_ref[c] - mu)
    rstd = lax.rsqrt(jnp.sum(s2, axis=1, keepdims=True) / d + NORM_EPS)
    for c in range(nc):
        yn = (cv_ref[c] - mu) * rstd * lg_ref[c] + lb_ref[c]
        z_ref[0, :, c * LANES:(c + 1) * LANES] = (yn * jax.nn.sigmoid(yn)).astype(BF16)


def _dwconv_ln_swish(y3, w3, bdw3, lg3, lb3, tm):
    batch, seq, d = y3.shape
    nc = d // LANES
    hb = tm // _HALO
    kern = functools.partial(_dwconv_kernel, tm=tm)
    return pl.pallas_call(
        kern,
        grid=(batch, seq // tm),
        in_specs=[
            pl.BlockSpec((1, tm, d), lambda b, i: (b, i, 0)),
            pl.BlockSpec((1, _HALO, d), lambda b, i: (b, jnp.maximum(i * hb - 1, 0), 0)),
            _resident(w3.shape), _resident(bdw3.shape), _resident(lg3.shape), _resident(lb3.shape),
        ],
        out_specs=pl.BlockSpec((1, tm, d), lambda b, i: (b, i, 0)),
        out_shape=jax.ShapeDtypeStruct((batch, seq, d), BF16),
        scratch_shapes=[
            pltpu.VMEM((nc, _HALO + tm, LANES), F32),
            pltpu.VMEM((nc, tm, LANES), F32),
        ],
        compiler_params=_params(("arbitrary", "arbitrary")),
        name="dwconv_ln_swish",
    )(y3, y3, w3, bdw3, lg3, lb3)


def _rope_tables(seq):
    rd = HEAD_DIM // ROPE_FRAC
    half = rd // 2
    pos = jnp.arange(seq, dtype=F32)
    inv = ROPE_THETA ** (-jnp.arange(half, dtype=F32) * 2.0 / rd)
    ang = pos[:, None] * inv[None, :]
    cos = jnp.cos(ang)
    sin = jnp.sin(ang)
    ones = jnp.ones((seq, HEAD_DIM - rd), F32)
    zeros = jnp.zeros((seq, HEAD_DIM - rd), F32)
    zh = jnp.zeros((seq, half), F32)
    cos_t = jnp.concatenate([cos, cos, ones], axis=1)
    sa_t = jnp.concatenate([-sin, zh, zeros], axis=1)
    sb_t = jnp.concatenate([zh, sin, zeros], axis=1)
    return cos_t, sa_t, sb_t


def _chunked(v, nc):
    v2 = v.reshape(-1, v.shape[-1])
    return v2.reshape(v2.shape[0], nc, LANES).transpose(1, 0, 2)


def kernel(x, p, mix_norm, mlp_norm, mlp_w1, mlp_w2, pe_proj, pe_gate_norm, pe_gate, dsa_w_in, dsa_w_out, conv_w_in, conv_b_in, conv_w_dw, conv_b_dw, conv_ln_g, conv_ln_b, conv_w_out, conv_b_out, final_norm):
    batch, seq, d = x.shape
    depth = p.shape[0]
    t = batch * seq
    topk = min(TOPK_MAX, seq // KEY_FRAC)
    tm = 512
    tm_big = min(1024, t)
    h = x.reshape(t, d)
    row_vec = lambda v: v.reshape(1, -1)

    for i in range(depth):
        j = i // 2
        if i % 2 == 0:
            w = dsa_w_in[j]
            wq, wk, wv, wqi, wki, wwi = jnp.split(
                w, [Q_DIM, Q_DIM + KV_DIM, Q_DIM + 2 * KV_DIM, Q_DIM + 2 * KV_DIM + IDXQ_DIM,
                    Q_DIM + 2 * KV_DIM + IDXQ_DIM + IDX_DIM], axis=1)
            wwi = jnp.pad(wwi, ((0, 0), (0, LANES - N_IDX_HEADS)))
            w_all = jnp.concatenate([wq, wk, wqi, wki, wv, wwi], axis=1).astype(BF16)
            cos_t, sa_t, sb_t = _rope_tables(seq)
            qt, k, vt, qit, ki, wit = _dsa_proj(h, row_vec(mix_norm[i]), w_all, cos_t, sa_t, sb_t, seq, tm)
            tq_attn = min(512, seq)
            bias = _topk_mask(qit, ki, wit, batch, seq, topk, tq=256, tk=512, tq_attn=tq_attn)
            o = _attention(qt, k, vt, bias, batch, seq, tq=tq_attn, tk=512)
            h = _proj_residual(h, o, dsa_w_out[j].astype(BF16), None, tm_big, "dsa_out")
        else:
            nc = d // LANES
            y = _conv_in(h, row_vec(mix_norm[i]), conv_w_in[j].astype(BF16), row_vec(conv_b_in[j]), tm_big)
            w3 = _chunked(jnp.pad(conv_w_dw[j], ((0, _HALO - CONV_WIDTH), (0, 0))), nc)
            z = _dwconv_ln_swish(y.reshape(batch, seq, d), w3, _chunked(conv_b_dw[j], nc),
                                 _chunked(conv_ln_g[j], nc), _chunked(conv_ln_b[j], nc), tm=256)
            h = _proj_residual(h, z.reshape(t, d), conv_w_out[j].astype(BF16), row_vec(conv_b_out[j]), tm_big,
                               "conv_out")
        h = _mlp(h, row_vec(mlp_norm[i]), mlp_w1[i].astype(BF16), mlp_w2[i].astype(BF16), tm=512, tf=2048)
        gf = row_vec(final_norm) if i == depth - 1 else None
        h = _pe(h, row_vec(pe_gate_norm[i]), pe_gate[i].astype(BF16), p.reshape(depth, t, -1), i,
                pe_proj[i].astype(BF16), gf, tm)
    return h.reshape(batch, seq, d)
```

```python
import functools
import math

import jax
import jax.numpy as jnp
from jax import lax
from jax.experimental import pallas as pl
from jax.experimental.pallas import tpu as pltpu

F32 = jnp.float32
BF16 = jnp.bfloat16
I32 = jnp.int32

CHUNK = 64
N_HEADS = 16
HEAD_DIM = 128
N_KV_HEADS = 4
GROUP = N_HEADS // N_KV_HEADS
N_IDX_HEADS = 8
IDX_DIM = 128
TOPK_MAX = 256
KEY_FRAC = 4
ROPE_THETA = 500000.0
ROPE_FRAC = 4
CONV_WIDTH = 31
NORM_EPS = 1e-6
Q_DIM = N_HEADS * HEAD_DIM
KV_DIM = N_KV_HEADS * HEAD_DIM
IDXQ_DIM = N_IDX_HEADS * IDX_DIM

LANES = 128
SUBLANES = 8
PACKED_SUBLANES = 16
V_ROWS = HEAD_DIM + PACKED_SUBLANES
F32_MIN_NORMAL = 1.1754943508222875e-38
BF16_MIN_NORMAL_BITS = 0x0080
INT_MIN = -(2 ** 31)
NEG_BIAS = -1e30
LOG2E = math.log2(math.e)

VMEM_LIMIT = 56 * 1024 * 1024


def _params(sem, vmem=VMEM_LIMIT):
    return pltpu.CompilerParams(dimension_semantics=sem, vmem_limit_bytes=vmem)


def _resident(shape):
    nd = len(shape)
    return pl.BlockSpec(shape, lambda *_: (0,) * nd, pipeline_mode=pl.Buffered(1))


def _rms(x, g):
    ms = jnp.mean(x * x, axis=-1, keepdims=True)
    return x * lax.rsqrt(ms + NORM_EPS) * g


_C_Q = 0
_C_K = _C_Q + Q_DIM
_C_QI = _C_K + KV_DIM
_C_KI = _C_QI + IDXQ_DIM
_C_V = _C_KI + IDX_DIM
_C_WI = _C_V + KV_DIM
_C_END = _C_WI + LANES


def _dsa_proj_kernel(h_ref, g_ref, w_ref, cos_ref, sa_ref, sb_ref,
                     qt_ref, k_ref, vt_ref, qit_ref, ki_ref, wit_ref):
    u = _rms(h_ref[...], g_ref[...]).astype(BF16)
    cos = cos_ref[...]
    sa = sa_ref[...]
    sb = sb_ref[...]
    rd = HEAD_DIM // ROPE_FRAC
    half = rd // 2

    def rope(y):
        return (y * cos + pltpu.roll(y, LANES - half, 1) * sa + pltpu.roll(y, half, 1) * sb)

    def proj(c0, n):
        return jnp.dot(u, w_ref[:, c0:c0 + n], preferred_element_type=F32)

    qscale = (HEAD_DIM ** -0.5) * LOG2E
    for c in range(N_HEADS // 4):
        y = proj(_C_Q + 512 * c, 512)
        for j in range(4):
            qt_ref[4 * c + j] = (rope(y[:, 128 * j:128 * (j + 1)]) * qscale).T.astype(BF16)
    y = proj(_C_K, KV_DIM)
    for j in range(N_KV_HEADS):
        k_ref[j] = rope(y[:, 128 * j:128 * (j + 1)]).astype(BF16)
    for c in range(N_IDX_HEADS // 4):
        y = proj(_C_QI + 512 * c, 512)
        for j in range(4):
            qit_ref[4 * c + j] = (rope(y[:, 128 * j:128 * (j + 1)]) * (IDX_DIM ** -0.5)).T.astype(BF16)
    ki_ref[...] = rope(proj(_C_KI, IDX_DIM)).astype(BF16)
    y = proj(_C_V, KV_DIM)
    for j in range(N_KV_HEADS):
        vt_ref[j, 0:HEAD_DIM, :] = y[:, 128 * j:128 * (j + 1)].T.astype(BF16)
        vt_ref[j, HEAD_DIM:V_ROWS, :] = jnp.ones((V_ROWS - HEAD_DIM, vt_ref.shape[2]), BF16)
    wit_ref[...] = (proj(_C_WI, LANES) * (N_IDX_HEADS ** -0.5)).T[:N_IDX_HEADS]


def _dsa_proj(h, g, w_all, cos, sa, sb, seq, tm):
    t, d = h.shape
    nseq = seq // tm
    row = lambda i: (i, 0)
    tab = pl.BlockSpec((tm, LANES), lambda i: (i % nseq, 0))
    return pl.pallas_call(
        _dsa_proj_kernel,
        grid=(t // tm,),
        in_specs=[pl.BlockSpec((tm, d), row), _resident((1, d)), _resident(w_all.shape), tab, tab, tab],
        out_specs=[
            pl.BlockSpec((N_HEADS, HEAD_DIM, tm), lambda i: (0, 0, i)),
            pl.BlockSpec((N_KV_HEADS, tm, HEAD_DIM), lambda i: (0, i, 0)),
            pl.BlockSpec((N_KV_HEADS, V_ROWS, tm), lambda i: (0, 0, i)),
            pl.BlockSpec((N_IDX_HEADS, IDX_DIM, tm), lambda i: (0, 0, i)),
            pl.BlockSpec((tm, IDX_DIM), row),
            pl.BlockSpec((N_IDX_HEADS, tm), lambda i: (0, i)),
        ],
        out_shape=[
            jax.ShapeDtypeStruct((N_HEADS, HEAD_DIM, t), BF16),
            jax.ShapeDtypeStruct((N_KV_HEADS, t, HEAD_DIM), BF16),
            jax.ShapeDtypeStruct((N_KV_HEADS, V_ROWS, t), BF16),
            jax.ShapeDtypeStruct((N_IDX_HEADS, IDX_DIM, t), BF16),
            jax.ShapeDtypeStruct((t, IDX_DIM), BF16),
            jax.ShapeDtypeStruct((N_IDX_HEADS, t), F32),
        ],
        compiler_params=_params(("arbitrary",)),
        name="dsa_proj",
    )(h, g, w_all, cos, sa, sb)


_KT = 256


def _fold(y):
    n = y.shape[0]
    parts = [y[j] for j in range(min(4, n))]
    for j in range(4, n):
        parts[j % 4] = parts[j % 4] + y[j]
    while len(parts) > 1:
        parts = [parts[j] + parts[j + 1] if j + 1 < len(parts) else parts[j] for j in range(0, len(parts), 2)]
    return parts[0]


def _topk_mask_kernel(qit_ref, ki_ref, wit_ref, bias_ref, sc_ref, hi_ref, *, tq, tk, topk, seq):
    i = pl.program_id(1)
    q0 = i * tq
    n_adm = q0 + tq
    nkt = (n_adm + tk - 1) // tk
    nct = n_adm // _KT
    kf = float(topk)
    lane = lax.broadcasted_iota(I32, (1, tq), 1)
    adm_end = q0 + (lane // CHUNK + 1) * CHUNK

    def score_tile(kt, masked):
        k0 = pl.multiple_of(kt * tk, tk)
        kb = ki_ref[pl.ds(k0, tk), :]
        acc = None
        for h in range(N_IDX_HEADS):
            r = jnp.dot(kb, qit_ref[h], preferred_element_type=F32)
            term = wit_ref[h:h + 1, :] * jnp.maximum(r, 0.0)
            acc = term if acc is None else acc + term
        acc = jnp.where(jnp.abs(acc) < F32_MIN_NORMAL, 0.0, acc)
        bits = pltpu.bitcast(acc, I32)
        key = bits ^ ((bits >> 31) & 0x7FFFFFFF)
        top = pltpu.bitcast(bits & jnp.int32(-65536), F32)
        if masked:
            adm = (k0 + lax.broadcasted_iota(I32, (tk, tq), 0)) < adm_end
            key = jnp.where(adm, key, INT_MIN)
            top = jnp.where(adm, top, -jnp.inf)
        sc_ref[pl.ds(k0, tk), :] = key
        hi_ref[pl.ds(k0, tk), :] = top.astype(BF16)

    def interior(kt, carry):
        score_tile(kt, False)
        return carry

    def boundary(kt, carry):
        score_tile(kt, True)
        return carry

    n_int = q0 // tk
    lax.fori_loop(0, n_int, interior, 0)
    lax.fori_loop(n_int, nkt, boundary, 0)

    row_kt = lax.broadcasted_iota(I32, (_KT, tq), 0)

    def count(pred):
        def body(c, acc):
            r0 = pl.multiple_of(c * _KT, _KT)
            key = sc_ref[pl.ds(r0, _KT), :]
            ones = jnp.where(pred(key, r0 + row_kt), 1.0, 0.0)
            return acc + _fold(ones.reshape(_KT // SUBLANES, SUBLANES, tq))
        acc = lax.fori_loop(0, nct, body, jnp.zeros((SUBLANES, tq), F32))
        return jnp.sum(acc, axis=0, keepdims=True)

    def count_hi(cand):
        assert tk // PACKED_SUBLANES < 256
        def body(c, acc):
            r0 = pl.multiple_of(c * tk, tk)
            ones = jnp.where(hi_ref[pl.ds(r0, tk), :] >= cand, jnp.ones((), BF16), jnp.zeros((), BF16))
            part = _fold(ones.reshape(tk // PACKED_SUBLANES, PACKED_SUBLANES, tq))
            return acc + part.astype(F32)
        acc = lax.fori_loop(0, nkt, body, jnp.zeros((PACKED_SUBLANES, tq), F32))
        return jnp.sum(acc, axis=0, keepdims=True)

    def emit(keep):
        def body(c, carry):
            r0 = pl.multiple_of(c * _KT, _KT)
            key = sc_ref[pl.ds(r0, _KT), :]
            bias_ref[0, 0, pl.ds(r0, _KT), :] = jnp.where(keep(key, r0 + row_kt), 0.0, NEG_BIAS).astype(BF16)
            return carry
        lax.fori_loop(0, nct, body, 0)

    def hi_pass(p, state):
        u, cnt_u = state
        cand_u = u | (jnp.int32(1) << (15 - p))
        c16 = cand_u - 32768
        c16 = jnp.where(jnp.logical_and(c16 > 0, c16 < BF16_MIN_NORMAL_BITS), BF16_MIN_NORMAL_BITS, c16)
        raw = jnp.where(c16 >= 0, c16, c16 ^ 0x7FFF) << 16
        cnt = count_hi(pltpu.bitcast(raw, F32).astype(BF16))
        ok = cnt >= kf
        return jnp.where(ok, cand_u, u), jnp.where(ok, cnt, cnt_u)

    many = jnp.full((1, tq), float(seq + 1), F32)
    u, cnt_u = lax.fori_loop(0, 16, hi_pass, (jnp.zeros((1, tq), I32), many))

    few = adm_end <= topk

    def unsettled(cnt):
        return jnp.max(jnp.where(jnp.logical_or(few, cnt == kf), 0.0, 1.0)) > 0.0

    def lo_cond(state):
        p, _, _, go = state
        return jnp.logical_and(p < 16, go)

    def lo_pass(state):
        p, tau, cnt_t, _ = state
        for b in range(2):
            cand = tau | (jnp.int32(1) << (15 - p - b))
            cnt = count(lambda key, kidx: key >= cand)
            ok = cnt >= kf
            cnt_t = jnp.where(ok, cnt, cnt_t)
            tau = jnp.where(ok, cand, tau)
        return p + 2, tau, cnt_t, unsettled(cnt_t)

    _, tau, cnt_ge, _ = lax.while_loop(lo_cond, lo_pass, (jnp.int32(0), (u - 32768) << 16, cnt_u, unsettled(cnt_u)))
    tau = jnp.maximum(tau, INT_MIN + 1)
    has_tie = jnp.max(jnp.where(few, 0.0, cnt_ge)) > kf

    def fill(c, carry):
        r0 = pl.multiple_of(c * _KT, _KT)
        bias_ref[0, 0, pl.ds(r0, _KT), :] = jnp.full((_KT, tq), NEG_BIAS, BF16)
        return carry

    lax.fori_loop(nct, seq // _KT, fill, 0)

    @pl.when(jnp.logical_not(has_tie))
    def _():
        emit(lambda key, kidx: key >= tau)

    @pl.when(has_tie)
    def _():
        tied = jnp.logical_and(jnp.logical_not(few), cnt_ge > kf)
        need = jnp.where(tied, kf - count(lambda key, kidx: key > tau), 0.0)
        max_need = jnp.max(need).astype(I32)
        nbits = max(1, (seq - 1).bit_length())

        def next_tied(after):
            def body(c, acc):
                r0 = pl.multiple_of(c * _KT, _KT)
                key = sc_ref[pl.ds(r0, _KT), :]
                kidx = r0 + row_kt
                v = jnp.where(jnp.logical_and(key == tau, kidx > after), kidx, seq)
                v = v.reshape(_KT // SUBLANES, SUBLANES, tq)
                for j in range(_KT // SUBLANES):
                    acc = jnp.minimum(acc, v[j])
                return acc
            acc = lax.fori_loop(0, nct, body, jnp.full((SUBLANES, tq), seq, I32))
            return jnp.min(acc.astype(F32), axis=0, keepdims=True).astype(I32)

        def by_scan():
            def step(r, j0):
                return jnp.where(r.astype(F32) < need, next_tied(j0), j0)
            return lax.fori_loop(0, max_need, step, jnp.full((1, tq), -1, I32))

        def by_bisect():
            def idx_pass(p, j0):
                cand = j0 + (jnp.int32(1) << (nbits - 1 - p))
                c_lt = count(lambda key, kidx: jnp.logical_and(key == tau, kidx < cand))
                return jnp.where(c_lt < need, cand, j0)
            return lax.fori_loop(0, nbits, idx_pass, jnp.zeros((1, tq), I32))

        j0 = lax.cond(max_need <= nbits, by_scan, by_bisect)
        j0 = jnp.where(tied, j0, seq)
        emit(lambda key, kidx: jnp.logical_or(key > tau, jnp.logical_and(key == tau, kidx <= j0)))


def _topk_mask(qit, ki, wit, batch, seq, topk, tq, tk, tq_attn):
    assert tq % _KT == 0 and tq_attn % tq == 0 and seq % tk == 0
    nq = seq // tq
    per = tq_attn // tq
    kern = functools.partial(_topk_mask_kernel, tq=tq, tk=tk, topk=topk, seq=seq)
    return pl.pallas_call(
        kern,
        grid=(batch, nq),
        in_specs=[
            pl.BlockSpec((N_IDX_HEADS, IDX_DIM, tq), lambda b, i: (0, 0, b * nq + i)),
            pl.BlockSpec((seq, IDX_DIM), lambda b, i: (b, 0)),
            pl.BlockSpec((N_IDX_HEADS, tq), lambda b, i: (0, b * nq + i)),
        ],
        out_specs=pl.BlockSpec((1, 1, seq, tq), lambda b, i: (b, i // per, 0, i % per)),
        out_shape=jax.ShapeDtypeStruct((batch, seq // tq_attn, seq, tq_attn), BF16),
        scratch_shapes=[pltpu.VMEM((seq, tq), I32), pltpu.VMEM((seq, tq), BF16)],
        compiler_params=_params(("arbitrary", "arbitrary")),
        name="topk_mask",
    )(qit, ki, wit)


_HP = 2


def _attn_kernel(qi_ref, kj_ref, qt_ref, k_ref, vt_ref, bias_ref, o_ref,
                 m_ref, l_ref, acc_ref, sa_ref, sb_ref, ma_ref, mb_ref, bf_ref, *, tq, tk):
    n = pl.program_id(1)
    i = qi_ref[n]
    kt = kj_ref[n]
    last = (i * tq + tq + tk - 1) // tk - 1
    nstage = N_HEADS // _HP

    @pl.when(kt == 0)
    def _():
        m_ref[...] = jnp.full(m_ref.shape, NEG_BIAS, F32)
        l_ref[...] = jnp.zeros(l_ref.shape, F32)
        acc_ref[...] = jnp.zeros(acc_ref.shape, F32)

    bf_ref[...] = bias_ref[0, 0].astype(F32)

    def scores(j, buf, mx):
        for r in range(_HP):
            h = j * _HP + r
            sb = jnp.dot(k_ref[h // GROUP], qt_ref[h], preferred_element_type=F32) + bf_ref[...]
            buf[r] = sb
            mx[r] = jnp.max(sb, axis=0, keepdims=True)

    def softmax_pv(j, buf, mx):
        for r in range(_HP):
            h = j * _HP + r
            m_prev = m_ref[h]
            m_new = jnp.maximum(m_prev, mx[r])
            alpha = jnp.exp2(m_prev - m_new)
            p = jnp.exp2(buf[r] - m_new).astype(BF16)
            pv = jnp.dot(vt_ref[h // GROUP], p, preferred_element_type=F32)
            l_ref[h] = alpha * l_ref[h] + pv[HEAD_DIM:HEAD_DIM + 1]
            acc_ref[h] = alpha * acc_ref[h] + pv[:HEAD_DIM]
            m_ref[h] = m_new

    scores(0, sa_ref, ma_ref)

    def two_stages(jj, carry):
        j = 2 * jj
        scores(j + 1, sb_ref, mb_ref)
        softmax_pv(j, sa_ref, ma_ref)
        scores(j + 2, sa_ref, ma_ref)
        softmax_pv(j + 1, sb_ref, mb_ref)
        return carry

    lax.fori_loop(0, nstage // 2 - 1, two_stages, 0)
    scores(nstage - 1, sb_ref, mb_ref)
    softmax_pv(nstage - 2, sa_ref, ma_ref)
    softmax_pv(nstage - 1, sb_ref, mb_ref)

    @pl.when(kt == last)
    def _():
        for h in range(N_HEADS):
            o_ref[:, HEAD_DIM * h:HEAD_DIM * (h + 1)] = (acc_ref[h] / l_ref[h]).T.astype(BF16)


def _attention(qt, k, vt, bias, batch, seq, tq, tk):
    nq = seq // tq
    nk = seq // tk
    t = batch * seq
    pairs = [(i, j) for i in range(nq) for j in range((i * tq + tq + tk - 1) // tk)]
    qi = jnp.asarray([p[0] for p in pairs], I32)
    kj = jnp.asarray([p[1] for p in pairs], I32)
    kern = functools.partial(_attn_kernel, tq=tq, tk=tk)
    grid_spec = pltpu.PrefetchScalarGridSpec(
        num_scalar_prefetch=2,
        grid=(batch, len(pairs)),
        in_specs=[
            pl.BlockSpec((N_HEADS, HEAD_DIM, tq), lambda b, n, qi, kj: (0, 0, b * nq + qi[n])),
            pl.BlockSpec((N_KV_HEADS, tk, HEAD_DIM), lambda b, n, qi, kj: (0, b * nk + kj[n], 0)),
            pl.BlockSpec((N_KV_HEADS, V_ROWS, tk), lambda b, n, qi, kj: (0, 0, b * nk + kj[n])),
            pl.BlockSpec((1, 1, tk, tq), lambda b, n, qi, kj: (b, qi[n], kj[n], 0)),
        ],
        out_specs=pl.BlockSpec((tq, Q_DIM), lambda b, n, qi, kj: (b * nq + qi[n], 0)),
        scratch_shapes=[
            pltpu.VMEM((N_HEADS, 1, tq), F32),
            pltpu.VMEM((N_HEADS, 1, tq), F32),
            pltpu.VMEM((N_HEADS, HEAD_DIM, tq), F32),
            pltpu.VMEM((_HP, tk, tq), F32),
            pltpu.VMEM((_HP, tk, tq), F32),
            pltpu.VMEM((_HP, 1, tq), F32),
            pltpu.VMEM((_HP, 1, tq), F32),
            pltpu.VMEM((tk, tq), F32),
        ],
    )
    return pl.pallas_call(
        kern,
        grid_spec=grid_spec,
        out_shape=jax.ShapeDtypeStruct((t, Q_DIM), BF16),
        compiler_params=_params(("arbitrary", "arbitrary")),
        name="masked_attention",
    )(qi, kj, qt, k, vt, bias)


def _proj_residual_kernel(h_ref, x_ref, w_ref, *rest, has_bias, tn):
    if has_bias:
        b_ref, o_ref = rest
    else:
        (o_ref,) = rest
    x = x_ref[...]
    for c in range(w_ref.shape[1] // tn):
        cols = slice(c * tn, (c + 1) * tn)
        y = h_ref[:, cols] + jnp.dot(x, w_ref[:, cols], preferred_element_type=F32)
        if has_bias:
            y = y + b_ref[:, cols]
        o_ref[:, cols] = y


def _proj_residual(h, x, w, b, tm, name):
    t, d = h.shape
    kdim = x.shape[1]
    row = lambda i: (i, 0)
    in_specs = [pl.BlockSpec((tm, d), row), pl.BlockSpec((tm, kdim), row), _resident(w.shape)]
    args = [h, x, w]
    if b is not None:
        in_specs.append(_resident((1, d)))
        args.append(b)
    kern = functools.partial(_proj_residual_kernel, has_bias=b is not None, tn=512)
    return pl.pallas_call(
        kern,
        grid=(t // tm,),
        in_specs=in_specs,
        out_specs=pl.BlockSpec((tm, d), row),
        out_shape=jax.ShapeDtypeStruct((t, d), F32),
        compiler_params=_params(("arbitrary",)),
        name=name,
    )(*args)


def _mlp_kernel(h_ref, g_ref, w1_ref, w2_ref, o_ref, u_ref):
    @pl.when(pl.program_id(1) == 0)
    def _():
        h = h_ref[...]
        u_ref[...] = _rms(h, g_ref[...]).astype(BF16)
        o_ref[...] = h

    a = jnp.dot(u_ref[...], w1_ref[...], preferred_element_type=F32)
    a = jnp.square(jnp.maximum(a, 0.0)).astype(BF16)
    o_ref[...] += jnp.dot(a, w2_ref[...], preferred_element_type=F32)


def _mlp(h, g, w1, w2, tm, tf):
    t, d = h.shape
    f = w1.shape[1]
    return pl.pallas_call(
        _mlp_kernel,
        grid=(t // tm, f // tf),
        in_specs=[
            pl.BlockSpec((tm, d), lambda i, j: (i, 0)),
            _resident((1, d)),
            pl.BlockSpec((d, tf), lambda i, j: (0, j)),
            pl.BlockSpec((tf, d), lambda i, j: (j, 0)),
        ],
        out_specs=pl.BlockSpec((tm, d), lambda i, j: (i, 0)),
        out_shape=jax.ShapeDtypeStruct((t, d), F32),
        scratch_shapes=[pltpu.VMEM((tm, d), BF16)],
        compiler_params=_params(("arbitrary", "arbitrary")),
        name="sqrelu_mlp",
    )(h, g, w1, w2)


def _pe_kernel(h_ref, g_ref, wg_ref, p_ref, wp_ref, *rest, final, tn):
    if final:
        gf_ref, o_ref = rest
    else:
        (o_ref,) = rest
    h = h_ref[...]
    u = _rms(h, g_ref[...]).astype(BF16)
    pb = p_ref[...].astype(BF16)
    for c in range(wg_ref.shape[1] // tn):
        cols = slice(c * tn, (c + 1) * tn)
        gate = jax.nn.sigmoid(jnp.dot(u, wg_ref[:, cols], preferred_element_type=F32))
        emb = jnp.dot(pb, wp_ref[:, cols], preferred_element_type=F32)
        o_ref[:, cols] = h_ref[:, cols] + emb * gate
    if final:
        o_ref[...] = _rms(o_ref[...], gf_ref[...])


def _pe(h, g, wg, p, layer, wp, gf, tm):
    t, d = h.shape
    row = lambda i: (i, 0)
    in_specs = [pl.BlockSpec((tm, d), row), _resident((1, d)), _resident(wg.shape),
                pl.BlockSpec((None, tm, p.shape[2]), lambda r: (layer, r, 0)), _resident(wp.shape)]
    args = [h, g, wg, p, wp]
    if gf is not None:
        in_specs.append(_resident((1, d)))
        args.append(gf)
    kern = functools.partial(_pe_kernel, final=gf is not None, tn=512)
    return pl.pallas_call(
        kern,
        grid=(t // tm,),
        in_specs=in_specs,
        out_specs=pl.BlockSpec((tm, d), row),
        out_shape=jax.ShapeDtypeStruct((t, d), F32),
        compiler_params=_params(("arbitrary",)),
        name="gated_embedding",
    )(*args)


def _conv_in_kernel(h_ref, g_ref, w_ref, b_ref, y_ref, *, tn):
    d = y_ref.shape[1]
    u = _rms(h_ref[...], g_ref[...]).astype(BF16)
    for c in range(d // tn):
        a = jnp.dot(u, w_ref[:, c * tn:(c + 1) * tn], preferred_element_type=F32) + b_ref[:, c * tn:(c + 1) * tn]
        gt = (jnp.dot(u, w_ref[:, d + c * tn:d + (c + 1) * tn], preferred_element_type=F32)
              + b_ref[:, d + c * tn:d + (c + 1) * tn])
        y_ref[:, c * tn:(c + 1) * tn] = a * jax.nn.sigmoid(gt)


def _conv_in(h, g, w, b, tm):
    t, d = h.shape
    row = lambda i: (i, 0)
    return pl.pallas_call(
        functools.partial(_conv_in_kernel, tn=512),
        grid=(t // tm,),
        in_specs=[pl.BlockSpec((tm, d), row), _resident((1, d)), _resident(w.shape), _resident(b.shape)],
        out_specs=pl.BlockSpec((tm, d), row),
        out_shape=jax.ShapeDtypeStruct((t, d), F32),
        compiler_params=_params(("arbitrary",)),
        name="conv_in_glu",
    )(h, g, w, b)


_HALO = 32
_CONV_RS = 64


def _dwconv_kernel(y_ref, halo_ref, w_ref, bdw_ref, lg_ref, lb_ref, z_ref, ext_ref, cv_ref, *, tm):
    d = y_ref.shape[2]
    nc = d // LANES
    first = pl.program_id(1) == 0
    for c in range(nc):
        cols = slice(c * LANES, (c + 1) * LANES)
        ext_ref[c, 0:_HALO] = jnp.where(first, 0.0, halo_ref[0, :, cols])
        ext_ref[c, _HALO:_HALO + tm] = y_ref[0, :, cols]

    off = _HALO - (CONV_WIDTH - 1)

    def chunk(c, carry):
        for rs in range(tm // _CONV_RS):
            r0 = rs * _CONV_RS
            acc = jnp.broadcast_to(bdw_ref[c], (_CONV_RS, LANES))
            for j in range(CONV_WIDTH):
                acc = acc + w_ref[c, j:j + 1, :] * ext_ref[c, r0 + off + j:r0 + off + j + _CONV_RS, :]
            cv_ref[c, r0:r0 + _CONV_RS] = acc
        return carry

    lax.fori_loop(0, nc, chunk, 0)

    s1 = cv_ref[0]
    for c in range(1, nc):
        s1 = s1 + cv_ref[c]
    mu = jnp.sum(s1, axis=1, keepdims=True) / d
    s2 = jnp.square(cv_ref[0] - mu)
    for c in range(1, nc):
        s2 = s2 + jnp.square(cv_ref[c] - mu)
    rstd = lax.rsqrt(jnp.sum(s2, axis=1, keepdims=True) / d + NORM_EPS)
    for c in range(nc):
        yn = (cv_ref[c] - mu) * rstd * lg_ref[c] + lb_ref[c]
        z_ref[0, :, c * LANES:(c + 1) * LANES] = (yn * jax.nn.sigmoid(yn)).astype(BF16)


def _dwconv_ln_swish(y3, w3, bdw3, lg3, lb3, tm):
    batch, seq, d = y3.shape
    nc = d // LANES
    hb = tm // _HALO
    kern = functools.partial(_dwconv_kernel, tm=tm)
    return pl.pallas_call(
        kern,
        grid=(batch, seq // tm),
        in_specs=[
            pl.BlockSpec((1, tm, d), lambda b, i: (b, i, 0)),
            pl.BlockSpec((1, _HALO, d), lambda b, i: (b, jnp.maximum(i * hb - 1, 0), 0)),
            _resident(w3.shape), _resident(bdw3.shape), _resident(lg3.shape), _resident(lb3.shape),
        ],
        out_specs=pl.BlockSpec((1, tm, d), lambda b, i: (b, i, 0)),
        out_shape=jax.ShapeDtypeStruct((batch, seq, d), BF16),
        scratch_shapes=[
            pltpu.VMEM((nc, _HALO + tm, LANES), F32),
            pltpu.VMEM((nc, tm, LANES), F32),
        ],
        compiler_params=_params(("arbitrary", "arbitrary")),
        name="dwconv_ln_swish",
    )(y3, y3, w3, bdw3, lg3, lb3)


def _rope_tables(seq):
    rd = HEAD_DIM // ROPE_FRAC
    half = rd // 2
    pos = jnp.arange(seq, dtype=F32)
    inv = ROPE_THETA ** (-jnp.arange(half, dtype=F32) * 2.0 / rd)
    ang = pos[:, None] * inv[None, :]
    cos = jnp.cos(ang)
    sin = jnp.sin(ang)
    ones = jnp.ones((seq, HEAD_DIM - rd), F32)
    zeros = jnp.zeros((seq, HEAD_DIM - rd), F32)
    zh = jnp.zeros((seq, half), F32)
    cos_t = jnp.concatenate([cos, cos, ones], axis=1)
    sa_t = jnp.concatenate([-sin, zh, zeros], axis=1)
    sb_t = jnp.concatenate([zh, sin, zeros], axis=1)
    return cos_t, sa_t, sb_t


def _chunked(v, nc):
    v2 = v.reshape(-1, v.shape[-1])
    return v2.reshape(v2.shape[0], nc, LANES).transpose(1, 0, 2)


def kernel(x, p, mix_norm, mlp_norm, mlp_w1, mlp_w2, pe_proj, pe_gate_norm, pe_gate, dsa_w_in, dsa_w_out, conv_w_in, conv_b_in, conv_w_dw, conv_b_dw, conv_ln_g, conv_ln_b, conv_w_out, conv_b_out, final_norm):
    batch, seq, d = x.shape
    depth = p.shape[0]
    t = batch * seq
    topk = min(TOPK_MAX, seq // KEY_FRAC)
    tm = 512
    tm_big = min(1024, t)
    h = x.reshape(t, d)
    row_vec = lambda v: v.reshape(1, -1)

    for i in range(depth):
        j = i // 2
        if i % 2 == 0:
            w = dsa_w_in[j]
            wq, wk, wv, wqi, wki, wwi = jnp.split(
                w, [Q_DIM, Q_DIM + KV_DIM, Q_DIM + 2 * KV_DIM, Q_DIM + 2 * KV_DIM + IDXQ_DIM,
                    Q_DIM + 2 * KV_DIM + IDXQ_DIM + IDX_DIM], axis=1)
            wwi = jnp.pad(wwi, ((0, 0), (0, LANES - N_IDX_HEADS)))
            w_all = jnp.concatenate([wq, wk, wqi, wki, wv, wwi], axis=1).astype(BF16)
            cos_t, sa_t, sb_t = _rope_tables(seq)
            qt, k, vt, qit, ki, wit = _dsa_proj(h, row_vec(mix_norm[i]), w_all, cos_t, sa_t, sb_t, seq, tm)
            tq_attn = min(512, seq)
            bias = _topk_mask(qit, ki, wit, batch, seq, topk, tq=256, tk=512, tq_attn=tq_attn)
            o = _attention(qt, k, vt, bias, batch, seq, tq=tq_attn, tk=min(1024, seq))
            h = _proj_residual(h, o, dsa_w_out[j].astype(BF16), None, tm_big, "dsa_out")
        else:
            nc = d // LANES
            y = _conv_in(h, row_vec(mix_norm[i]), conv_w_in[j].astype(BF16), row_vec(conv_b_in[j]), tm_big)
            w3 = _chunked(jnp.pad(conv_w_dw[j], ((0, _HALO - CONV_WIDTH), (0, 0))), nc)
            z = _dwconv_ln_swish(y.reshape(batch, seq, d), w3, _chunked(conv_b_dw[j], nc),
                                 _chunked(conv_ln_g[j], nc), _chunked(conv_ln_b[j], nc), tm=512)
            h = _proj_residual(h, z.reshape(t, d), conv_w_out[j].astype(BF16), row_vec(conv_b_out[j]), tm_big,
                               "conv_out")
        h = _mlp(h, row_vec(mlp_norm[i]), mlp_w1[i].astype(BF16), mlp_w2[i].astype(BF16), tm=512, tf=2048)
        gf = row_vec(final_norm) if i == depth - 1 else None
        h = _pe(h, row_vec(pe_gate_norm[i]), pe_gate[i].astype(BF16), p.reshape(depth, t, -1), i,
                pe_proj[i].astype(BF16), gf, tm)
    return h.reshape(batch, seq, d)
```

```python
import functools
import math

import jax
import jax.numpy as jnp
from jax import lax
from jax.experimental import pallas as pl
from jax.experimental.pallas import tpu as pltpu

F32 = jnp.float32
BF16 = jnp.bfloat16
I32 = jnp.int32

CHUNK = 64
N_HEADS = 16
HEAD_DIM = 128
N_KV_HEADS = 4
GROUP = N_HEADS // N_KV_HEADS
N_IDX_HEADS = 8
IDX_DIM = 128
TOPK_MAX = 256
KEY_FRAC = 4
ROPE_THETA = 500000.0
ROPE_FRAC = 4
CONV_WIDTH = 31
NORM_EPS = 1e-6
Q_DIM = N_HEADS * HEAD_DIM
KV_DIM = N_KV_HEADS * HEAD_DIM
IDXQ_DIM = N_IDX_HEADS * IDX_DIM

LANES = 128
SUBLANES = 8
PACKED_SUBLANES = 16
V_ROWS = HEAD_DIM + PACKED_SUBLANES
F32_MIN_NORMAL = 1.1754943508222875e-38
BF16_MIN_NORMAL_BITS = 0x0080
INT_MIN = -(2 ** 31)
NEG_BIAS = -1e30
LOG2E = math.log2(math.e)

VMEM_LIMIT = 56 * 1024 * 1024


def _params(sem, vmem=VMEM_LIMIT):
    return pltpu.CompilerParams(dimension_semantics=sem, vmem_limit_bytes=vmem)


def _resident(shape):
    nd = len(shape)
    return pl.BlockSpec(shape, lambda *_: (0,) * nd, pipeline_mode=pl.Buffered(1))


def _rms(x, g):
    ms = jnp.mean(x * x, axis=-1, keepdims=True)
    return x * lax.rsqrt(ms + NORM_EPS) * g


_C_Q = 0
_C_K = _C_Q + Q_DIM
_C_QI = _C_K + KV_DIM
_C_V = _C_QI + IDXQ_DIM
_C_KI = _C_V + KV_DIM
_C_WI = _C_KI + IDX_DIM
_C_END = _C_WI + LANES


def _dsa_proj_kernel(h_ref, g_ref, w_ref, cos_ref, sa_ref, sb_ref,
                     qt_ref, k_ref, vt_ref, qit_ref, ki_ref, wit_ref):
    u = _rms(h_ref[...], g_ref[...]).astype(BF16)
    cos = cos_ref[...]
    sa = sa_ref[...]
    sb = sb_ref[...]
    rd = HEAD_DIM // ROPE_FRAC
    half = rd // 2

    def rope(y):
        return (y * cos + pltpu.roll(y, LANES - half, 1) * sa + pltpu.roll(y, half, 1) * sb)

    def proj(c0, n):
        return jnp.dot(u, w_ref[:, c0:c0 + n], preferred_element_type=F32)

    qscale = (HEAD_DIM ** -0.5) * LOG2E
    for c in range(N_HEADS // 4):
        y = proj(_C_Q + 512 * c, 512)
        for j in range(4):
            qt_ref[4 * c + j] = (rope(y[:, 128 * j:128 * (j + 1)]) * qscale).T.astype(BF16)
    y = proj(_C_K, KV_DIM)
    for j in range(N_KV_HEADS):
        k_ref[j] = rope(y[:, 128 * j:128 * (j + 1)]).astype(BF16)
    for c in range(N_IDX_HEADS // 4):
        y = proj(_C_QI + 512 * c, 512)
        for j in range(4):
            qit_ref[4 * c + j] = (rope(y[:, 128 * j:128 * (j + 1)]) * (IDX_DIM ** -0.5)).T.astype(BF16)
    y = proj(_C_KI, IDX_DIM + LANES)
    ki_ref[...] = rope(y[:, :IDX_DIM]).astype(BF16)
    wit_ref[...] = (y[:, IDX_DIM:] * (N_IDX_HEADS ** -0.5)).T[:N_IDX_HEADS]
    y = proj(_C_V, KV_DIM)
    for j in range(N_KV_HEADS):
        vt_ref[j, 0:HEAD_DIM, :] = y[:, 128 * j:128 * (j + 1)].T.astype(BF16)
        vt_ref[j, HEAD_DIM:V_ROWS, :] = jnp.ones((V_ROWS - HEAD_DIM, vt_ref.shape[2]), BF16)


def _dsa_proj(h, g, w_all, cos, sa, sb, seq, tm):
    t, d = h.shape
    nseq = seq // tm
    row = lambda i: (i, 0)
    tab = pl.BlockSpec((tm, LANES), lambda i: (i % nseq, 0))
    return pl.pallas_call(
        _dsa_proj_kernel,
        grid=(t // tm,),
        in_specs=[pl.BlockSpec((tm, d), row), _resident((1, d)), _resident(w_all.shape), tab, tab, tab],
        out_specs=[
            pl.BlockSpec((N_HEADS, HEAD_DIM, tm), lambda i: (0, 0, i)),
            pl.BlockSpec((N_KV_HEADS, tm, HEAD_DIM), lambda i: (0, i, 0)),
            pl.BlockSpec((N_KV_HEADS, V_ROWS, tm), lambda i: (0, 0, i)),
            pl.BlockSpec((N_IDX_HEADS, IDX_DIM, tm), lambda i: (0, 0, i)),
            pl.BlockSpec((tm, IDX_DIM), row),
            pl.BlockSpec((N_IDX_HEADS, tm), lambda i: (0, i)),
        ],
        out_shape=[
            jax.ShapeDtypeStruct((N_HEADS, HEAD_DIM, t), BF16),
            jax.ShapeDtypeStruct((N_KV_HEADS, t, HEAD_DIM), BF16),
            jax.ShapeDtypeStruct((N_KV_HEADS, V_ROWS, t), BF16),
            jax.ShapeDtypeStruct((N_IDX_HEADS, IDX_DIM, t), BF16),
            jax.ShapeDtypeStruct((t, IDX_DIM), BF16),
            jax.ShapeDtypeStruct((N_IDX_HEADS, t), F32),
        ],
        compiler_params=_params(("arbitrary",)),
        name="dsa_proj",
    )(h, g, w_all, cos, sa, sb)


_KT = 256
_LO_FIXED = 6


def _fold(y):
    n = y.shape[0]
    parts = [y[j] for j in range(min(4, n))]
    for j in range(4, n):
        parts[j % 4] = parts[j % 4] + y[j]
    while len(parts) > 1:
        parts = [parts[j] + parts[j + 1] if j + 1 < len(parts) else parts[j] for j in range(0, len(parts), 2)]
    return parts[0]


def _topk_mask_kernel(qit_ref, ki_ref, wit_ref, bias_ref, sc_ref, hi_ref, *, tq, tk, topk, seq):
    i = pl.program_id(1)
    q0 = i * tq
    n_adm = q0 + tq
    nkt = (n_adm + tk - 1) // tk
    nct = n_adm // _KT
    kf = float(topk)
    lane = lax.broadcasted_iota(I32, (1, tq), 1)
    adm_end = q0 + (lane // CHUNK + 1) * CHUNK

    def score_tile(kt, masked):
        k0 = pl.multiple_of(kt * tk, tk)
        kb = ki_ref[pl.ds(k0, tk), :]
        acc = None
        for h in range(N_IDX_HEADS):
            r = jnp.dot(kb, qit_ref[h], preferred_element_type=F32)
            term = wit_ref[h:h + 1, :] * jnp.maximum(r, 0.0)
            acc = term if acc is None else acc + term
        acc = jnp.where(jnp.abs(acc) < F32_MIN_NORMAL, 0.0, acc)
        bits = pltpu.bitcast(acc, I32)
        key = bits ^ ((bits >> 31) & 0x7FFFFFFF)
        top = pltpu.bitcast(bits & jnp.int32(-65536), F32)
        if masked:
            adm = (k0 + lax.broadcasted_iota(I32, (tk, tq), 0)) < adm_end
            key = jnp.where(adm, key, INT_MIN)
            top = jnp.where(adm, top, -jnp.inf)
        sc_ref[pl.ds(k0, tk), :] = key
        hi_ref[pl.ds(k0, tk), :] = top.astype(BF16)

    def interior(kt, carry):
        score_tile(kt, False)
        return carry

    def boundary(kt, carry):
        score_tile(kt, True)
        return carry

    n_int = q0 // tk
    lax.fori_loop(0, n_int, interior, 0)
    lax.fori_loop(n_int, nkt, boundary, 0)

    row_kt = lax.broadcasted_iota(I32, (_KT, tq), 0)

    def count(pred):
        def body(c, acc):
            r0 = pl.multiple_of(c * _KT, _KT)
            key = sc_ref[pl.ds(r0, _KT), :]
            ones = jnp.where(pred(key, r0 + row_kt), 1.0, 0.0)
            return acc + _fold(ones.reshape(_KT // SUBLANES, SUBLANES, tq))
        acc = lax.fori_loop(0, nct, body, jnp.zeros((SUBLANES, tq), F32))
        return jnp.sum(acc, axis=0, keepdims=True)

    def count_hi(cand):
        assert tk // PACKED_SUBLANES < 256
        def body(c, acc):
            r0 = pl.multiple_of(c * tk, tk)
            ones = jnp.where(hi_ref[pl.ds(r0, tk), :] >= cand, jnp.ones((), BF16), jnp.zeros((), BF16))
            part = _fold(ones.reshape(tk // PACKED_SUBLANES, PACKED_SUBLANES, tq))
            return acc + part.astype(F32)
        acc = lax.fori_loop(0, nkt, body, jnp.zeros((PACKED_SUBLANES, tq), F32))
        return jnp.sum(acc, axis=0, keepdims=True)

    def emit(keep):
        def body(c, carry):
            r0 = pl.multiple_of(c * _KT, _KT)
            key = sc_ref[pl.ds(r0, _KT), :]
            bias_ref[0, 0, pl.ds(r0, _KT), :] = jnp.where(keep(key, r0 + row_kt), 0.0, NEG_BIAS).astype(BF16)
            return carry
        lax.fori_loop(0, nct, body, 0)

    def hi_pass(p, state):
        u, cnt_u = state
        cand_u = u | (jnp.int32(1) << (15 - p))
        c16 = cand_u - 32768
        c16 = jnp.where(jnp.logical_and(c16 > 0, c16 < BF16_MIN_NORMAL_BITS), BF16_MIN_NORMAL_BITS, c16)
        raw = jnp.where(c16 >= 0, c16, c16 ^ 0x7FFF) << 16
        cnt = count_hi(pltpu.bitcast(raw, F32).astype(BF16))
        ok = cnt >= kf
        return jnp.where(ok, cand_u, u), jnp.where(ok, cnt, cnt_u)

    many = jnp.full((1, tq), float(seq + 1), F32)
    u, cnt_u = lax.fori_loop(0, 16, hi_pass, (jnp.zeros((1, tq), I32), many))

    few = adm_end <= topk

    def unsettled(cnt):
        return jnp.max(jnp.where(jnp.logical_or(few, cnt == kf), 0.0, 1.0)) > 0.0

    def lo_cond(state):
        p, _, _, go = state
        return jnp.logical_and(p < 16, go)

    def two_passes(p, tau, cnt_t):
        for b in range(2):
            cand = tau | (jnp.int32(1) << (15 - p - b))
            cnt = count(lambda key, kidx: key >= cand)
            ok = cnt >= kf
            cnt_t = jnp.where(ok, cnt, cnt_t)
            tau = jnp.where(ok, cand, tau)
        return tau, cnt_t

    def lo_fixed(jj, state):
        return two_passes(2 * jj, *state)

    def lo_pass(state):
        p, tau, cnt_t, _ = state
        tau, cnt_t = two_passes(p, tau, cnt_t)
        return p + 2, tau, cnt_t, unsettled(cnt_t)

    tau, cnt_t = lax.fori_loop(0, _LO_FIXED // 2, lo_fixed, ((u - 32768) << 16, cnt_u))
    _, tau, cnt_ge, _ = lax.while_loop(lo_cond, lo_pass, (jnp.int32(_LO_FIXED), tau, cnt_t, unsettled(cnt_t)))
    tau = jnp.maximum(tau, INT_MIN + 1)
    has_tie = jnp.max(jnp.where(few, 0.0, cnt_ge)) > kf

    def fill(c, carry):
        r0 = pl.multiple_of(c * _KT, _KT)
        bias_ref[0, 0, pl.ds(r0, _KT), :] = jnp.full((_KT, tq), NEG_BIAS, BF16)
        return carry

    lax.fori_loop(nct, seq // _KT, fill, 0)

    @pl.when(jnp.logical_not(has_tie))
    def _():
        emit(lambda key, kidx: key >= tau)

    @pl.when(has_tie)
    def _():
        tied = jnp.logical_and(jnp.logical_not(few), cnt_ge > kf)
        need = jnp.where(tied, kf - count(lambda key, kidx: key > tau), 0.0)
        max_need = jnp.max(need).astype(I32)
        nbits = max(1, (seq - 1).bit_length())

        def next_tied(after):
            def body(c, acc):
                r0 = pl.multiple_of(c * _KT, _KT)
                key = sc_ref[pl.ds(r0, _KT), :]
                kidx = r0 + row_kt
                v = jnp.where(jnp.logical_and(key == tau, kidx > after), kidx, seq)
                v = v.reshape(_KT // SUBLANES, SUBLANES, tq)
                for j in range(_KT // SUBLANES):
                    acc = jnp.minimum(acc, v[j])
                return acc
            acc = lax.fori_loop(0, nct, body, jnp.full((SUBLANES, tq), seq, I32))
            return jnp.min(acc.astype(F32), axis=0, keepdims=True).astype(I32)

        def by_scan():
            def step(r, j0):
                return jnp.where(r.astype(F32) < need, next_tied(j0), j0)
            return lax.fori_loop(0, max_need, step, jnp.full((1, tq), -1, I32))

        def by_bisect():
            def idx_pass(p, j0):
                cand = j0 + (jnp.int32(1) << (nbits - 1 - p))
                c_lt = count(lambda key, kidx: jnp.logical_and(key == tau, kidx < cand))
                return jnp.where(c_lt < need, cand, j0)
            return lax.fori_loop(0, nbits, idx_pass, jnp.zeros((1, tq), I32))

        j0 = lax.cond(max_need <= nbits, by_scan, by_bisect)
        j0 = jnp.where(tied, j0, seq)
        emit(lambda key, kidx: jnp.logical_or(key > tau, jnp.logical_and(key == tau, kidx <= j0)))


def _topk_mask(qit, ki, wit, batch, seq, topk, tq, tk, tq_attn):
    assert tq % _KT == 0 and tq_attn % tq == 0 and seq % tk == 0
    nq = seq // tq
    per = tq_attn // tq
    kern = functools.partial(_topk_mask_kernel, tq=tq, tk=tk, topk=topk, seq=seq)
    return pl.pallas_call(
        kern,
        grid=(batch, nq),
        in_specs=[
            pl.BlockSpec((N_IDX_HEADS, IDX_DIM, tq), lambda b, i: (0, 0, b * nq + i)),
            pl.BlockSpec((seq, IDX_DIM), lambda b, i: (b, 0)),
            pl.BlockSpec((N_IDX_HEADS, tq), lambda b, i: (0, b * nq + i)),
        ],
        out_specs=pl.BlockSpec((1, 1, seq, tq), lambda b, i: (b, i // per, 0, i % per)),
        out_shape=jax.ShapeDtypeStruct((batch, seq // tq_attn, seq, tq_attn), BF16),
        scratch_shapes=[pltpu.VMEM((seq, tq), I32), pltpu.VMEM((seq, tq), BF16)],
        compiler_params=_params(("arbitrary", "arbitrary")),
        name="topk_mask",
    )(qit, ki, wit)


_HP = 2


def _attn_kernel(qi_ref, kj_ref, qt_ref, k_ref, vt_ref, bias_ref, o_ref,
                 m_ref, l_ref, acc_ref, sa_ref, sb_ref, ma_ref, mb_ref, bf_ref, *, tq, tk):
    n = pl.program_id(1)
    i = qi_ref[n]
    kt = kj_ref[n]
    last = (i * tq + tq + tk - 1) // tk - 1
    nstage = N_HEADS // _HP

    @pl.when(kt == 0)
    def _():
        m_ref[...] = jnp.full(m_ref.shape, NEG_BIAS, F32)
        l_ref[...] = jnp.zeros(l_ref.shape, F32)
        acc_ref[...] = jnp.zeros(acc_ref.shape, F32)

    bf_ref[...] = bias_ref[0, 0].astype(F32)

    def scores(j, buf, mx):
        for r in range(_HP):
            h = j * _HP + r
            sb = jnp.dot(k_ref[h // GROUP], qt_ref[h], preferred_element_type=F32) + bf_ref[...]
            buf[r] = sb
            mx[r] = jnp.max(sb, axis=0, keepdims=True)

    def softmax_pv(j, buf, mx):
        for r in range(_HP):
            h = j * _HP + r
            m_prev = m_ref[h]
            m_new = jnp.maximum(m_prev, mx[r])
            alpha = jnp.exp2(m_prev - m_new)
            p = jnp.exp2(buf[r] - m_new).astype(BF16)
            pv = jnp.dot(vt_ref[h // GROUP], p, preferred_element_type=F32)
            l_ref[h] = alpha * l_ref[h] + pv[HEAD_DIM:HEAD_DIM + 1]
            acc_ref[h] = alpha * acc_ref[h] + pv[:HEAD_DIM]
            m_ref[h] = m_new

    scores(0, sa_ref, ma_ref)

    def two_stages(jj, carry):
        j = 2 * jj
        scores(j + 1, sb_ref, mb_ref)
        softmax_pv(j, sa_ref, ma_ref)
        scores(j + 2, sa_ref, ma_ref)
        softmax_pv(j + 1, sb_ref, mb_ref)
        return carry

    lax.fori_loop(0, nstage // 2 - 1, two_stages, 0)
    scores(nstage - 1, sb_ref, mb_ref)
    softmax_pv(nstage - 2, sa_ref, ma_ref)
    softmax_pv(nstage - 1, sb_ref, mb_ref)

    @pl.when(kt == last)
    def _():
        for h in range(N_HEADS):
            o_ref[:, HEAD_DIM * h:HEAD_DIM * (h + 1)] = (acc_ref[h] / l_ref[h]).T.astype(BF16)


def _attention(qt, k, vt, bias, batch, seq, tq, tk):
    nq = seq // tq
    nk = seq // tk
    t = batch * seq
    pairs = [(i, j) for i in range(nq) for j in range((i * tq + tq + tk - 1) // tk)]
    qi = jnp.asarray([p[0] for p in pairs], I32)
    kj = jnp.asarray([p[1] for p in pairs], I32)
    kern = functools.partial(_attn_kernel, tq=tq, tk=tk)
    grid_spec = pltpu.PrefetchScalarGridSpec(
        num_scalar_prefetch=2,
        grid=(batch, len(pairs)),
        in_specs=[
            pl.BlockSpec((N_HEADS, HEAD_DIM, tq), lambda b, n, qi, kj: (0, 0, b * nq + qi[n])),
            pl.BlockSpec((N_KV_HEADS, tk, HEAD_DIM), lambda b, n, qi, kj: (0, b * nk + kj[n], 0)),
            pl.BlockSpec((N_KV_HEADS, V_ROWS, tk), lambda b, n, qi, kj: (0, 0, b * nk + kj[n])),
            pl.BlockSpec((1, 1, tk, tq), lambda b, n, qi, kj: (b, qi[n], kj[n], 0)),
        ],
        out_specs=pl.BlockSpec((tq, Q_DIM), lambda b, n, qi, kj: (b * nq + qi[n], 0)),
        scratch_shapes=[
            pltpu.VMEM((N_HEADS, 1, tq), F32),
            pltpu.VMEM((N_HEADS, 1, tq), F32),
            pltpu.VMEM((N_HEADS, HEAD_DIM, tq), F32),
            pltpu.VMEM((_HP, tk, tq), F32),
            pltpu.VMEM((_HP, tk, tq), F32),
            pltpu.VMEM((_HP, 1, tq), F32),
            pltpu.VMEM((_HP, 1, tq), F32),
            pltpu.VMEM((tk, tq), F32),
        ],
    )
    return pl.pallas_call(
        kern,
        grid_spec=grid_spec,
        out_shape=jax.ShapeDtypeStruct((t, Q_DIM), BF16),
        compiler_params=_params(("arbitrary", "arbitrary")),
        name="masked_attention",
    )(qi, kj, qt, k, vt, bias)


def _proj_residual_kernel(h_ref, x_ref, w_ref, *rest, has_bias, tn):
    if has_bias:
        b_ref, o_ref = rest
    else:
        (o_ref,) = rest
    x = x_ref[...]
    for c in range(w_ref.shape[1] // tn):
        cols = slice(c * tn, (c + 1) * tn)
        y = h_ref[:, cols] + jnp.dot(x, w_ref[:, cols], preferred_element_type=F32)
        if has_bias:
            y = y + b_ref[:, cols]
        o_ref[:, cols] = y


def _proj_residual(h, x, w, b, tm, name):
    t, d = h.shape
    kdim = x.shape[1]
    row = lambda i: (i, 0)
    in_specs = [pl.BlockSpec((tm, d), row), pl.BlockSpec((tm, kdim), row), _resident(w.shape)]
    args = [h, x, w]
    if b is not None:
        in_specs.append(_resident((1, d)))
        args.append(b)
    kern = functools.partial(_proj_residual_kernel, has_bias=b is not None, tn=512)
    return pl.pallas_call(
        kern,
        grid=(t // tm,),
        in_specs=in_specs,
        out_specs=pl.BlockSpec((tm, d), row),
        out_shape=jax.ShapeDtypeStruct((t, d), F32),
        compiler_params=_params(("arbitrary",)),
        name=name,
    )(*args)


def _mlp_kernel(h_ref, g_ref, w1_ref, w2_ref, o_ref, u_ref):
    @pl.when(pl.program_id(1) == 0)
    def _():
        h = h_ref[...]
        u_ref[...] = _rms(h, g_ref[...]).astype(BF16)
        o_ref[...] = h

    a = jnp.dot(u_ref[...], w1_ref[...], preferred_element_type=F32)
    a = jnp.square(jnp.maximum(a, 0.0)).astype(BF16)
    o_ref[...] += jnp.dot(a, w2_ref[...], preferred_element_type=F32)


def _mlp(h, g, w1, w2, tm, tf):
    t, d = h.shape
    f = w1.shape[1]
    return pl.pallas_call(
        _mlp_kernel,
        grid=(t // tm, f // tf),
        in_specs=[
            pl.BlockSpec((tm, d), lambda i, j: (i, 0)),
            _resident((1, d)),
            pl.BlockSpec((d, tf), lambda i, j: (0, j)),
            pl.BlockSpec((tf, d), lambda i, j: (j, 0)),
        ],
        out_specs=pl.BlockSpec((tm, d), lambda i, j: (i, 0)),
        out_shape=jax.ShapeDtypeStruct((t, d), F32),
        scratch_shapes=[pltpu.VMEM((tm, d), BF16)],
        compiler_params=_params(("arbitrary", "arbitrary")),
        name="sqrelu_mlp",
    )(h, g, w1, w2)


def _pe_kernel(h_ref, g_ref, wg_ref, p_ref, wp_ref, *rest, final, tn):
    if final:
        gf_ref, o_ref = rest
    else:
        (o_ref,) = rest
    h = h_ref[...]
    u = _rms(h, g_ref[...]).astype(BF16)
    pb = p_ref[...].astype(BF16)
    for c in range(wg_ref.shape[1] // tn):
        cols = slice(c * tn, (c + 1) * tn)
        gate = jax.nn.sigmoid(jnp.dot(u, wg_ref[:, cols], preferred_element_type=F32))
        emb = jnp.dot(pb, wp_ref[:, cols], preferred_element_type=F32)
        o_ref[:, cols] = h_ref[:, cols] + emb * gate
    if final:
        o_ref[...] = _rms(o_ref[...], gf_ref[...])


def _pe(h, g, wg, p, layer, wp, gf, tm):
    t, d = h.shape
    row = lambda i: (i, 0)
    in_specs = [pl.BlockSpec((tm, d), row), _resident((1, d)), _resident(wg.shape),
                pl.BlockSpec((None, tm, p.shape[2]), lambda r: (layer, r, 0)), _resident(wp.shape)]
    args = [h, g, wg, p, wp]
    if gf is not None:
        in_specs.append(_resident((1, d)))
        args.append(gf)
    kern = functools.partial(_pe_kernel, final=gf is not None, tn=512)
    return pl.pallas_call(
        kern,
        grid=(t // tm,),
        in_specs=in_specs,
        out_specs=pl.BlockSpec((tm, d), row),
        out_shape=jax.ShapeDtypeStruct((t, d), F32),
        compiler_params=_params(("arbitrary",)),
        name="gated_embedding",
    )(*args)


def _conv_in_kernel(h_ref, g_ref, w_ref, b_ref, y_ref, *, tn):
    d = y_ref.shape[1]
    u = _rms(h_ref[...], g_ref[...]).astype(BF16)
    for c in range(d // tn):
        a = jnp.dot(u, w_ref[:, c * tn:(c + 1) * tn], preferred_element_type=F32) + b_ref[:, c * tn:(c + 1) * tn]
        gt = (jnp.dot(u, w_ref[:, d + c * tn:d + (c + 1) * tn], preferred_element_type=F32)
              + b_ref[:, d + c * tn:d + (c + 1) * tn])
        y_ref[:, c * tn:(c + 1) * tn] = a * jax.nn.sigmoid(gt)


def _conv_in(h, g, w, b, tm):
    t, d = h.shape
    row = lambda i: (i, 0)
    return pl.pallas_call(
        functools.partial(_conv_in_kernel, tn=512),
        grid=(t // tm,),
        in_specs=[pl.BlockSpec((tm, d), row), _resident((1, d)), _resident(w.shape), _resident(b.shape)],
        out_specs=pl.BlockSpec((tm, d), row),
        out_shape=jax.ShapeDtypeStruct((t, d), F32),
        compiler_params=_params(("arbitrary",)),
        name="conv_in_glu",
    )(h, g, w, b)


_HALO = 32
_CONV_RS = 64


def _dwconv_kernel(y_ref, halo_ref, w_ref, bdw_ref, lg_ref, lb_ref, z_ref, ext_ref, cv_ref, *, tm):
    d = y_ref.shape[2]
    nc = d // LANES
    first = pl.program_id(1) == 0
    for c in range(nc):
        cols = slice(c * LANES, (c + 1) * LANES)
        ext_ref[c, 0:_HALO] = jnp.where(first, 0.0, halo_ref[0, :, cols])
        ext_ref[c, _HALO:_HALO + tm] = y_ref[0, :, cols]

    off = _HALO - (CONV_WIDTH - 1)

    def chunk(c, carry):
        for rs in range(tm // _CONV_RS):
            r0 = rs * _CONV_RS
            acc = jnp.broadcast_to(bdw_ref[c], (_CONV_RS, LANES))
            for j in range(CONV_WIDTH):
                acc = acc + w_ref[c, j:j + 1, :] * ext_ref[c, r0 + off + j:r0 + off + j + _CONV_RS, :]
            cv_ref[c, r0:r0 + _CONV_RS] = acc
        return carry

    lax.fori_loop(0, nc, chunk, 0)

    s1 = cv_ref[0]
    for c in range(1, nc):
        s1 = s1 + cv_ref[c]
    mu = jnp.sum(s1, axis=1, keepdims=True) / d
    s2 = jnp.square(cv_ref[0] - mu)
    for c in range(1, nc):
        s2 = s2 + jnp.square(cv_ref[c] - mu)
    rstd = lax.rsqrt(jnp.sum(s2, axis=1, keepdims=True) / d + NORM_EPS)
    for c in range(nc):
        yn = (cv_ref[c] - mu) * rstd * lg_ref[c] + lb_ref[c]
        z_ref[0, :, c * LANES:(c + 1) * LANES] = (yn * jax.nn.sigmoid(yn)).astype(BF16)


def _dwconv_ln_swish(y3, w3, bdw3, lg3, lb3, tm):
    batch, seq, d = y3.shape
    nc = d // LANES
    hb = tm // _HALO
    kern = functools.partial(_dwconv_kernel, tm=tm)
    return pl.pallas_call(
        kern,
        grid=(batch, seq // tm),
        in_specs=[
            pl.BlockSpec((1, tm, d), lambda b, i: (b, i, 0)),
            pl.BlockSpec((1, _HALO, d), lambda b, i: (b, jnp.maximum(i * hb - 1, 0), 0)),
            _resident(w3.shape), _resident(bdw3.shape), _resident(lg3.shape), _resident(lb3.shape),
        ],
        out_specs=pl.BlockSpec((1, tm, d), lambda b, i: (b, i, 0)),
        out_shape=jax.ShapeDtypeStruct((batch, seq, d), BF16),
        scratch_shapes=[
            pltpu.VMEM((nc, _HALO + tm, LANES), F32),
            pltpu.VMEM((nc, tm, LANES), F32),
        ],
        compiler_params=_params(("arbitrary", "arbitrary")),
        name="dwconv_ln_swish",
    )(y3, y3, w3, bdw3, lg3, lb3)


def _rope_tables(seq):
    rd = HEAD_DIM // ROPE_FRAC
    half = rd // 2
    pos = jnp.arange(seq, dtype=F32)
    inv = ROPE_THETA ** (-jnp.arange(half, dtype=F32) * 2.0 / rd)
    ang = pos[:, None] * inv[None, :]
    cos = jnp.cos(ang)
    sin = jnp.sin(ang)
    ones = jnp.ones((seq, HEAD_DIM - rd), F32)
    zeros = jnp.zeros((seq, HEAD_DIM - rd), F32)
    zh = jnp.zeros((seq, half), F32)
    cos_t = jnp.concatenate([cos, cos, ones], axis=1)
    sa_t = jnp.concatenate([-sin, zh, zeros], axis=1)
    sb_t = jnp.concatenate([zh, sin, zeros], axis=1)
    return cos_t, sa_t, sb_t


def _chunked(v, nc):
    v2 = v.reshape(-1, v.shape[-1])
    return v2.reshape(v2.shape[0], nc, LANES).transpose(1, 0, 2)


def kernel(x, p, mix_norm, mlp_norm, mlp_w1, mlp_w2, pe_proj, pe_gate_norm, pe_gate, dsa_w_in, dsa_w_out, conv_w_in, conv_b_in, conv_w_dw, conv_b_dw, conv_ln_g, conv_ln_b, conv_w_out, conv_b_out, final_norm):
    batch, seq, d = x.shape
    depth = p.shape[0]
    t = batch * seq
    topk = min(TOPK_MAX, seq // KEY_FRAC)
    tm = 512
    tm_big = min(1024, t)
    h = x.reshape(t, d)
    row_vec = lambda v: v.reshape(1, -1)

    for i in range(depth):
        j = i // 2
        if i % 2 == 0:
            w = dsa_w_in[j]
            wq, wk, wv, wqi, wki, wwi = jnp.split(
                w, [Q_DIM, Q_DIM + KV_DIM, Q_DIM + 2 * KV_DIM, Q_DIM + 2 * KV_DIM + IDXQ_DIM,
                    Q_DIM + 2 * KV_DIM + IDXQ_DIM + IDX_DIM], axis=1)
            wwi = jnp.pad(wwi, ((0, 0), (0, LANES - N_IDX_HEADS)))
            w_all = jnp.concatenate([wq, wk, wqi, wv, wki, wwi], axis=1).astype(BF16)
            cos_t, sa_t, sb_t = _rope_tables(seq)
            qt, k, vt, qit, ki, wit = _dsa_proj(h, row_vec(mix_norm[i]), w_all, cos_t, sa_t, sb_t, seq, tm)
            tq_attn = min(512, seq)
            bias = _topk_mask(qit, ki, wit, batch, seq, topk, tq=256, tk=512, tq_attn=tq_attn)
            o = _attention(qt, k, vt, bias, batch, seq, tq=tq_attn, tk=min(1024, seq))
            h = _proj_residual(h, o, dsa_w_out[j].astype(BF16), None, tm_big, "dsa_out")
        else:
            nc = d // LANES
            y = _conv_in(h, row_vec(mix_norm[i]), conv_w_in[j].astype(BF16), row_vec(conv_b_in[j]), tm_big)
            w3 = _chunked(jnp.pad(conv_w_dw[j], ((0, _HALO - CONV_WIDTH), (0, 0))), nc)
            z = _dwconv_ln_swish(y.reshape(batch, seq, d), w3, _chunked(conv_b_dw[j], nc),
                                 _chunked(conv_ln_g[j], nc), _chunked(conv_ln_b[j], nc), tm=512)
            h = _proj_residual(h, z.reshape(t, d), conv_w_out[j].astype(BF16), row_vec(conv_b_out[j]), tm_big,
                               "conv_out")
        h = _mlp(h, row_vec(mlp_norm[i]), mlp_w1[i].astype(BF16), mlp_w2[i].astype(BF16), tm=512, tf=2048)
        gf = row_vec(final_norm) if i == depth - 1 else None
        h = _pe(h, row_vec(pe_gate_norm[i]), pe_gate[i].astype(BF16), p.reshape(depth, t, -1), i,
                pe_proj[i].astype(BF16), gf, tm)
    return h.reshape(batch, seq, d)
```

```python
import functools
import math

import jax
import jax.numpy as jnp
from jax import lax
from jax.experimental import pallas as pl
from jax.experimental.pallas import tpu as pltpu

F32 = jnp.float32
BF16 = jnp.bfloat16
I32 = jnp.int32

CHUNK = 64
N_HEADS = 16
HEAD_DIM = 128
N_KV_HEADS = 4
GROUP = N_HEADS // N_KV_HEADS
N_IDX_HEADS = 8
IDX_DIM = 128
TOPK_MAX = 256
KEY_FRAC = 4
ROPE_THETA = 500000.0
ROPE_FRAC = 4
CONV_WIDTH = 31
NORM_EPS = 1e-6
Q_DIM = N_HEADS * HEAD_DIM
KV_DIM = N_KV_HEADS * HEAD_DIM
IDXQ_DIM = N_IDX_HEADS * IDX_DIM

LANES = 128
SUBLANES = 8
PACKED_SUBLANES = 16
V_ROWS = HEAD_DIM + PACKED_SUBLANES
F32_MIN_NORMAL = 1.1754943508222875e-38
BF16_MIN_NORMAL_BITS = 0x0080
INT_MIN = -(2 ** 31)
NEG_BIAS = -1e30
LOG2E = math.log2(math.e)

VMEM_LIMIT = 56 * 1024 * 1024


def _params(sem, vmem=VMEM_LIMIT):
    return pltpu.CompilerParams(dimension_semantics=sem, vmem_limit_bytes=vmem)


def _resident(shape):
    nd = len(shape)
    return pl.BlockSpec(shape, lambda *_: (0,) * nd, pipeline_mode=pl.Buffered(1))


def _rms(x, g):
    ms = jnp.mean(x * x, axis=-1, keepdims=True)
    return x * lax.rsqrt(ms + NORM_EPS) * g


_C_Q = 0
_C_K = _C_Q + Q_DIM
_C_QI = _C_K + KV_DIM
_C_V = _C_QI + IDXQ_DIM
_C_KI = _C_V + KV_DIM
_C_WI = _C_KI + IDX_DIM
_C_END = _C_WI + LANES


def _dsa_proj_kernel(h_ref, g_ref, w_ref, cos_ref, sa_ref, sb_ref,
                     qt_ref, k_ref, vt_ref, qit_ref, ki_ref, wit_ref):
    u = _rms(h_ref[...], g_ref[...]).astype(BF16)
    cos = cos_ref[...]
    sa = sa_ref[...]
    sb = sb_ref[...]
    rd = HEAD_DIM // ROPE_FRAC
    half = rd // 2

    def rope(y):
        return (y * cos + pltpu.roll(y, LANES - half, 1) * sa + pltpu.roll(y, half, 1) * sb)

    def proj(c0, n):
        return jnp.dot(u, w_ref[:, c0:c0 + n], preferred_element_type=F32)

    qscale = (HEAD_DIM ** -0.5) * LOG2E
    for c in range(N_HEADS // 4):
        y = proj(_C_Q + 512 * c, 512)
        for j in range(4):
            qt_ref[4 * c + j] = (rope(y[:, 128 * j:128 * (j + 1)]) * qscale).T.astype(BF16)
    y = proj(_C_K, KV_DIM)
    for j in range(N_KV_HEADS):
        k_ref[j] = rope(y[:, 128 * j:128 * (j + 1)]).astype(BF16)
    for c in range(N_IDX_HEADS // 4):
        y = proj(_C_QI + 512 * c, 512)
        for j in range(4):
            qit_ref[4 * c + j] = (rope(y[:, 128 * j:128 * (j + 1)]) * (IDX_DIM ** -0.5)).T.astype(BF16)
    y = proj(_C_KI, IDX_DIM + LANES)
    ki_ref[...] = rope(y[:, :IDX_DIM]).astype(BF16)
    wit_ref[...] = (y[:, IDX_DIM:] * (N_IDX_HEADS ** -0.5)).T[:N_IDX_HEADS]
    y = proj(_C_V, KV_DIM)
    for j in range(N_KV_HEADS):
        vt_ref[j, 0:HEAD_DIM, :] = y[:, 128 * j:128 * (j + 1)].T.astype(BF16)
        vt_ref[j, HEAD_DIM:V_ROWS, :] = jnp.ones((V_ROWS - HEAD_DIM, vt_ref.shape[2]), BF16)


def _dsa_proj(h, g, w_all, cos, sa, sb, seq, tm):
    t, d = h.shape
    nseq = seq // tm
    row = lambda i: (i, 0)
    tab = pl.BlockSpec((tm, LANES), lambda i: (i % nseq, 0))
    return pl.pallas_call(
        _dsa_proj_kernel,
        grid=(t // tm,),
        in_specs=[pl.BlockSpec((tm, d), row), _resident((1, d)), _resident(w_all.shape), tab, tab, tab],
        out_specs=[
            pl.BlockSpec((N_HEADS, HEAD_DIM, tm), lambda i: (0, 0, i)),
            pl.BlockSpec((N_KV_HEADS, tm, HEAD_DIM), lambda i: (0, i, 0)),
            pl.BlockSpec((N_KV_HEADS, V_ROWS, tm), lambda i: (0, 0, i)),
            pl.BlockSpec((N_IDX_HEADS, IDX_DIM, tm), lambda i: (0, 0, i)),
            pl.BlockSpec((tm, IDX_DIM), row),
            pl.BlockSpec((N_IDX_HEADS, tm), lambda i: (0, i)),
        ],
        out_shape=[
            jax.ShapeDtypeStruct((N_HEADS, HEAD_DIM, t), BF16),
            jax.ShapeDtypeStruct((N_KV_HEADS, t, HEAD_DIM), BF16),
            jax.ShapeDtypeStruct((N_KV_HEADS, V_ROWS, t), BF16),
            jax.ShapeDtypeStruct((N_IDX_HEADS, IDX_DIM, t), BF16),
            jax.ShapeDtypeStruct((t, IDX_DIM), BF16),
            jax.ShapeDtypeStruct((N_IDX_HEADS, t), F32),
        ],
        compiler_params=_params(("arbitrary",)),
        name="dsa_proj",
    )(h, g, w_all, cos, sa, sb)


_KT = 256
_LO_FIXED = 6


def _fold(y):
    n = y.shape[0]
    parts = [y[j] for j in range(min(4, n))]
    for j in range(4, n):
        parts[j % 4] = parts[j % 4] + y[j]
    while len(parts) > 1:
        parts = [parts[j] + parts[j + 1] if j + 1 < len(parts) else parts[j] for j in range(0, len(parts), 2)]
    return parts[0]


def _topk_mask_kernel(qit_ref, ki_ref, wit_ref, bias_ref, sc_ref, hi_ref, *, tq, tk, topk, seq):
    i = pl.program_id(1)
    q0 = i * tq
    n_adm = q0 + tq
    nkt = (n_adm + tk - 1) // tk
    nct = n_adm // _KT
    kf = float(topk)
    lane = lax.broadcasted_iota(I32, (1, tq), 1)
    adm_end = q0 + (lane // CHUNK + 1) * CHUNK

    def score_tile(kt, masked):
        k0 = pl.multiple_of(kt * tk, tk)
        kb = ki_ref[pl.ds(k0, tk), :]
        acc = None
        for h in range(N_IDX_HEADS):
            r = jnp.dot(kb, qit_ref[h], preferred_element_type=F32)
            term = wit_ref[h:h + 1, :] * jnp.maximum(r, 0.0)
            acc = term if acc is None else acc + term
        acc = jnp.where(jnp.abs(acc) < F32_MIN_NORMAL, 0.0, acc)
        bits = pltpu.bitcast(acc, I32)
        key = bits ^ ((bits >> 31) & 0x7FFFFFFF)
        top = pltpu.bitcast(bits & jnp.int32(-65536), F32)
        if masked:
            adm = (k0 + lax.broadcasted_iota(I32, (tk, tq), 0)) < adm_end
            key = jnp.where(adm, key, INT_MIN)
            top = jnp.where(adm, top, -jnp.inf)
        sc_ref[pl.ds(k0, tk), :] = key
        hi_ref[pl.ds(k0, tk), :] = top.astype(BF16)

    def interior(kt, carry):
        score_tile(kt, False)
        return carry

    def boundary(kt, carry):
        score_tile(kt, True)
        return carry

    n_int = q0 // tk
    lax.fori_loop(0, n_int, interior, 0)
    lax.fori_loop(n_int, nkt, boundary, 0)

    row_kt = lax.broadcasted_iota(I32, (_KT, tq), 0)

    def count(pred):
        def body(c, acc):
            r0 = pl.multiple_of(c * _KT, _KT)
            key = sc_ref[pl.ds(r0, _KT), :]
            ones = jnp.where(pred(key, r0 + row_kt), 1.0, 0.0)
            return acc + _fold(ones.reshape(_KT // SUBLANES, SUBLANES, tq))
        acc = lax.fori_loop(0, nct, body, jnp.zeros((SUBLANES, tq), F32))
        return jnp.sum(acc, axis=0, keepdims=True)

    def count_hi(cand):
        assert tk // PACKED_SUBLANES < 256
        def body(c, acc):
            r0 = pl.multiple_of(c * tk, tk)
            ones = jnp.where(hi_ref[pl.ds(r0, tk), :] >= cand, jnp.ones((), BF16), jnp.zeros((), BF16))
            part = _fold(ones.reshape(tk // PACKED_SUBLANES, PACKED_SUBLANES, tq))
            return acc + part.astype(F32)
        acc = lax.fori_loop(0, nkt, body, jnp.zeros((PACKED_SUBLANES, tq), F32))
        return jnp.sum(acc, axis=0, keepdims=True)

    def emit(keep):
        def body(c, carry):
            r0 = pl.multiple_of(c * _KT, _KT)
            key = sc_ref[pl.ds(r0, _KT), :]
            bias_ref[0, 0, pl.ds(r0, _KT), :] = jnp.where(keep(key, r0 + row_kt), 0.0, NEG_BIAS).astype(BF16)
            return carry
        lax.fori_loop(0, nct, body, 0)

    def hi_pass(p, state):
        u, cnt_u = state
        cand_u = u | (jnp.int32(1) << (15 - p))
        c16 = cand_u - 32768
        c16 = jnp.where(jnp.logical_and(c16 > 0, c16 < BF16_MIN_NORMAL_BITS), BF16_MIN_NORMAL_BITS, c16)
        raw = jnp.where(c16 >= 0, c16, c16 ^ 0x7FFF) << 16
        cnt = count_hi(pltpu.bitcast(raw, F32).astype(BF16))
        ok = cnt >= kf
        return jnp.where(ok, cand_u, u), jnp.where(ok, cnt, cnt_u)

    many = jnp.full((1, tq), float(seq + 1), F32)
    u, cnt_u = lax.fori_loop(0, 16, hi_pass, (jnp.zeros((1, tq), I32), many))

    few = adm_end <= topk

    def unsettled(cnt):
        return jnp.max(jnp.where(jnp.logical_or(few, cnt == kf), 0.0, 1.0)) > 0.0

    def lo_cond(state):
        p, _, _, go = state
        return jnp.logical_and(p < 16, go)

    def two_passes(p, tau, cnt_t):
        for b in range(2):
            cand = tau | (jnp.int32(1) << (15 - p - b))
            cnt = count(lambda key, kidx: key >= cand)
            ok = cnt >= kf
            cnt_t = jnp.where(ok, cnt, cnt_t)
            tau = jnp.where(ok, cand, tau)
        return tau, cnt_t

    def lo_fixed(jj, state):
        return two_passes(2 * jj, *state)

    def lo_pass(state):
        p, tau, cnt_t, _ = state
        tau, cnt_t = two_passes(p, tau, cnt_t)
        return p + 2, tau, cnt_t, unsettled(cnt_t)

    tau, cnt_t = lax.fori_loop(0, _LO_FIXED // 2, lo_fixed, ((u - 32768) << 16, cnt_u))
    _, tau, cnt_ge, _ = lax.while_loop(lo_cond, lo_pass, (jnp.int32(_LO_FIXED), tau, cnt_t, unsettled(cnt_t)))
    tau = jnp.maximum(tau, INT_MIN + 1)
    has_tie = jnp.max(jnp.where(few, 0.0, cnt_ge)) > kf

    def fill(c, carry):
        r0 = pl.multiple_of(c * _KT, _KT)
        bias_ref[0, 0, pl.ds(r0, _KT), :] = jnp.full((_KT, tq), NEG_BIAS, BF16)
        return carry

    lax.fori_loop(nct, seq // _KT, fill, 0)

    @pl.when(jnp.logical_not(has_tie))
    def _():
        emit(lambda key, kidx: key >= tau)

    @pl.when(has_tie)
    def _():
        tied = jnp.logical_and(jnp.logical_not(few), cnt_ge > kf)
        need = jnp.where(tied, kf - count(lambda key, kidx: key > tau), 0.0)
        max_need = jnp.max(need).astype(I32)
        nbits = max(1, (seq - 1).bit_length())

        def next_tied(after):
            def body(c, acc):
                r0 = pl.multiple_of(c * _KT, _KT)
                key = sc_ref[pl.ds(r0, _KT), :]
                kidx = r0 + row_kt
                v = jnp.where(jnp.logical_and(key == tau, kidx > after), kidx, seq)
                v = v.reshape(_KT // SUBLANES, SUBLANES, tq)
                for j in range(_KT // SUBLANES):
                    acc = jnp.minimum(acc, v[j])
                return acc
            acc = lax.fori_loop(0, nct, body, jnp.full((SUBLANES, tq), seq, I32))
            return jnp.min(acc.astype(F32), axis=0, keepdims=True).astype(I32)

        def by_scan():
            def step(r, j0):
                return jnp.where(r.astype(F32) < need, next_tied(j0), j0)
            return lax.fori_loop(0, max_need, step, jnp.full((1, tq), -1, I32))

        def by_bisect():
            def idx_pass(p, j0):
                cand = j0 + (jnp.int32(1) << (nbits - 1 - p))
                c_lt = count(lambda key, kidx: jnp.logical_and(key == tau, kidx < cand))
                return jnp.where(c_lt < need, cand, j0)
            return lax.fori_loop(0, nbits, idx_pass, jnp.zeros((1, tq), I32))

        j0 = lax.cond(max_need <= nbits, by_scan, by_bisect)
        j0 = jnp.where(tied, j0, seq)
        emit(lambda key, kidx: jnp.logical_or(key > tau, jnp.logical_and(key == tau, kidx <= j0)))


def _topk_mask(qit, ki, wit, batch, seq, topk, tq, tk, tq_attn):
    assert tq % _KT == 0 and tq_attn % tq == 0 and seq % tk == 0
    nq = seq // tq
    per = tq_attn // tq
    kern = functools.partial(_topk_mask_kernel, tq=tq, tk=tk, topk=topk, seq=seq)
    return pl.pallas_call(
        kern,
        grid=(batch, nq),
        in_specs=[
            pl.BlockSpec((N_IDX_HEADS, IDX_DIM, tq), lambda b, i: (0, 0, b * nq + i)),
            pl.BlockSpec((seq, IDX_DIM), lambda b, i: (b, 0)),
            pl.BlockSpec((N_IDX_HEADS, tq), lambda b, i: (0, b * nq + i)),
        ],
        out_specs=pl.BlockSpec((1, 1, seq, tq), lambda b, i: (b, i // per, 0, i % per)),
        out_shape=jax.ShapeDtypeStruct((batch, seq // tq_attn, seq, tq_attn), BF16),
        scratch_shapes=[pltpu.VMEM((seq, tq), I32), pltpu.VMEM((seq, tq), BF16)],
        compiler_params=_params(("arbitrary", "arbitrary")),
        name="topk_mask",
    )(qit, ki, wit)


_HP = 2


def _attn_kernel(qi_ref, kj_ref, qt_ref, k_ref, vt_ref, bias_ref, o_ref,
                 m_ref, l_ref, acc_ref, sa_ref, sb_ref, ma_ref, mb_ref, bf_ref, *, tq, tk):
    n = pl.program_id(1)
    i = qi_ref[n]
    kt = kj_ref[n]
    last = (i * tq + tq + tk - 1) // tk - 1
    nstage = N_HEADS // _HP

    @pl.when(kt == 0)
    def _():
        m_ref[...] = jnp.full(m_ref.shape, NEG_BIAS, F32)
        l_ref[...] = jnp.zeros(l_ref.shape, F32)
        acc_ref[...] = jnp.zeros(acc_ref.shape, F32)

    bf_ref[...] = bias_ref[0, 0].astype(F32)

    def scores(j, buf, mx):
        for r in range(_HP):
            h = j * _HP + r
            sb = jnp.dot(k_ref[h // GROUP], qt_ref[h], preferred_element_type=F32) + bf_ref[...]
            buf[r] = sb
            mx[r] = jnp.max(sb, axis=0, keepdims=True)

    def softmax_pv(j, buf, mx):
        for r in range(_HP):
            h = j * _HP + r
            m_prev = m_ref[h]
            m_new = jnp.maximum(m_prev, mx[r])
            alpha = jnp.exp2(m_prev - m_new)
            p = jnp.exp2(buf[r] - m_new).astype(BF16)
            pv = jnp.dot(vt_ref[h // GROUP], p, preferred_element_type=F32)
            l_ref[h] = alpha * l_ref[h] + pv[HEAD_DIM:HEAD_DIM + 1]
            acc_ref[h] = alpha * acc_ref[h] + pv[:HEAD_DIM]
            m_ref[h] = m_new

    scores(0, sa_ref, ma_ref)

    def two_stages(jj, carry):
        j = 2 * jj
        scores(j + 1, sb_ref, mb_ref)
        softmax_pv(j, sa_ref, ma_ref)
        scores(j + 2, sa_ref, ma_ref)
        softmax_pv(j + 1, sb_ref, mb_ref)
        return carry

    lax.fori_loop(0, nstage // 2 - 1, two_stages, 0)
    scores(nstage - 1, sb_ref, mb_ref)
    softmax_pv(nstage - 2, sa_ref, ma_ref)
    softmax_pv(nstage - 1, sb_ref, mb_ref)

    @pl.when(kt == last)
    def _():
        for h in range(N_HEADS):
            o_ref[:, HEAD_DIM * h:HEAD_DIM * (h + 1)] = (acc_ref[h] / l_ref[h]).T.astype(BF16)


def _attention(qt, k, vt, bias, batch, seq, tq, tk):
    nq = seq // tq
    nk = seq // tk
    t = batch * seq
    pairs = [(i, j) for i in range(nq) for j in range((i * tq + tq + tk - 1) // tk)]
    qi = jnp.asarray([p[0] for p in pairs], I32)
    kj = jnp.asarray([p[1] for p in pairs], I32)
    kern = functools.partial(_attn_kernel, tq=tq, tk=tk)
    grid_spec = pltpu.PrefetchScalarGridSpec(
        num_scalar_prefetch=2,
        grid=(batch, len(pairs)),
        in_specs=[
            pl.BlockSpec((N_HEADS, HEAD_DIM, tq), lambda b, n, qi, kj: (0, 0, b * nq + qi[n])),
            pl.BlockSpec((N_KV_HEADS, tk, HEAD_DIM), lambda b, n, qi, kj: (0, b * nk + kj[n], 0)),
            pl.BlockSpec((N_KV_HEADS, V_ROWS, tk), lambda b, n, qi, kj: (0, 0, b * nk + kj[n])),
            pl.BlockSpec((1, 1, tk, tq), lambda b, n, qi, kj: (b, qi[n], kj[n], 0)),
        ],
        out_specs=pl.BlockSpec((tq, Q_DIM), lambda b, n, qi, kj: (b * nq + qi[n], 0)),
        scratch_shapes=[
            pltpu.VMEM((N_HEADS, 1, tq), F32),
            pltpu.VMEM((N_HEADS, 1, tq), F32),
            pltpu.VMEM((N_HEADS, HEAD_DIM, tq), F32),
            pltpu.VMEM((_HP, tk, tq), F32),
            pltpu.VMEM((_HP, tk, tq), F32),
            pltpu.VMEM((_HP, 1, tq), F32),
            pltpu.VMEM((_HP, 1, tq), F32),
            pltpu.VMEM((tk, tq), F32),
        ],
    )
    return pl.pallas_call(
        kern,
        grid_spec=grid_spec,
        out_shape=jax.ShapeDtypeStruct((t, Q_DIM), BF16),
        compiler_params=_params(("arbitrary", "arbitrary")),
        name="masked_attention",
    )(qi, kj, qt, k, vt, bias)


def _proj_residual_kernel(h_ref, x_ref, w_ref, *rest, has_bias, tn):
    if has_bias:
        b_ref, o_ref = rest
    else:
        (o_ref,) = rest
    x = x_ref[...]
    for c in range(w_ref.shape[1] // tn):
        cols = slice(c * tn, (c + 1) * tn)
        y = h_ref[:, cols] + jnp.dot(x, w_ref[:, cols], preferred_element_type=F32)
        if has_bias:
            y = y + b_ref[:, cols]
        o_ref[:, cols] = y


def _proj_residual(h, x, w, b, tm, name):
    t, d = h.shape
    kdim = x.shape[1]
    row = lambda i: (i, 0)
    in_specs = [pl.BlockSpec((tm, d), row), pl.BlockSpec((tm, kdim), row), _resident(w.shape)]
    args = [h, x, w]
    if b is not None:
        in_specs.append(_resident((1, d)))
        args.append(b)
    kern = functools.partial(_proj_residual_kernel, has_bias=b is not None, tn=512)
    return pl.pallas_call(
        kern,
        grid=(t // tm,),
        in_specs=in_specs,
        out_specs=pl.BlockSpec((tm, d), row),
        out_shape=jax.ShapeDtypeStruct((t, d), F32),
        compiler_params=_params(("arbitrary",)),
        name=name,
    )(*args)


def _mlp_kernel(h_ref, g_ref, w1_ref, w2_ref, o_ref, u_ref):
    @pl.when(pl.program_id(1) == 0)
    def _():
        h = h_ref[...]
        u_ref[...] = _rms(h, g_ref[...]).astype(BF16)
        o_ref[...] = h

    a = jnp.dot(u_ref[...], w1_ref[...], preferred_element_type=F32)
    a = jnp.square(jnp.maximum(a, 0.0)).astype(BF16)
    o_ref[...] += jnp.dot(a, w2_ref[...], preferred_element_type=F32)


def _mlp(h, g, w1, w2, layer, tm, tf):
    t, d = h.shape
    f = w1.shape[2]
    return pl.pallas_call(
        _mlp_kernel,
        grid=(t // tm, f // tf),
        in_specs=[
            pl.BlockSpec((tm, d), lambda i, j: (i, 0)),
            _resident((1, d)),
            pl.BlockSpec((None, d, tf), lambda i, j: (layer, 0, j)),
            pl.BlockSpec((None, tf, d), lambda i, j: (layer, j, 0)),
        ],
        out_specs=pl.BlockSpec((tm, d), lambda i, j: (i, 0)),
        out_shape=jax.ShapeDtypeStruct((t, d), F32),
        scratch_shapes=[pltpu.VMEM((tm, d), BF16)],
        compiler_params=_params(("arbitrary", "arbitrary")),
        name="sqrelu_mlp",
    )(h, g, w1, w2)


def _pe_kernel(h_ref, g_ref, wg_ref, p_ref, wp_ref, *rest, final, tn):
    if final:
        gf_ref, o_ref = rest
    else:
        (o_ref,) = rest
    h = h_ref[...]
    u = _rms(h, g_ref[...]).astype(BF16)
    pb = p_ref[...].astype(BF16)
    for c in range(wg_ref.shape[1] // tn):
        cols = slice(c * tn, (c + 1) * tn)
        gate = jax.nn.sigmoid(jnp.dot(u, wg_ref[:, cols], preferred_element_type=F32))
        emb = jnp.dot(pb, wp_ref[:, cols], preferred_element_type=F32)
        o_ref[:, cols] = h_ref[:, cols] + emb * gate
    if final:
        o_ref[...] = _rms(o_ref[...], gf_ref[...])


def _pe(h, g, wg, p, layer, wp, gf, tm):
    t, d = h.shape
    row = lambda i: (i, 0)
    in_specs = [pl.BlockSpec((tm, d), row), _resident((1, d)), _resident(wg.shape),
                pl.BlockSpec((None, tm, p.shape[2]), lambda r: (layer, r, 0)), _resident(wp.shape)]
    args = [h, g, wg, p, wp]
    if gf is not None:
        in_specs.append(_resident((1, d)))
        args.append(gf)
    kern = functools.partial(_pe_kernel, final=gf is not None, tn=512)
    return pl.pallas_call(
        kern,
        grid=(t // tm,),
        in_specs=in_specs,
        out_specs=pl.BlockSpec((tm, d), row),
        out_shape=jax.ShapeDtypeStruct((t, d), F32),
        compiler_params=_params(("arbitrary",)),
        name="gated_embedding",
    )(*args)


def _conv_in_kernel(h_ref, g_ref, w_ref, b_ref, y_ref, *, tn):
    d = y_ref.shape[1]
    u = _rms(h_ref[...], g_ref[...]).astype(BF16)
    for c in range(d // tn):
        a = jnp.dot(u, w_ref[:, c * tn:(c + 1) * tn], preferred_element_type=F32) + b_ref[:, c * tn:(c + 1) * tn]
        gt = (jnp.dot(u, w_ref[:, d + c * tn:d + (c + 1) * tn], preferred_element_type=F32)
              + b_ref[:, d + c * tn:d + (c + 1) * tn])
        y_ref[:, c * tn:(c + 1) * tn] = a * jax.nn.sigmoid(gt)


def _conv_in(h, g, w, b, tm):
    t, d = h.shape
    row = lambda i: (i, 0)
    return pl.pallas_call(
        functools.partial(_conv_in_kernel, tn=512),
        grid=(t // tm,),
        in_specs=[pl.BlockSpec((tm, d), row), _resident((1, d)), _resident(w.shape), _resident(b.shape)],
        out_specs=pl.BlockSpec((tm, d), row),
        out_shape=jax.ShapeDtypeStruct((t, d), F32),
        compiler_params=_params(("arbitrary",)),
        name="conv_in_glu",
    )(h, g, w, b)


_HALO = 32
_CONV_RS = 64


def _dwconv_kernel(y_ref, halo_ref, w_ref, bdw_ref, lg_ref, lb_ref, z_ref, ext_ref, cv_ref, *, tm):
    d = y_ref.shape[2]
    nc = d // LANES
    first = pl.program_id(1) == 0
    for c in range(nc):
        cols = slice(c * LANES, (c + 1) * LANES)
        ext_ref[c, 0:_HALO] = jnp.where(first, 0.0, halo_ref[0, :, cols])
        ext_ref[c, _HALO:_HALO + tm] = y_ref[0, :, cols]

    off = _HALO - (CONV_WIDTH - 1)

    def chunk(c, carry):
        for rs in range(tm // _CONV_RS):
            r0 = rs * _CONV_RS
            acc = jnp.broadcast_to(bdw_ref[c], (_CONV_RS, LANES))
            for j in range(CONV_WIDTH):
                acc = acc + w_ref[c, j:j + 1, :] * ext_ref[c, r0 + off + j:r0 + off + j + _CONV_RS, :]
            cv_ref[c, r0:r0 + _CONV_RS] = acc
        return carry

    lax.fori_loop(0, nc, chunk, 0)

    s1 = cv_ref[0]
    for c in range(1, nc):
        s1 = s1 + cv_ref[c]
    mu = jnp.sum(s1, axis=1, keepdims=True) / d
    s2 = jnp.square(cv_ref[0] - mu)
    for c in range(1, nc):
        s2 = s2 + jnp.square(cv_ref[c] - mu)
    rstd = lax.rsqrt(jnp.sum(s2, axis=1, keepdims=True) / d + NORM_EPS)
    for c in range(nc):
        yn = (cv_ref[c] - mu) * rstd * lg_ref[c] + lb_ref[c]
        z_ref[0, :, c * LANES:(c + 1) * LANES] = (yn * jax.nn.sigmoid(yn)).astype(BF16)


def _dwconv_ln_swish(y3, w3, bdw3, lg3, lb3, tm):
    batch, seq, d = y3.shape
    nc = d // LANES
    hb = tm // _HALO
    kern = functools.partial(_dwconv_kernel, tm=tm)
    return pl.pallas_call(
        kern,
        grid=(batch, seq // tm),
        in_specs=[
            pl.BlockSpec((1, tm, d), lambda b, i: (b, i, 0)),
            pl.BlockSpec((1, _HALO, d), lambda b, i: (b, jnp.maximum(i * hb - 1, 0), 0)),
            _resident(w3.shape), _resident(bdw3.shape), _resident(lg3.shape), _resident(lb3.shape),
        ],
        out_specs=pl.BlockSpec((1, tm, d), lambda b, i: (b, i, 0)),
        out_shape=jax.ShapeDtypeStruct((batch, seq, d), BF16),
        scratch_shapes=[
            pltpu.VMEM((nc, _HALO + tm, LANES), F32),
            pltpu.VMEM((nc, tm, LANES), F32),
        ],
        compiler_params=_params(("arbitrary", "arbitrary")),
        name="dwconv_ln_swish",
    )(y3, y3, w3, bdw3, lg3, lb3)


def _rope_tables(seq):
    rd = HEAD_DIM // ROPE_FRAC
    half = rd // 2
    pos = jnp.arange(seq, dtype=F32)
    inv = ROPE_THETA ** (-jnp.arange(half, dtype=F32) * 2.0 / rd)
    ang = pos[:, None] * inv[None, :]
    cos = jnp.cos(ang)
    sin = jnp.sin(ang)
    ones = jnp.ones((seq, HEAD_DIM - rd), F32)
    zeros = jnp.zeros((seq, HEAD_DIM - rd), F32)
    zh = jnp.zeros((seq, half), F32)
    cos_t = jnp.concatenate([cos, cos, ones], axis=1)
    sa_t = jnp.concatenate([-sin, zh, zeros], axis=1)
    sb_t = jnp.concatenate([zh, sin, zeros], axis=1)
    return cos_t, sa_t, sb_t


def _chunked(v, nc):
    v2 = v.reshape(-1, v.shape[-1])
    return v2.reshape(v2.shape[0], nc, LANES).transpose(1, 0, 2)


def kernel(x, p, mix_norm, mlp_norm, mlp_w1, mlp_w2, pe_proj, pe_gate_norm, pe_gate, dsa_w_in, dsa_w_out, conv_w_in, conv_b_in, conv_w_dw, conv_b_dw, conv_ln_g, conv_ln_b, conv_w_out, conv_b_out, final_norm):
    batch, seq, d = x.shape
    depth = p.shape[0]
    t = batch * seq
    topk = min(TOPK_MAX, seq // KEY_FRAC)
    tm = 512
    tm_big = min(1024, t)
    h = x.reshape(t, d)
    row_vec = lambda v: v.reshape(1, -1)
    w1_all = mlp_w1.astype(BF16)
    w2_all = mlp_w2.astype(BF16)

    for i in range(depth):
        j = i // 2
        if i % 2 == 0:
            w = dsa_w_in[j]
            wq, wk, wv, wqi, wki, wwi = jnp.split(
                w, [Q_DIM, Q_DIM + KV_DIM, Q_DIM + 2 * KV_DIM, Q_DIM + 2 * KV_DIM + IDXQ_DIM,
                    Q_DIM + 2 * KV_DIM + IDXQ_DIM + IDX_DIM], axis=1)
            wwi = jnp.pad(wwi, ((0, 0), (0, LANES - N_IDX_HEADS)))
            w_all = jnp.concatenate([wq, wk, wqi, wv, wki, wwi], axis=1).astype(BF16)
            cos_t, sa_t, sb_t = _rope_tables(seq)
            qt, k, vt, qit, ki, wit = _dsa_proj(h, row_vec(mix_norm[i]), w_all, cos_t, sa_t, sb_t, seq, tm)
            tq_attn = min(512, seq)
            bias = _topk_mask(qit, ki, wit, batch, seq, topk, tq=256, tk=512, tq_attn=tq_attn)
            o = _attention(qt, k, vt, bias, batch, seq, tq=tq_attn, tk=min(1024, seq))
            h = _proj_residual(h, o, dsa_w_out[j].astype(BF16), None, tm_big, "dsa_out")
        else:
            nc = d // LANES
            y = _conv_in(h, row_vec(mix_norm[i]), conv_w_in[j].astype(BF16), row_vec(conv_b_in[j]), tm_big)
            w3 = _chunked(jnp.pad(conv_w_dw[j], ((0, _HALO - CONV_WIDTH), (0, 0))), nc)
            z = _dwconv_ln_swish(y.reshape(batch, seq, d), w3, _chunked(conv_b_dw[j], nc),
                                 _chunked(conv_ln_g[j], nc), _chunked(conv_ln_b[j], nc), tm=512)
            h = _proj_residual(h, z.reshape(t, d), conv_w_out[j].astype(BF16), row_vec(conv_b_out[j]), tm_big,
                               "conv_out")
        h = _mlp(h, row_vec(mlp_norm[i]), w1_all, w2_all, i, tm=512, tf=2048)
        gf = row_vec(final_norm) if i == depth - 1 else None
        h = _pe(h, row_vec(pe_gate_norm[i]), pe_gate[i].astype(BF16), p.reshape(depth, t, -1), i,
                pe_proj[i].astype(BF16), gf, tm)
    return h.reshape(batch, seq, d)
```

```python
import functools
import math

import jax
import jax.numpy as jnp
from jax import lax
from jax.experimental import pallas as pl
from jax.experimental.pallas import tpu as pltpu

F32 = jnp.float32
BF16 = jnp.bfloat16
I32 = jnp.int32

CHUNK = 64
N_HEADS = 16
HEAD_DIM = 128
N_KV_HEADS = 4
GROUP = N_HEADS // N_KV_HEADS
N_IDX_HEADS = 8
IDX_DIM = 128
TOPK_MAX = 256
KEY_FRAC = 4
ROPE_THETA = 500000.0
ROPE_FRAC = 4
CONV_WIDTH = 31
NORM_EPS = 1e-6
Q_DIM = N_HEADS * HEAD_DIM
KV_DIM = N_KV_HEADS * HEAD_DIM
IDXQ_DIM = N_IDX_HEADS * IDX_DIM

LANES = 128
SUBLANES = 8
PACKED_SUBLANES = 16
V_ROWS = HEAD_DIM + PACKED_SUBLANES
F32_MIN_NORMAL = 1.1754943508222875e-38
BF16_MIN_NORMAL_BITS = 0x0080
INT_MIN = -(2 ** 31)
NEG_BIAS = -1e30
LOG2E = math.log2(math.e)

VMEM_LIMIT = 56 * 1024 * 1024


def _params(sem, vmem=VMEM_LIMIT):
    return pltpu.CompilerParams(dimension_semantics=sem, vmem_limit_bytes=vmem)


def _resident(shape):
    nd = len(shape)
    return pl.BlockSpec(shape, lambda *_: (0,) * nd, pipeline_mode=pl.Buffered(1))


def _rms(x, g):
    ms = jnp.mean(x * x, axis=-1, keepdims=True)
    return x * lax.rsqrt(ms + NORM_EPS) * g


_C_Q = 0
_C_K = _C_Q + Q_DIM
_C_QI = _C_K + KV_DIM
_C_V = _C_QI + IDXQ_DIM
_C_KI = _C_V + KV_DIM
_C_WI = _C_KI + IDX_DIM
_C_END = _C_WI + LANES


def _dsa_proj_kernel(h_ref, g_ref, w_ref, cos_ref, sa_ref, sb_ref,
                     qt_ref, k_ref, vt_ref, qit_ref, ki_ref, wit_ref):
    u = _rms(h_ref[...], g_ref[...]).astype(BF16)
    cos = cos_ref[...]
    sa = sa_ref[...]
    sb = sb_ref[...]
    rd = HEAD_DIM // ROPE_FRAC
    half = rd // 2

    def rope(y):
        return (y * cos + pltpu.roll(y, LANES - half, 1) * sa + pltpu.roll(y, half, 1) * sb)

    def proj(c0, n):
        return jnp.dot(u, w_ref[:, c0:c0 + n], preferred_element_type=F32)

    qscale = (HEAD_DIM ** -0.5) * LOG2E
    for c in range(N_HEADS // 4):
        y = proj(_C_Q + 512 * c, 512)
        for j in range(4):
            qt_ref[4 * c + j] = (rope(y[:, 128 * j:128 * (j + 1)]) * qscale).T.astype(BF16)
    y = proj(_C_K, KV_DIM)
    for j in range(N_KV_HEADS):
        k_ref[j] = rope(y[:, 128 * j:128 * (j + 1)]).astype(BF16)
    for c in range(N_IDX_HEADS // 4):
        y = proj(_C_QI + 512 * c, 512)
        for j in range(4):
            qit_ref[4 * c + j] = (rope(y[:, 128 * j:128 * (j + 1)]) * (IDX_DIM ** -0.5)).T.astype(BF16)
    y = proj(_C_KI, IDX_DIM + LANES)
    ki_ref[...] = rope(y[:, :IDX_DIM]).astype(BF16)
    wit_ref[...] = (y[:, IDX_DIM:] * (N_IDX_HEADS ** -0.5)).T[:N_IDX_HEADS]
    y = proj(_C_V, KV_DIM)
    for j in range(N_KV_HEADS):
        vt_ref[j, 0:HEAD_DIM, :] = y[:, 128 * j:128 * (j + 1)].T.astype(BF16)
        vt_ref[j, HEAD_DIM:V_ROWS, :] = jnp.ones((V_ROWS - HEAD_DIM, vt_ref.shape[2]), BF16)


def _dsa_proj(h, g, w_all, cos, sa, sb, seq, tm):
    t, d = h.shape
    nseq = seq // tm
    row = lambda i: (i, 0)
    tab = pl.BlockSpec((tm, LANES), lambda i: (i % nseq, 0))
    return pl.pallas_call(
        _dsa_proj_kernel,
        grid=(t // tm,),
        in_specs=[pl.BlockSpec((tm, d), row), _resident((1, d)), _resident(w_all.shape), tab, tab, tab],
        out_specs=[
            pl.BlockSpec((N_HEADS, HEAD_DIM, tm), lambda i: (0, 0, i)),
            pl.BlockSpec((N_KV_HEADS, tm, HEAD_DIM), lambda i: (0, i, 0)),
            pl.BlockSpec((N_KV_HEADS, V_ROWS, tm), lambda i: (0, 0, i)),
            pl.BlockSpec((N_IDX_HEADS, IDX_DIM, tm), lambda i: (0, 0, i)),
            pl.BlockSpec((tm, IDX_DIM), row),
            pl.BlockSpec((N_IDX_HEADS, tm), lambda i: (0, i)),
        ],
        out_shape=[
            jax.ShapeDtypeStruct((N_HEADS, HEAD_DIM, t), BF16),
            jax.ShapeDtypeStruct((N_KV_HEADS, t, HEAD_DIM), BF16),
            jax.ShapeDtypeStruct((N_KV_HEADS, V_ROWS, t), BF16),
            jax.ShapeDtypeStruct((N_IDX_HEADS, IDX_DIM, t), BF16),
            jax.ShapeDtypeStruct((t, IDX_DIM), BF16),
            jax.ShapeDtypeStruct((N_IDX_HEADS, t), F32),
        ],
        compiler_params=_params(("arbitrary",)),
        name="dsa_proj",
    )(h, g, w_all, cos, sa, sb)


_KT = 256
_LO_FIXED = 6


def _fold(y):
    n = y.shape[0]
    parts = [y[j] for j in range(min(4, n))]
    for j in range(4, n):
        parts[j % 4] = parts[j % 4] + y[j]
    while len(parts) > 1:
        parts = [parts[j] + parts[j + 1] if j + 1 < len(parts) else parts[j] for j in range(0, len(parts), 2)]
    return parts[0]


def _topk_mask_kernel(qit_ref, ki_ref, wit_ref, bias_ref, sc_ref, hi_ref, *, tq, tk, topk, seq):
    i = pl.program_id(1)
    q0 = i * tq
    n_adm = q0 + tq
    nkt = (n_adm + tk - 1) // tk
    nct = n_adm // _KT
    kf = float(topk)
    lane = lax.broadcasted_iota(I32, (1, tq), 1)
    adm_end = q0 + (lane // CHUNK + 1) * CHUNK

    def score_tile(kt, masked):
        k0 = pl.multiple_of(kt * tk, tk)
        kb = ki_ref[pl.ds(k0, tk), :]
        acc = None
        for h in range(N_IDX_HEADS):
            r = jnp.dot(kb, qit_ref[h], preferred_element_type=F32)
            term = wit_ref[h:h + 1, :] * jnp.maximum(r, 0.0)
            acc = term if acc is None else acc + term
        acc = jnp.where(jnp.abs(acc) < F32_MIN_NORMAL, 0.0, acc)
        bits = pltpu.bitcast(acc, I32)
        key = bits ^ ((bits >> 31) & 0x7FFFFFFF)
        top = pltpu.bitcast(bits & jnp.int32(-65536), F32)
        if masked:
            adm = (k0 + lax.broadcasted_iota(I32, (tk, tq), 0)) < adm_end
            key = jnp.where(adm, key, INT_MIN)
            top = jnp.where(adm, top, -jnp.inf)
        sc_ref[pl.ds(k0, tk), :] = key
        hi_ref[pl.ds(k0, tk), :] = top.astype(BF16)

    def interior(kt, carry):
        score_tile(kt, False)
        return carry

    def boundary(kt, carry):
        score_tile(kt, True)
        return carry

    n_int = q0 // tk
    lax.fori_loop(0, n_int, interior, 0)
    lax.fori_loop(n_int, nkt, boundary, 0)

    row_kt = lax.broadcasted_iota(I32, (_KT, tq), 0)

    def count(pred):
        def body(c, acc):
            r0 = pl.multiple_of(c * _KT, _KT)
            key = sc_ref[pl.ds(r0, _KT), :]
            ones = jnp.where(pred(key, r0 + row_kt), 1.0, 0.0)
            return acc + _fold(ones.reshape(_KT // SUBLANES, SUBLANES, tq))
        acc = lax.fori_loop(0, nct, body, jnp.zeros((SUBLANES, tq), F32))
        return jnp.sum(acc, axis=0, keepdims=True)

    def count_hi(cand):
        assert tk // PACKED_SUBLANES < 256
        def body(c, acc):
            r0 = pl.multiple_of(c * tk, tk)
            ones = jnp.where(hi_ref[pl.ds(r0, tk), :] >= cand, jnp.ones((), BF16), jnp.zeros((), BF16))
            part = _fold(ones.reshape(tk // PACKED_SUBLANES, PACKED_SUBLANES, tq))
            return acc + part.astype(F32)
        acc = lax.fori_loop(0, nkt, body, jnp.zeros((PACKED_SUBLANES, tq), F32))
        return jnp.sum(acc, axis=0, keepdims=True)

    def emit(keep):
        def body(c, carry):
            r0 = pl.multiple_of(c * _KT, _KT)
            key = sc_ref[pl.ds(r0, _KT), :]
            bias_ref[0, 0, pl.ds(r0, _KT), :] = jnp.where(keep(key, r0 + row_kt), 0.0, NEG_BIAS).astype(F32)
            return carry
        lax.fori_loop(0, nct, body, 0)

    def hi_pass(p, state):
        u, cnt_u = state
        cand_u = u | (jnp.int32(1) << (15 - p))
        c16 = cand_u - 32768
        c16 = jnp.where(jnp.logical_and(c16 > 0, c16 < BF16_MIN_NORMAL_BITS), BF16_MIN_NORMAL_BITS, c16)
        raw = jnp.where(c16 >= 0, c16, c16 ^ 0x7FFF) << 16
        cnt = count_hi(pltpu.bitcast(raw, F32).astype(BF16))
        ok = cnt >= kf
        return jnp.where(ok, cand_u, u), jnp.where(ok, cnt, cnt_u)

    many = jnp.full((1, tq), float(seq + 1), F32)
    u, cnt_u = lax.fori_loop(0, 16, hi_pass, (jnp.zeros((1, tq), I32), many))

    few = adm_end <= topk

    def unsettled(cnt):
        return jnp.max(jnp.where(jnp.logical_or(few, cnt == kf), 0.0, 1.0)) > 0.0

    def lo_cond(state):
        p, _, _, go = state
        return jnp.logical_and(p < 16, go)

    def two_passes(p, tau, cnt_t):
        for b in range(2):
            cand = tau | (jnp.int32(1) << (15 - p - b))
            cnt = count(lambda key, kidx: key >= cand)
            ok = cnt >= kf
            cnt_t = jnp.where(ok, cnt, cnt_t)
            tau = jnp.where(ok, cand, tau)
        return tau, cnt_t

    def lo_fixed(jj, state):
        return two_passes(2 * jj, *state)

    def lo_pass(state):
        p, tau, cnt_t, _ = state
        tau, cnt_t = two_passes(p, tau, cnt_t)
        return p + 2, tau, cnt_t, unsettled(cnt_t)

    tau, cnt_t = lax.fori_loop(0, _LO_FIXED // 2, lo_fixed, ((u - 32768) << 16, cnt_u))
    _, tau, cnt_ge, _ = lax.while_loop(lo_cond, lo_pass, (jnp.int32(_LO_FIXED), tau, cnt_t, unsettled(cnt_t)))
    tau = jnp.maximum(tau, INT_MIN + 1)
    has_tie = jnp.max(jnp.where(few, 0.0, cnt_ge)) > kf

    def fill(c, carry):
        r0 = pl.multiple_of(c * _KT, _KT)
        bias_ref[0, 0, pl.ds(r0, _KT), :] = jnp.full((_KT, tq), NEG_BIAS, F32)
        return carry

    lax.fori_loop(nct, seq // _KT, fill, 0)

    @pl.when(jnp.logical_not(has_tie))
    def _():
        emit(lambda key, kidx: key >= tau)

    @pl.when(has_tie)
    def _():
        tied = jnp.logical_and(jnp.logical_not(few), cnt_ge > kf)
        need = jnp.where(tied, kf - count(lambda key, kidx: key > tau), 0.0)
        max_need = jnp.max(need).astype(I32)
        nbits = max(1, (seq - 1).bit_length())

        def next_tied(after):
            def body(c, acc):
                r0 = pl.multiple_of(c * _KT, _KT)
                key = sc_ref[pl.ds(r0, _KT), :]
                kidx = r0 + row_kt
                v = jnp.where(jnp.logical_and(key == tau, kidx > after), kidx, seq)
                v = v.reshape(_KT // SUBLANES, SUBLANES, tq)
                for j in range(_KT // SUBLANES):
                    acc = jnp.minimum(acc, v[j])
                return acc
            acc = lax.fori_loop(0, nct, body, jnp.full((SUBLANES, tq), seq, I32))
            return jnp.min(acc.astype(F32), axis=0, keepdims=True).astype(I32)

        def by_scan():
            def step(r, j0):
                return jnp.where(r.astype(F32) < need, next_tied(j0), j0)
            return lax.fori_loop(0, max_need, step, jnp.full((1, tq), -1, I32))

        def by_bisect():
            def idx_pass(p, j0):
                cand = j0 + (jnp.int32(1) << (nbits - 1 - p))
                c_lt = count(lambda key, kidx: jnp.logical_and(key == tau, kidx < cand))
                return jnp.where(c_lt < need, cand, j0)
            return lax.fori_loop(0, nbits, idx_pass, jnp.zeros((1, tq), I32))

        j0 = lax.cond(max_need <= nbits, by_scan, by_bisect)
        j0 = jnp.where(tied, j0, seq)
        emit(lambda key, kidx: jnp.logical_or(key > tau, jnp.logical_and(key == tau, kidx <= j0)))


def _topk_mask(qit, ki, wit, batch, seq, topk, tq, tk, tq_attn):
    assert tq % _KT == 0 and tq_attn % tq == 0 and seq % tk == 0
    nq = seq // tq
    per = tq_attn // tq
    kern = functools.partial(_topk_mask_kernel, tq=tq, tk=tk, topk=topk, seq=seq)
    return pl.pallas_call(
        kern,
        grid=(batch, nq),
        in_specs=[
            pl.BlockSpec((N_IDX_HEADS, IDX_DIM, tq), lambda b, i: (0, 0, b * nq + i)),
            pl.BlockSpec((seq, IDX_DIM), lambda b, i: (b, 0)),
            pl.BlockSpec((N_IDX_HEADS, tq), lambda b, i: (0, b * nq + i)),
        ],
        out_specs=pl.BlockSpec((1, 1, seq, tq), lambda b, i: (b, i // per, 0, i % per)),
        out_shape=jax.ShapeDtypeStruct((batch, seq // tq_attn, seq, tq_attn), F32),
        scratch_shapes=[pltpu.VMEM((seq, tq), I32), pltpu.VMEM((seq, tq), BF16)],
        compiler_params=_params(("arbitrary", "arbitrary")),
        name="topk_mask",
    )(qit, ki, wit)


_HP = 2


def _attn_kernel(qi_ref, kj_ref, qt_ref, k_ref, vt_ref, bias_ref, o_ref,
                 m_ref, l_ref, acc_ref, sa_ref, sb_ref, ma_ref, mb_ref, *, tq, tk):
    n = pl.program_id(1)
    i = qi_ref[n]
    kt = kj_ref[n]
    last = (i * tq + tq + tk - 1) // tk - 1
    nstage = N_HEADS // _HP

    @pl.when(kt == 0)
    def _():
        m_ref[...] = jnp.full(m_ref.shape, NEG_BIAS, F32)
        l_ref[...] = jnp.zeros(l_ref.shape, F32)
        acc_ref[...] = jnp.zeros(acc_ref.shape, F32)


    def scores(j, buf, mx):
        for r in range(_HP):
            h = j * _HP + r
            sb = jnp.dot(k_ref[h // GROUP], qt_ref[h], preferred_element_type=F32) + bias_ref[0, 0]
            buf[r] = sb
            mx[r] = jnp.max(sb, axis=0, keepdims=True)

    def softmax_pv(j, buf, mx):
        for r in range(_HP):
            h = j * _HP + r
            m_prev = m_ref[h]
            m_new = jnp.maximum(m_prev, mx[r])
            alpha = jnp.exp2(m_prev - m_new)
            p = jnp.exp2(buf[r] - m_new).astype(BF16)
            pv = jnp.dot(vt_ref[h // GROUP], p, preferred_element_type=F32)
            l_ref[h] = alpha * l_ref[h] + pv[HEAD_DIM:HEAD_DIM + 1]
            acc_ref[h] = alpha * acc_ref[h] + pv[:HEAD_DIM]
            m_ref[h] = m_new

    scores(0, sa_ref, ma_ref)

    def two_stages(jj, carry):
        j = 2 * jj
        scores(j + 1, sb_ref, mb_ref)
        softmax_pv(j, sa_ref, ma_ref)
        scores(j + 2, sa_ref, ma_ref)
        softmax_pv(j + 1, sb_ref, mb_ref)
        return carry

    lax.fori_loop(0, nstage // 2 - 1, two_stages, 0)
    scores(nstage - 1, sb_ref, mb_ref)
    softmax_pv(nstage - 2, sa_ref, ma_ref)
    softmax_pv(nstage - 1, sb_ref, mb_ref)

    @pl.when(kt == last)
    def _():
        for h in range(N_HEADS):
            o_ref[:, HEAD_DIM * h:HEAD_DIM * (h + 1)] = (acc_ref[h] / l_ref[h]).T.astype(BF16)


def _attention(qt, k, vt, bias, batch, seq, tq, tk):
    nq = seq // tq
    nk = seq // tk
    t = batch * seq
    pairs = [(i, j) for i in range(nq) for j in range((i * tq + tq + tk - 1) // tk)]
    qi = jnp.asarray([p[0] for p in pairs], I32)
    kj = jnp.asarray([p[1] for p in pairs], I32)
    kern = functools.partial(_attn_kernel, tq=tq, tk=tk)
    grid_spec = pltpu.PrefetchScalarGridSpec(
        num_scalar_prefetch=2,
        grid=(batch, len(pairs)),
        in_specs=[
            pl.BlockSpec((N_HEADS, HEAD_DIM, tq), lambda b, n, qi, kj: (0, 0, b * nq + qi[n])),
            pl.BlockSpec((N_KV_HEADS, tk, HEAD_DIM), lambda b, n, qi, kj: (0, b * nk + kj[n], 0)),
            pl.BlockSpec((N_KV_HEADS, V_ROWS, tk), lambda b, n, qi, kj: (0, 0, b * nk + kj[n])),
            pl.BlockSpec((1, 1, tk, tq), lambda b, n, qi, kj: (b, qi[n], kj[n], 0)),
        ],
        out_specs=pl.BlockSpec((tq, Q_DIM), lambda b, n, qi, kj: (b * nq + qi[n], 0)),
        scratch_shapes=[
            pltpu.VMEM((N_HEADS, 1, tq), F32),
            pltpu.VMEM((N_HEADS, 1, tq), F32),
            pltpu.VMEM((N_HEADS, HEAD_DIM, tq), F32),
            pltpu.VMEM((_HP, tk, tq), F32),
            pltpu.VMEM((_HP, tk, tq), F32),
            pltpu.VMEM((_HP, 1, tq), F32),
            pltpu.VMEM((_HP, 1, tq), F32),
        ],
    )
    return pl.pallas_call(
        kern,
        grid_spec=grid_spec,
        out_shape=jax.ShapeDtypeStruct((t, Q_DIM), BF16),
        compiler_params=_params(("arbitrary", "arbitrary")),
        name="masked_attention",
    )(qi, kj, qt, k, vt, bias)


def _proj_residual_kernel(h_ref, x_ref, w_ref, *rest, has_bias, tn):
    if has_bias:
        b_ref, o_ref = rest
    else:
        (o_ref,) = rest
    x = x_ref[...]
    for c in range(w_ref.shape[1] // tn):
        cols = slice(c * tn, (c + 1) * tn)
        y = h_ref[:, cols] + jnp.dot(x, w_ref[:, cols], preferred_element_type=F32)
        if has_bias:
            y = y + b_ref[:, cols]
        o_ref[:, cols] = y


def _proj_residual(h, x, w, b, tm, name):
    t, d = h.shape
    kdim = x.shape[1]
    row = lambda i: (i, 0)
    in_specs = [pl.BlockSpec((tm, d), row), pl.BlockSpec((tm, kdim), row), _resident(w.shape)]
    args = [h, x, w]
    if b is not None:
        in_specs.append(_resident((1, d)))
        args.append(b)
    kern = functools.partial(_proj_residual_kernel, has_bias=b is not None, tn=512)
    return pl.pallas_call(
        kern,
        grid=(t // tm,),
        in_specs=in_specs,
        out_specs=pl.BlockSpec((tm, d), row),
        out_shape=jax.ShapeDtypeStruct((t, d), F32),
        compiler_params=_params(("arbitrary",)),
        name=name,
    )(*args)


def _mlp_kernel(h_ref, g_ref, w1_ref, w2_ref, o_ref, u_ref):
    @pl.when(pl.program_id(1) == 0)
    def _():
        h = h_ref[...]
        u_ref[...] = _rms(h, g_ref[...]).astype(BF16)
        o_ref[...] = h

    a = jnp.dot(u_ref[...], w1_ref[...], preferred_element_type=F32)
    a = jnp.square(jnp.maximum(a, 0.0)).astype(BF16)
    o_ref[...] += jnp.dot(a, w2_ref[...], preferred_element_type=F32)


def _mlp(h, g, w1, w2, layer, tm, tf):
    t, d = h.shape
    f = w1.shape[2]
    return pl.pallas_call(
        _mlp_kernel,
        grid=(t // tm, f // tf),
        in_specs=[
            pl.BlockSpec((tm, d), lambda i, j: (i, 0)),
            _resident((1, d)),
            pl.BlockSpec((None, d, tf), lambda i, j: (layer, 0, j)),
            pl.BlockSpec((None, tf, d), lambda i, j: (layer, j, 0)),
        ],
        out_specs=pl.BlockSpec((tm, d), lambda i, j: (i, 0)),
        out_shape=jax.ShapeDtypeStruct((t, d), F32),
        scratch_shapes=[pltpu.VMEM((tm, d), BF16)],
        compiler_params=_params(("arbitrary", "arbitrary")),
        name="sqrelu_mlp",
    )(h, g, w1, w2)


def _pe_kernel(h_ref, g_ref, wg_ref, p_ref, wp_ref, *rest, final, tn):
    if final:
        gf_ref, o_ref = rest
    else:
        (o_ref,) = rest
    h = h_ref[...]
    u = _rms(h, g_ref[...]).astype(BF16)
    pb = p_ref[...].astype(BF16)
    for c in range(wg_ref.shape[1] // tn):
        cols = slice(c * tn, (c + 1) * tn)
        gate = jax.nn.sigmoid(jnp.dot(u, wg_ref[:, cols], preferred_element_type=F32))
        emb = jnp.dot(pb, wp_ref[:, cols], preferred_element_type=F32)
        o_ref[:, cols] = h_ref[:, cols] + emb * gate
    if final:
        o_ref[...] = _rms(o_ref[...], gf_ref[...])


def _pe(h, g, wg, p, layer, wp, gf, tm):
    t, d = h.shape
    row = lambda i: (i, 0)
    in_specs = [pl.BlockSpec((tm, d), row), _resident((1, d)), _resident(wg.shape),
                pl.BlockSpec((None, tm, p.shape[2]), lambda r: (layer, r, 0)), _resident(wp.shape)]
    args = [h, g, wg, p, wp]
    if gf is not None:
        in_specs.append(_resident((1, d)))
        args.append(gf)
    kern = functools.partial(_pe_kernel, final=gf is not None, tn=512)
    return pl.pallas_call(
        kern,
        grid=(t // tm,),
        in_specs=in_specs,
        out_specs=pl.BlockSpec((tm, d), row),
        out_shape=jax.ShapeDtypeStruct((t, d), F32),
        compiler_params=_params(("arbitrary",)),
        name="gated_embedding",
    )(*args)


def _conv_in_kernel(h_ref, g_ref, w_ref, b_ref, y_ref, *, tn):
    d = y_ref.shape[1]
    u = _rms(h_ref[...], g_ref[...]).astype(BF16)
    for c in range(d // tn):
        a = jnp.dot(u, w_ref[:, c * tn:(c + 1) * tn], preferred_element_type=F32) + b_ref[:, c * tn:(c + 1) * tn]
        gt = (jnp.dot(u, w_ref[:, d + c * tn:d + (c + 1) * tn], preferred_element_type=F32)
              + b_ref[:, d + c * tn:d + (c + 1) * tn])
        y_ref[:, c * tn:(c + 1) * tn] = a * jax.nn.sigmoid(gt)


def _conv_in(h, g, w, b, tm):
    t, d = h.shape
    row = lambda i: (i, 0)
    return pl.pallas_call(
        functools.partial(_conv_in_kernel, tn=512),
        grid=(t // tm,),
        in_specs=[pl.BlockSpec((tm, d), row), _resident((1, d)), _resident(w.shape), _resident(b.shape)],
        out_specs=pl.BlockSpec((tm, d), row),
        out_shape=jax.ShapeDtypeStruct((t, d), F32),
        compiler_params=_params(("arbitrary",)),
        name="conv_in_glu",
    )(h, g, w, b)


_HALO = 32
_CONV_RS = 64


def _dwconv_kernel(y_ref, halo_ref, w_ref, bdw_ref, lg_ref, lb_ref, z_ref, ext_ref, cv_ref, *, tm):
    d = y_ref.shape[2]
    nc = d // LANES
    first = pl.program_id(1) == 0
    for c in range(nc):
        cols = slice(c * LANES, (c + 1) * LANES)
        ext_ref[c, 0:_HALO] = jnp.where(first, 0.0, halo_ref[0, :, cols])
        ext_ref[c, _HALO:_HALO + tm] = y_ref[0, :, cols]

    off = _HALO - (CONV_WIDTH - 1)

    def chunk(c, carry):
        for rs in range(tm // _CONV_RS):
            r0 = rs * _CONV_RS
            acc = jnp.broadcast_to(bdw_ref[c], (_CONV_RS, LANES))
            for j in range(CONV_WIDTH):
                acc = acc + w_ref[c, j:j + 1, :] * ext_ref[c, r0 + off + j:r0 + off + j + _CONV_RS, :]
            cv_ref[c, r0:r0 + _CONV_RS] = acc
        return carry

    lax.fori_loop(0, nc, chunk, 0)

    s1 = cv_ref[0]
    for c in range(1, nc):
        s1 = s1 + cv_ref[c]
    mu = jnp.sum(s1, axis=1, keepdims=True) / d
    s2 = jnp.square(cv_ref[0] - mu)
    for c in range(1, nc):
        s2 = s2 + jnp.square(cv_ref[c] - mu)
    rstd = lax.rsqrt(jnp.sum(s2, axis=1, keepdims=True) / d + NORM_EPS)
    for c in range(nc):
        yn = (cv_ref[c] - mu) * rstd * lg_ref[c] + lb_ref[c]
        z_ref[0, :, c * LANES:(c + 1) * LANES] = (yn * jax.nn.sigmoid(yn)).astype(BF16)


def _dwconv_ln_swish(y3, w3, bdw3, lg3, lb3, tm):
    batch, seq, d = y3.shape
    nc = d // LANES
    hb = tm // _HALO
    kern = functools.partial(_dwconv_kernel, tm=tm)
    return pl.pallas_call(
        kern,
        grid=(batch, seq // tm),
        in_specs=[
            pl.BlockSpec((1, tm, d), lambda b, i: (b, i, 0)),
            pl.BlockSpec((1, _HALO, d), lambda b, i: (b, jnp.maximum(i * hb - 1, 0), 0)),
            _resident(w3.shape), _resident(bdw3.shape), _resident(lg3.shape), _resident(lb3.shape),
        ],
        out_specs=pl.BlockSpec((1, tm, d), lambda b, i: (b, i, 0)),
        out_shape=jax.ShapeDtypeStruct((batch, seq, d), BF16),
        scratch_shapes=[
            pltpu.VMEM((nc, _HALO + tm, LANES), F32),
            pltpu.VMEM((nc, tm, LANES), F32),
        ],
        compiler_params=_params(("arbitrary", "arbitrary")),
        name="dwconv_ln_swish",
    )(y3, y3, w3, bdw3, lg3, lb3)


def _rope_tables(seq):
    rd = HEAD_DIM // ROPE_FRAC
    half = rd // 2
    pos = jnp.arange(seq, dtype=F32)
    inv = ROPE_THETA ** (-jnp.arange(half, dtype=F32) * 2.0 / rd)
    ang = pos[:, None] * inv[None, :]
    cos = jnp.cos(ang)
    sin = jnp.sin(ang)
    ones = jnp.ones((seq, HEAD_DIM - rd), F32)
    zeros = jnp.zeros((seq, HEAD_DIM - rd), F32)
    zh = jnp.zeros((seq, half), F32)
    cos_t = jnp.concatenate([cos, cos, ones], axis=1)
    sa_t = jnp.concatenate([-sin, zh, zeros], axis=1)
    sb_t = jnp.concatenate([zh, sin, zeros], axis=1)
    return cos_t, sa_t, sb_t


def _chunked(v, nc):
    v2 = v.reshape(-1, v.shape[-1])
    return v2.reshape(v2.shape[0], nc, LANES).transpose(1, 0, 2)


def kernel(x, p, mix_norm, mlp_norm, mlp_w1, mlp_w2, pe_proj, pe_gate_norm, pe_gate, dsa_w_in, dsa_w_out, conv_w_in, conv_b_in, conv_w_dw, conv_b_dw, conv_ln_g, conv_ln_b, conv_w_out, conv_b_out, final_norm):
    batch, seq, d = x.shape
    depth = p.shape[0]
    t = batch * seq
    topk = min(TOPK_MAX, seq // KEY_FRAC)
    tm = 512
    tm_big = min(1024, t)
    h = x.reshape(t, d)
    row_vec = lambda v: v.reshape(1, -1)
    w1_all = mlp_w1.astype(BF16)
    w2_all = mlp_w2.astype(BF16)

    for i in range(depth):
        j = i // 2
        if i % 2 == 0:
            w = dsa_w_in[j]
            wq, wk, wv, wqi, wki, wwi = jnp.split(
                w, [Q_DIM, Q_DIM + KV_DIM, Q_DIM + 2 * KV_DIM, Q_DIM + 2 * KV_DIM + IDXQ_DIM,
                    Q_DIM + 2 * KV_DIM + IDXQ_DIM + IDX_DIM], axis=1)
            wwi = jnp.pad(wwi, ((0, 0), (0, LANES - N_IDX_HEADS)))
            w_all = jnp.concatenate([wq, wk, wqi, wv, wki, wwi], axis=1).astype(BF16)
            cos_t, sa_t, sb_t = _rope_tables(seq)
            qt, k, vt, qit, ki, wit = _dsa_proj(h, row_vec(mix_norm[i]), w_all, cos_t, sa_t, sb_t, seq, tm)
            tq_attn = min(512, seq)
            bias = _topk_mask(qit, ki, wit, batch, seq, topk, tq=256, tk=512, tq_attn=tq_attn)
            o = _attention(qt, k, vt, bias, batch, seq, tq=tq_attn, tk=min(1024, seq))
            h = _proj_residual(h, o, dsa_w_out[j].astype(BF16), None, tm_big, "dsa_out")
        else:
            nc = d // LANES
            y = _conv_in(h, row_vec(mix_norm[i]), conv_w_in[j].astype(BF16), row_vec(conv_b_in[j]), tm_big)
            w3 = _chunked(jnp.pad(conv_w_dw[j], ((0, _HALO - CONV_WIDTH), (0, 0))), nc)
            z = _dwconv_ln_swish(y.reshape(batch, seq, d), w3, _chunked(conv_b_dw[j], nc),
                                 _chunked(conv_ln_g[j], nc), _chunked(conv_ln_b[j], nc), tm=512)
            h = _proj_residual(h, z.reshape(t, d), conv_w_out[j].astype(BF16), row_vec(conv_b_out[j]), tm_big,
                               "conv_out")
        h = _mlp(h, row_vec(mlp_norm[i]), w1_all, w2_all, i, tm=512, tf=2048)
        gf = row_vec(final_norm) if i == depth - 1 else None
        h = _pe(h, row_vec(pe_gate_norm[i]), pe_gate[i].astype(BF16), p.reshape(depth, t, -1), i,
                pe_proj[i].astype(BF16), gf, tm)
    return h.reshape(batch, seq, d)
```

```python
import functools
import math

import jax
import jax.numpy as jnp
from jax import lax
from jax.experimental import pallas as pl
from jax.experimental.pallas import tpu as pltpu

F32 = jnp.float32
BF16 = jnp.bfloat16
I32 = jnp.int32

CHUNK = 64
N_HEADS = 16
HEAD_DIM = 128
N_KV_HEADS = 4
GROUP = N_HEADS // N_KV_HEADS
N_IDX_HEADS = 8
IDX_DIM = 128
TOPK_MAX = 256
KEY_FRAC = 4
ROPE_THETA = 500000.0
ROPE_FRAC = 4
CONV_WIDTH = 31
NORM_EPS = 1e-6
Q_DIM = N_HEADS * HEAD_DIM
KV_DIM = N_KV_HEADS * HEAD_DIM
IDXQ_DIM = N_IDX_HEADS * IDX_DIM

LANES = 128
SUBLANES = 8
PACKED_SUBLANES = 16
V_ROWS = HEAD_DIM + PACKED_SUBLANES
F32_MIN_NORMAL = 1.1754943508222875e-38
BF16_MIN_NORMAL_BITS = 0x0080
INT_MIN = -(2 ** 31)
NEG_BIAS = -1e30
LOG2E = math.log2(math.e)

VMEM_LIMIT = 56 * 1024 * 1024


def _params(sem, vmem=VMEM_LIMIT):
    return pltpu.CompilerParams(dimension_semantics=sem, vmem_limit_bytes=vmem)


def _resident(shape):
    nd = len(shape)
    return pl.BlockSpec(shape, lambda *_: (0,) * nd, pipeline_mode=pl.Buffered(1))


def _rms(x, g):
    ms = jnp.mean(x * x, axis=-1, keepdims=True)
    return x * lax.rsqrt(ms + NORM_EPS) * g


_C_Q = 0
_C_K = _C_Q + Q_DIM
_C_QI = _C_K + KV_DIM
_C_V = _C_QI + IDXQ_DIM
_C_KI = _C_V + KV_DIM
_C_WI = _C_KI + IDX_DIM
_C_END = _C_WI + LANES


def _dsa_proj_kernel(h_ref, g_ref, w_ref, cos_ref, sa_ref, sb_ref,
                     qt_ref, k_ref, vt_ref, qit_ref, ki_ref, wit_ref):
    u = _rms(h_ref[...], g_ref[...]).astype(BF16)
    cos = cos_ref[...]
    sa = sa_ref[...]
    sb = sb_ref[...]
    rd = HEAD_DIM // ROPE_FRAC
    half = rd // 2

    def rope(y):
        return (y * cos + pltpu.roll(y, LANES - half, 1) * sa + pltpu.roll(y, half, 1) * sb)

    def proj(c0, n):
        return jnp.dot(u, w_ref[:, c0:c0 + n], preferred_element_type=F32)

    qscale = (HEAD_DIM ** -0.5) * LOG2E
    for c in range(N_HEADS // 4):
        y = proj(_C_Q + 512 * c, 512)
        for j in range(4):
            qt_ref[4 * c + j] = (rope(y[:, 128 * j:128 * (j + 1)]) * qscale).T.astype(BF16)
    y = proj(_C_K, KV_DIM)
    for j in range(N_KV_HEADS):
        k_ref[j] = rope(y[:, 128 * j:128 * (j + 1)]).astype(BF16)
    for c in range(N_IDX_HEADS // 4):
        y = proj(_C_QI + 512 * c, 512)
        for j in range(4):
            qit_ref[4 * c + j] = (rope(y[:, 128 * j:128 * (j + 1)]) * (IDX_DIM ** -0.5)).T.astype(BF16)
    y = proj(_C_KI, IDX_DIM + LANES)
    ki_ref[...] = rope(y[:, :IDX_DIM]).astype(BF16)
    wit_ref[...] = (y[:, IDX_DIM:] * (N_IDX_HEADS ** -0.5)).T[:N_IDX_HEADS]
    y = proj(_C_V, KV_DIM)
    for j in range(N_KV_HEADS):
        vt_ref[j, 0:HEAD_DIM, :] = y[:, 128 * j:128 * (j + 1)].T.astype(BF16)
        vt_ref[j, HEAD_DIM:V_ROWS, :] = jnp.ones((V_ROWS - HEAD_DIM, vt_ref.shape[2]), BF16)


def _dsa_proj(h, g, w_all, cos, sa, sb, seq, tm):
    t, d = h.shape
    nseq = seq // tm
    row = lambda i: (i, 0)
    tab = pl.BlockSpec((tm, LANES), lambda i: (i % nseq, 0))
    return pl.pallas_call(
        _dsa_proj_kernel,
        grid=(t // tm,),
        in_specs=[pl.BlockSpec((tm, d), row), _resident((1, d)), _resident(w_all.shape), tab, tab, tab],
        out_specs=[
            pl.BlockSpec((N_HEADS, HEAD_DIM, tm), lambda i: (0, 0, i)),
            pl.BlockSpec((N_KV_HEADS, tm, HEAD_DIM), lambda i: (0, i, 0)),
            pl.BlockSpec((N_KV_HEADS, V_ROWS, tm), lambda i: (0, 0, i)),
            pl.BlockSpec((N_IDX_HEADS, IDX_DIM, tm), lambda i: (0, 0, i)),
            pl.BlockSpec((tm, IDX_DIM), row),
            pl.BlockSpec((N_IDX_HEADS, tm), lambda i: (0, i)),
        ],
        out_shape=[
            jax.ShapeDtypeStruct((N_HEADS, HEAD_DIM, t), BF16),
            jax.ShapeDtypeStruct((N_KV_HEADS, t, HEAD_DIM), BF16),
            jax.ShapeDtypeStruct((N_KV_HEADS, V_ROWS, t), BF16),
            jax.ShapeDtypeStruct((N_IDX_HEADS, IDX_DIM, t), BF16),
            jax.ShapeDtypeStruct((t, IDX_DIM), BF16),
            jax.ShapeDtypeStruct((N_IDX_HEADS, t), F32),
        ],
        compiler_params=_params(("arbitrary",)),
        name="dsa_proj",
    )(h, g, w_all, cos, sa, sb)


_KT = 256
_LO_FIXED = 6


def _fold(y):
    n = y.shape[0]
    parts = [y[j] for j in range(min(4, n))]
    for j in range(4, n):
        parts[j % 4] = parts[j % 4] + y[j]
    while len(parts) > 1:
        parts = [parts[j] + parts[j + 1] if j + 1 < len(parts) else parts[j] for j in range(0, len(parts), 2)]
    return parts[0]


def _topk_mask_kernel(qit_ref, ki_ref, wit_ref, bias_ref, sc_ref, hi_ref, *, tq, tk, topk, seq):
    i = pl.program_id(1)
    q0 = i * tq
    n_adm = q0 + tq
    nkt = (n_adm + tk - 1) // tk
    nct = n_adm // _KT
    kf = float(topk)
    lane = lax.broadcasted_iota(I32, (1, tq), 1)
    adm_end = q0 + (lane // CHUNK + 1) * CHUNK

    def score_tile(kt, masked):
        k0 = pl.multiple_of(kt * tk, tk)
        kb = ki_ref[pl.ds(k0, tk), :]
        acc = None
        for h in range(N_IDX_HEADS):
            r = jnp.dot(kb, qit_ref[h], preferred_element_type=F32)
            term = wit_ref[h:h + 1, :] * jnp.maximum(r, 0.0)
            acc = term if acc is None else acc + term
        acc = jnp.where(jnp.abs(acc) < F32_MIN_NORMAL, 0.0, acc)
        bits = pltpu.bitcast(acc, I32)
        key = bits ^ ((bits >> 31) & 0x7FFFFFFF)
        top = pltpu.bitcast(bits & jnp.int32(-65536), F32)
        if masked:
            adm = (k0 + lax.broadcasted_iota(I32, (tk, tq), 0)) < adm_end
            key = jnp.where(adm, key, INT_MIN)
            top = jnp.where(adm, top, -jnp.inf)
        sc_ref[pl.ds(k0, tk), :] = key
        hi_ref[pl.ds(k0, tk), :] = top.astype(BF16)

    def interior(kt, carry):
        score_tile(kt, False)
        return carry

    def boundary(kt, carry):
        score_tile(kt, True)
        return carry

    n_int = q0 // tk
    lax.fori_loop(0, n_int, interior, 0)
    lax.fori_loop(n_int, nkt, boundary, 0)

    row_kt = lax.broadcasted_iota(I32, (_KT, tq), 0)

    def count(pred):
        def body(c, acc):
            r0 = pl.multiple_of(c * _KT, _KT)
            key = sc_ref[pl.ds(r0, _KT), :]
            ones = jnp.where(pred(key, r0 + row_kt), 1.0, 0.0)
            return acc + _fold(ones.reshape(_KT // SUBLANES, SUBLANES, tq))
        acc = lax.fori_loop(0, nct, body, jnp.zeros((SUBLANES, tq), F32))
        return jnp.sum(acc, axis=0, keepdims=True)

    def count_hi(cand):
        assert tk // PACKED_SUBLANES < 256
        def body(c, acc):
            r0 = pl.multiple_of(c * tk, tk)
            ones = jnp.where(hi_ref[pl.ds(r0, tk), :] >= cand, jnp.ones((), BF16), jnp.zeros((), BF16))
            part = _fold(ones.reshape(tk // PACKED_SUBLANES, PACKED_SUBLANES, tq))
            return acc + part.astype(F32)
        acc = lax.fori_loop(0, nkt, body, jnp.zeros((PACKED_SUBLANES, tq), F32))
        return jnp.sum(acc, axis=0, keepdims=True)

    def emit(keep):
        def body(c, carry):
            r0 = pl.multiple_of(c * _KT, _KT)
            key = sc_ref[pl.ds(r0, _KT), :]
            bias_ref[0, 0, pl.ds(r0, _KT), :] = jnp.where(keep(key, r0 + row_kt), 0.0, NEG_BIAS).astype(BF16)
            return carry
        lax.fori_loop(0, nct, body, 0)

    def hi_pass(p, state):
        u, cnt_u = state
        cand_u = u | (jnp.int32(1) << (15 - p))
        c16 = cand_u - 32768
        c16 = jnp.where(jnp.logical_and(c16 > 0, c16 < BF16_MIN_NORMAL_BITS), BF16_MIN_NORMAL_BITS, c16)
        raw = jnp.where(c16 >= 0, c16, c16 ^ 0x7FFF) << 16
        cnt = count_hi(pltpu.bitcast(raw, F32).astype(BF16))
        ok = cnt >= kf
        return jnp.where(ok, cand_u, u), jnp.where(ok, cnt, cnt_u)

    many = jnp.full((1, tq), float(seq + 1), F32)
    u, cnt_u = lax.fori_loop(0, 16, hi_pass, (jnp.zeros((1, tq), I32), many))

    few = adm_end <= topk

    def unsettled(cnt):
        return jnp.max(jnp.where(jnp.logical_or(few, cnt == kf), 0.0, 1.0)) > 0.0

    def lo_cond(state):
        p, _, _, go = state
        return jnp.logical_and(p < 16, go)

    def two_passes(p, tau, cnt_t):
        for b in range(2):
            cand = tau | (jnp.int32(1) << (15 - p - b))
            cnt = count(lambda key, kidx: key >= cand)
            ok = cnt >= kf
            cnt_t = jnp.where(ok, cnt, cnt_t)
            tau = jnp.where(ok, cand, tau)
        return tau, cnt_t

    def lo_fixed(jj, state):
        return two_passes(2 * jj, *state)

    def lo_pass(state):
        p, tau, cnt_t, _ = state
        tau, cnt_t = two_passes(p, tau, cnt_t)
        return p + 2, tau, cnt_t, unsettled(cnt_t)

    tau, cnt_t = lax.fori_loop(0, _LO_FIXED // 2, lo_fixed, ((u - 32768) << 16, cnt_u))
    _, tau, cnt_ge, _ = lax.while_loop(lo_cond, lo_pass, (jnp.int32(_LO_FIXED), tau, cnt_t, unsettled(cnt_t)))
    tau = jnp.maximum(tau, INT_MIN + 1)
    has_tie = jnp.max(jnp.where(few, 0.0, cnt_ge)) > kf

    def fill(c, carry):
        r0 = pl.multiple_of(c * _KT, _KT)
        bias_ref[0, 0, pl.ds(r0, _KT), :] = jnp.full((_KT, tq), NEG_BIAS, BF16)
        return carry

    lax.fori_loop(nct, seq // _KT, fill, 0)

    @pl.when(jnp.logical_not(has_tie))
    def _():
        emit(lambda key, kidx: key >= tau)

    @pl.when(has_tie)
    def _():
        tied = jnp.logical_and(jnp.logical_not(few), cnt_ge > kf)
        need = jnp.where(tied, kf - count(lambda key, kidx: key > tau), 0.0)
        max_need = jnp.max(need).astype(I32)
        nbits = max(1, (seq - 1).bit_length())

        def next_tied(after):
            def body(c, acc):
                r0 = pl.multiple_of(c * _KT, _KT)
                key = sc_ref[pl.ds(r0, _KT), :]
                kidx = r0 + row_kt
                v = jnp.where(jnp.logical_and(key == tau, kidx > after), kidx, seq)
                v = v.reshape(_KT // SUBLANES, SUBLANES, tq)
                for j in range(_KT // SUBLANES):
                    acc = jnp.minimum(acc, v[j])
                return acc
            acc = lax.fori_loop(0, nct, body, jnp.full((SUBLANES, tq), seq, I32))
            return jnp.min(acc.astype(F32), axis=0, keepdims=True).astype(I32)

        def by_scan():
            def step(r, j0):
                return jnp.where(r.astype(F32) < need, next_tied(j0), j0)
            return lax.fori_loop(0, max_need, step, jnp.full((1, tq), -1, I32))

        def by_bisect():
            def idx_pass(p, j0):
                cand = j0 + (jnp.int32(1) << (nbits - 1 - p))
                c_lt = count(lambda key, kidx: jnp.logical_and(key == tau, kidx < cand))
                return jnp.where(c_lt < need, cand, j0)
            return lax.fori_loop(0, nbits, idx_pass, jnp.zeros((1, tq), I32))

        j0 = lax.cond(max_need <= nbits, by_scan, by_bisect)
        j0 = jnp.where(tied, j0, seq)
        emit(lambda key, kidx: jnp.logical_or(key > tau, jnp.logical_and(key == tau, kidx <= j0)))


def _topk_mask(qit, ki, wit, batch, seq, topk, tq, tk, tq_attn):
    assert tq % _KT == 0 and tq_attn % tq == 0 and seq % tk == 0
    nq = seq // tq
    per = tq_attn // tq
    kern = functools.partial(_topk_mask_kernel, tq=tq, tk=tk, topk=topk, seq=seq)
    return pl.pallas_call(
        kern,
        grid=(batch, nq),
        in_specs=[
            pl.BlockSpec((N_IDX_HEADS, IDX_DIM, tq), lambda b, i: (0, 0, b * nq + i)),
            pl.BlockSpec((seq, IDX_DIM), lambda b, i: (b, 0)),
            pl.BlockSpec((N_IDX_HEADS, tq), lambda b, i: (0, b * nq + i)),
        ],
        out_specs=pl.BlockSpec((1, 1, seq, tq), lambda b, i: (b, i // per, 0, i % per)),
        out_shape=jax.ShapeDtypeStruct((batch, seq // tq_attn, seq, tq_attn), BF16),
        scratch_shapes=[pltpu.VMEM((seq, tq), I32), pltpu.VMEM((seq, tq), BF16)],
        compiler_params=_params(("arbitrary", "arbitrary")),
        name="topk_mask",
    )(qit, ki, wit)


_HP = 1


def _attn_kernel(qi_ref, kj_ref, qt_ref, k_ref, vt_ref, bias_ref, o_ref,
                 m_ref, l_ref, acc_ref, sa_ref, sb_ref, ma_ref, mb_ref, bf_ref, *, tq, tk):
    n = pl.program_id(1)
    i = qi_ref[n]
    kt = kj_ref[n]
    last = (i * tq + tq + tk - 1) // tk - 1
    nstage = N_HEADS // _HP

    @pl.when(kt == 0)
    def _():
        m_ref[...] = jnp.full(m_ref.shape, NEG_BIAS, F32)
        l_ref[...] = jnp.zeros(l_ref.shape, F32)
        acc_ref[...] = jnp.zeros(acc_ref.shape, F32)

    bf_ref[...] = bias_ref[0, 0].astype(F32)

    def scores(j, buf, mx):
        for r in range(_HP):
            h = j * _HP + r
            sb = jnp.dot(k_ref[h // GROUP], qt_ref[h], preferred_element_type=F32) + bf_ref[...]
            buf[r] = sb
            mx[r] = jnp.max(sb, axis=0, keepdims=True)

    def softmax_pv(j, buf, mx):
        for r in range(_HP):
            h = j * _HP + r
            m_prev = m_ref[h]
            m_new = jnp.maximum(m_prev, mx[r])
            alpha = jnp.exp2(m_prev - m_new)
            p = jnp.exp2(buf[r] - m_new).astype(BF16)
            pv = jnp.dot(vt_ref[h // GROUP], p, preferred_element_type=F32)
            l_ref[h] = alpha * l_ref[h] + pv[HEAD_DIM:HEAD_DIM + 1]
            acc_ref[h] = alpha * acc_ref[h] + pv[:HEAD_DIM]
            m_ref[h] = m_new

    scores(0, sa_ref, ma_ref)

    def two_stages(jj, carry):
        j = 2 * jj
        scores(j + 1, sb_ref, mb_ref)
        softmax_pv(j, sa_ref, ma_ref)
        scores(j + 2, sa_ref, ma_ref)
        softmax_pv(j + 1, sb_ref, mb_ref)
        return carry

    lax.fori_loop(0, nstage // 2 - 1, two_stages, 0)
    scores(nstage - 1, sb_ref, mb_ref)
    softmax_pv(nstage - 2, sa_ref, ma_ref)
    softmax_pv(nstage - 1, sb_ref, mb_ref)

    @pl.when(kt == last)
    def _():
        for h in range(N_HEADS):
            o_ref[:, HEAD_DIM * h:HEAD_DIM * (h + 1)] = (acc_ref[h] / l_ref[h]).T.astype(BF16)


def _attention(qt, k, vt, bias, batch, seq, tq, tk):
    nq = seq // tq
    nk = seq // tk
    t = batch * seq
    pairs = [(i, j) for i in range(nq) for j in range((i * tq + tq + tk - 1) // tk)]
    qi = jnp.asarray([p[0] for p in pairs], I32)
    kj = jnp.asarray([p[1] for p in pairs], I32)
    kern = functools.partial(_attn_kernel, tq=tq, tk=tk)
    grid_spec = pltpu.PrefetchScalarGridSpec(
        num_scalar_prefetch=2,
        grid=(batch, len(pairs)),
        in_specs=[
            pl.BlockSpec((N_HEADS, HEAD_DIM, tq), lambda b, n, qi, kj: (0, 0, b * nq + qi[n])),
            pl.BlockSpec((N_KV_HEADS, tk, HEAD_DIM), lambda b, n, qi, kj: (0, b * nk + kj[n], 0)),
            pl.BlockSpec((N_KV_HEADS, V_ROWS, tk), lambda b, n, qi, kj: (0, 0, b * nk + kj[n])),
            pl.BlockSpec((1, 1, tk, tq), lambda b, n, qi, kj: (b, qi[n], kj[n], 0)),
        ],
        out_specs=pl.BlockSpec((tq, Q_DIM), lambda b, n, qi, kj: (b * nq + qi[n], 0)),
        scratch_shapes=[
            pltpu.VMEM((N_HEADS, 1, tq), F32),
            pltpu.VMEM((N_HEADS, 1, tq), F32),
            pltpu.VMEM((N_HEADS, HEAD_DIM, tq), F32),
            pltpu.VMEM((_HP, tk, tq), F32),
            pltpu.VMEM((_HP, tk, tq), F32),
            pltpu.VMEM((_HP, 1, tq), F32),
            pltpu.VMEM((_HP, 1, tq), F32),
            pltpu.VMEM((tk, tq), F32),
        ],
    )
    return pl.pallas_call(
        kern,
        grid_spec=grid_spec,
        out_shape=jax.ShapeDtypeStruct((t, Q_DIM), BF16),
        compiler_params=_params(("arbitrary", "arbitrary")),
        name="masked_attention",
    )(qi, kj, qt, k, vt, bias)


def _proj_residual_kernel(h_ref, x_ref, w_ref, *rest, has_bias, tn):
    if has_bias:
        b_ref, o_ref = rest
    else:
        (o_ref,) = rest
    x = x_ref[...]
    for c in range(w_ref.shape[1] // tn):
        cols = slice(c * tn, (c + 1) * tn)
        y = h_ref[:, cols] + jnp.dot(x, w_ref[:, cols], preferred_element_type=F32)
        if has_bias:
            y = y + b_ref[:, cols]
        o_ref[:, cols] = y


def _proj_residual(h, x, w, b, tm, name):
    t, d = h.shape
    kdim = x.shape[1]
    row = lambda i: (i, 0)
    in_specs = [pl.BlockSpec((tm, d), row), pl.BlockSpec((tm, kdim), row), _resident(w.shape)]
    args = [h, x, w]
    if b is not None:
        in_specs.append(_resident((1, d)))
        args.append(b)
    kern = functools.partial(_proj_residual_kernel, has_bias=b is not None, tn=512)
    return pl.pallas_call(
        kern,
        grid=(t // tm,),
        in_specs=in_specs,
        out_specs=pl.BlockSpec((tm, d), row),
        out_shape=jax.ShapeDtypeStruct((t, d), F32),
        compiler_params=_params(("arbitrary",)),
        name=name,
    )(*args)


def _mlp_kernel(h_ref, g_ref, w1_ref, w2_ref, o_ref, u_ref):
    @pl.when(pl.program_id(1) == 0)
    def _():
        h = h_ref[...]
        u_ref[...] = _rms(h, g_ref[...]).astype(BF16)
        o_ref[...] = h

    a = jnp.dot(u_ref[...], w1_ref[...], preferred_element_type=F32)
    a = jnp.square(jnp.maximum(a, 0.0)).astype(BF16)
    o_ref[...] += jnp.dot(a, w2_ref[...], preferred_element_type=F32)


def _mlp(h, g, w1, w2, layer, tm, tf):
    t, d = h.shape
    f = w1.shape[2]
    return pl.pallas_call(
        _mlp_kernel,
        grid=(t // tm, f // tf),
        in_specs=[
            pl.BlockSpec((tm, d), lambda i, j: (i, 0)),
            _resident((1, d)),
            pl.BlockSpec((None, d, tf), lambda i, j: (layer, 0, j)),
            pl.BlockSpec((None, tf, d), lambda i, j: (layer, j, 0)),
        ],
        out_specs=pl.BlockSpec((tm, d), lambda i, j: (i, 0)),
        out_shape=jax.ShapeDtypeStruct((t, d), F32),
        scratch_shapes=[pltpu.VMEM((tm, d), BF16)],
        compiler_params=_params(("arbitrary", "arbitrary")),
        name="sqrelu_mlp",
    )(h, g, w1, w2)


def _pe_kernel(h_ref, g_ref, wg_ref, p_ref, wp_ref, *rest, final, tn):
    if final:
        gf_ref, o_ref = rest
    else:
        (o_ref,) = rest
    h = h_ref[...]
    u = _rms(h, g_ref[...]).astype(BF16)
    pb = p_ref[...].astype(BF16)
    for c in range(wg_ref.shape[1] // tn):
        cols = slice(c * tn, (c + 1) * tn)
        gate = jax.nn.sigmoid(jnp.dot(u, wg_ref[:, cols], preferred_element_type=F32))
        emb = jnp.dot(pb, wp_ref[:, cols], preferred_element_type=F32)
        o_ref[:, cols] = h_ref[:, cols] + emb * gate
    if final:
        o_ref[...] = _rms(o_ref[...], gf_ref[...])


def _pe(h, g, wg, p, layer, wp, gf, tm):
    t, d = h.shape
    row = lambda i: (i, 0)
    in_specs = [pl.BlockSpec((tm, d), row), _resident((1, d)), _resident(wg.shape),
                pl.BlockSpec((None, tm, p.shape[2]), lambda r: (layer, r, 0)), _resident(wp.shape)]
    args = [h, g, wg, p, wp]
    if gf is not None:
        in_specs.append(_resident((1, d)))
        args.append(gf)
    kern = functools.partial(_pe_kernel, final=gf is not None, tn=512)
    return pl.pallas_call(
        kern,
        grid=(t // tm,),
        in_specs=in_specs,
        out_specs=pl.BlockSpec((tm, d), row),
        out_shape=jax.ShapeDtypeStruct((t, d), F32),
        compiler_params=_params(("arbitrary",)),
        name="gated_embedding",
    )(*args)


def _conv_in_kernel(h_ref, g_ref, w_ref, b_ref, y_ref, *, tn):
    d = y_ref.shape[1]
    u = _rms(h_ref[...], g_ref[...]).astype(BF16)
    for c in range(d // tn):
        a = jnp.dot(u, w_ref[:, c * tn:(c + 1) * tn], preferred_element_type=F32) + b_ref[:, c * tn:(c + 1) * tn]
        gt = (jnp.dot(u, w_ref[:, d + c * tn:d + (c + 1) * tn], preferred_element_type=F32)
              + b_ref[:, d + c * tn:d + (c + 1) * tn])
        y_ref[:, c * tn:(c + 1) * tn] = a * jax.nn.sigmoid(gt)


def _conv_in(h, g, w, b, tm):
    t, d = h.shape
    row = lambda i: (i, 0)
    return pl.pallas_call(
        functools.partial(_conv_in_kernel, tn=512),
        grid=(t // tm,),
        in_specs=[pl.BlockSpec((tm, d), row), _resident((1, d)), _resident(w.shape), _resident(b.shape)],
        out_specs=pl.BlockSpec((tm, d), row),
        out_shape=jax.ShapeDtypeStruct((t, d), F32),
        compiler_params=_params(("arbitrary",)),
        name="conv_in_glu",
    )(h, g, w, b)


_HALO = 32
_CONV_RS = 64


def _dwconv_kernel(y_ref, halo_ref, w_ref, bdw_ref, lg_ref, lb_ref, z_ref, ext_ref, cv_ref, *, tm):
    d = y_ref.shape[2]
    nc = d // LANES
    first = pl.program_id(1) == 0
    for c in range(nc):
        cols = slice(c * LANES, (c + 1) * LANES)
        ext_ref[c, 0:_HALO] = jnp.where(first, 0.0, halo_ref[0, :, cols])
        ext_ref[c, _HALO:_HALO + tm] = y_ref[0, :, cols]

    off = _HALO - (CONV_WIDTH - 1)

    def chunk(c, carry):
        for rs in range(tm // _CONV_RS):
            r0 = rs * _CONV_RS
            acc = jnp.broadcast_to(bdw_ref[c], (_CONV_RS, LANES))
            for j in range(CONV_WIDTH):
                acc = acc + w_ref[c, j:j + 1, :] * ext_ref[c, r0 + off + j:r0 + off + j + _CONV_RS, :]
            cv_ref[c, r0:r0 + _CONV_RS] = acc
        return carry

    lax.fori_loop(0, nc, chunk, 0)

    s1 = cv_ref[0]
    for c in range(1, nc):
        s1 = s1 + cv_ref[c]
    mu = jnp.sum(s1, axis=1, keepdims=True) / d
    s2 = jnp.square(cv_ref[0] - mu)
    for c in range(1, nc):
        s2 = s2 + jnp.square(cv_ref[c] - mu)
    rstd = lax.rsqrt(jnp.sum(s2, axis=1, keepdims=True) / d + NORM_EPS)
    for c in range(nc):
        yn = (cv_ref[c] - mu) * rstd * lg_ref[c] + lb_ref[c]
        z_ref[0, :, c * LANES:(c + 1) * LANES] = (yn * jax.nn.sigmoid(yn)).astype(BF16)


def _dwconv_ln_swish(y3, w3, bdw3, lg3, lb3, tm):
    batch, seq, d = y3.shape
    nc = d // LANES
    hb = tm // _HALO
    kern = functools.partial(_dwconv_kernel, tm=tm)
    return pl.pallas_call(
        kern,
        grid=(batch, seq // tm),
        in_specs=[
            pl.BlockSpec((1, tm, d), lambda b, i: (b, i, 0)),
            pl.BlockSpec((1, _HALO, d), lambda b, i: (b, jnp.maximum(i * hb - 1, 0), 0)),
            _resident(w3.shape), _resident(bdw3.shape), _resident(lg3.shape), _resident(lb3.shape),
        ],
        out_specs=pl.BlockSpec((1, tm, d), lambda b, i: (b, i, 0)),
        out_shape=jax.ShapeDtypeStruct((batch, seq, d), BF16),
        scratch_shapes=[
            pltpu.VMEM((nc, _HALO + tm, LANES), F32),
            pltpu.VMEM((nc, tm, LANES), F32),
        ],
        compiler_params=_params(("arbitrary", "arbitrary")),
        name="dwconv_ln_swish",
    )(y3, y3, w3, bdw3, lg3, lb3)


def _rope_tables(seq):
    rd = HEAD_DIM // ROPE_FRAC
    half = rd // 2
    pos = jnp.arange(seq, dtype=F32)
    inv = ROPE_THETA ** (-jnp.arange(half, dtype=F32) * 2.0 / rd)
    ang = pos[:, None] * inv[None, :]
    cos = jnp.cos(ang)
    sin = jnp.sin(ang)
    ones = jnp.ones((seq, HEAD_DIM - rd), F32)
    zeros = jnp.zeros((seq, HEAD_DIM - rd), F32)
    zh = jnp.zeros((seq, half), F32)
    cos_t = jnp.concatenate([cos, cos, ones], axis=1)
    sa_t = jnp.concatenate([-sin, zh, zeros], axis=1)
    sb_t = jnp.concatenate([zh, sin, zeros], axis=1)
    return cos_t, sa_t, sb_t


def _chunked(v, nc):
    v2 = v.reshape(-1, v.shape[-1])
    return v2.reshape(v2.shape[0], nc, LANES).transpose(1, 0, 2)


def kernel(x, p, mix_norm, mlp_norm, mlp_w1, mlp_w2, pe_proj, pe_gate_norm, pe_gate, dsa_w_in, dsa_w_out, conv_w_in, conv_b_in, conv_w_dw, conv_b_dw, conv_ln_g, conv_ln_b, conv_w_out, conv_b_out, final_norm):
    batch, seq, d = x.shape
    depth = p.shape[0]
    t = batch * seq
    topk = min(TOPK_MAX, seq // KEY_FRAC)
    tm = 512
    tm_big = min(1024, t)
    h = x.reshape(t, d)
    row_vec = lambda v: v.reshape(1, -1)
    w1_all = mlp_w1.astype(BF16)
    w2_all = mlp_w2.astype(BF16)

    for i in range(depth):
        j = i // 2
        if i % 2 == 0:
            w = dsa_w_in[j]
            wq, wk, wv, wqi, wki, wwi = jnp.split(
                w, [Q_DIM, Q_DIM + KV_DIM, Q_DIM + 2 * KV_DIM, Q_DIM + 2 * KV_DIM + IDXQ_DIM,
                    Q_DIM + 2 * KV_DIM + IDXQ_DIM + IDX_DIM], axis=1)
            wwi = jnp.pad(wwi, ((0, 0), (0, LANES - N_IDX_HEADS)))
            w_all = jnp.concatenate([wq, wk, wqi, wv, wki, wwi], axis=1).astype(BF16)
            cos_t, sa_t, sb_t = _rope_tables(seq)
            qt, k, vt, qit, ki, wit = _dsa_proj(h, row_vec(mix_norm[i]), w_all, cos_t, sa_t, sb_t, seq, tm)
            tq_attn = min(1024, seq)
            bias = _topk_mask(qit, ki, wit, batch, seq, topk, tq=256, tk=512, tq_attn=tq_attn)
            o = _attention(qt, k, vt, bias, batch, seq, tq=tq_attn, tk=min(1024, seq))
            h = _proj_residual(h, o, dsa_w_out[j].astype(BF16), None, tm_big, "dsa_out")
        else:
            nc = d // LANES
            y = _conv_in(h, row_vec(mix_norm[i]), conv_w_in[j].astype(BF16), row_vec(conv_b_in[j]), tm_big)
            w3 = _chunked(jnp.pad(conv_w_dw[j], ((0, _HALO - CONV_WIDTH), (0, 0))), nc)
            z = _dwconv_ln_swish(y.reshape(batch, seq, d), w3, _chunked(conv_b_dw[j], nc),
                                 _chunked(conv_ln_g[j], nc), _chunked(conv_ln_b[j], nc), tm=512)
            h = _proj_residual(h, z.reshape(t, d), conv_w_out[j].astype(BF16), row_vec(conv_b_out[j]), tm_big,
                               "conv_out")
        h = _mlp(h, row_vec(mlp_norm[i]), w1_all, w2_all, i, tm=512, tf=2048)
        gf = row_vec(final_norm) if i == depth - 1 else None
        h = _pe(h, row_vec(pe_gate_norm[i]), pe_gate[i].astype(BF16), p.reshape(depth, t, -1), i,
                pe_proj[i].astype(BF16), gf, tm)
    return h.reshape(batch, seq, d)
```
